```python
import math
import jax, jax.numpy as jnp
from jax import lax
import numpy as np

D_MODEL = 1024
BATCH = 16
SEQ = 2048
DEPTH = 2

N_A_LAYERS = DEPTH // 2
N_B_LAYERS = DEPTH - N_A_LAYERS
N_HEADS = 16
HEAD_DIM = D_MODEL // N_HEADS
Q_BLOCK = 128
CONV_WIDTH = 31
N_EXPERTS = 32
TOP_K = 4
D_FF = D_MODEL
SWIGLU_LIMIT = 7.0
SWIGLU_ALPHA = 1.702
MOE_BLOCK = 128
ALPHA_DN = (2.0 * DEPTH) ** 0.25
BETA_DN = (8.0 * DEPTH) ** -0.25
LN_EPS = 1e-5

kernel_name = "yoco_conformer_stickbreaking_moe_adaln"


def layer_norm(x, g, b):
    xf = x.astype(jnp.float32)
    mu = jnp.mean(xf, axis=-1, keepdims=True)
    var = jnp.mean(jnp.square(xf - mu), axis=-1, keepdims=True)
    y = (xf - mu) * lax.rsqrt(var + LN_EPS) * g.astype(jnp.float32) + b.astype(jnp.float32)
    return y.astype(x.dtype)


def modulate(x, shift, scale):
    return x * (1.0 + scale[:, None, :]) + shift[:, None, :]


def conformer_conv(h, w1, b1, dw, db, lg, lb, w2, b2):
    u = h @ w1 + b1
    a, g = jnp.split(u, 2, axis=-1)
    u = a * jax.nn.sigmoid(g)
    u = lax.conv_general_dilated(
        u, dw[:, None, :], window_strides=(1,), padding=[(CONV_WIDTH - 1, 0)],
        dimension_numbers=('NWC', 'WIO', 'NWC'), feature_group_count=D_MODEL) + db
    u = jax.nn.silu(layer_norm(u, lg, lb))
    return u @ w2 + b2


def stick_breaking_attention(h, q_w, k, v, o_w):
    B, L, _ = h.shape
    q = (h @ q_w).reshape(B, L, N_HEADS, HEAD_DIM).transpose(0, 2, 1, 3)
    scale = HEAD_DIM ** -0.5
    outs = []
    for i in range(L // Q_BLOCK):
        t0, end = i * Q_BLOCK, (i + 1) * Q_BLOCK
        qb = q[:, :, t0:end]
        kb, vb = k[:, :, :end], v[:, :, :end]
        z = jnp.einsum('bhtd,bhsd->bhts', qb, kb).astype(jnp.float32) * scale
        t_pos = t0 + jnp.arange(Q_BLOCK)[:, None]
        s_pos = jnp.arange(end)[None, :]
        causal = s_pos < t_pos
        log_om = jnp.where(causal, jax.nn.log_sigmoid(-z), 0.0)
        suffix = lax.cumsum(log_om, axis=3, reverse=True) - log_om
        log_a = jax.nn.log_sigmoid(z) + suffix
        a = jnp.where(causal, jnp.exp(log_a), 0.0)
        outs.append(jnp.einsum('bhts,bhsd->bhtd', a.astype(vb.dtype), vb))
    o = jnp.concatenate(outs, axis=2)
    o = o.transpose(0, 2, 1, 3).reshape(B, L, D_MODEL)
    return o @ o_w


def clamped_swiglu(u):
    x_glu, x_lin = jnp.split(u, 2, axis=-1)
    x_glu = jnp.minimum(x_glu, SWIGLU_LIMIT)
    x_lin = jnp.clip(x_lin, -SWIGLU_LIMIT, SWIGLU_LIMIT)
    return (x_lin + 1.0) * (x_glu * jax.nn.sigmoid(SWIGLU_ALPHA * x_glu))


def moe(h, router_w, router_b, w_in, b_in, w_out, b_out):
    B, L, D = h.shape
    T = B * L
    A = T * TOP_K
    xt = h.reshape(T, D)
    logits = (xt @ router_w + router_b).astype(jnp.float32)
    top_val, top_idx = lax.top_k(logits, TOP_K)
    gates = jax.nn.softmax(top_val, axis=-1).astype(h.dtype)
    flat_e = top_idx.reshape(A).astype(jnp.int32)
    flat_tok = jnp.arange(A, dtype=jnp.int32) // TOP_K
    flat_g = gates.reshape(A)
    order = jnp.argsort(flat_e, stable=True)
    s_e, s_tok, s_g = flat_e[order], flat_tok[order], flat_g[order]
    counts = jnp.bincount(flat_e, length=N_EXPERTS).astype(jnp.int32)
    starts = jnp.cumsum(counts) - counts
    padded = ((counts + MOE_BLOCK - 1) // MOE_BLOCK) * MOE_BLOCK
    pend = jnp.cumsum(padded)
    pstart = pend - padded
    dest = pstart[s_e] + (jnp.arange(A, dtype=jnp.int32) - starts[s_e])
    n_blocks = -(-A // MOE_BLOCK) + N_EXPERTS
    R = n_blocks * MOE_BLOCK
    row_tok = jnp.zeros((R,), jnp.int32).at[dest].set(s_tok)
    row_g = jnp.zeros((R,), h.dtype).at[dest].set(s_g)
    block_start = jnp.arange(n_blocks, dtype=jnp.int32) * MOE_BLOCK
    block_e = jnp.minimum(jnp.searchsorted(pend, block_start, side='right'),
                          N_EXPERTS - 1).astype(jnp.int32)

    def expert_block(args):
        e, tok, g = args
        xe = xt[tok]
        u = xe @ w_in[e] + b_in[e]
        y = clamped_swiglu(u) @ w_out[e] + b_out[e]
        return y * g[:, None]

    yb = lax.map(expert_block, (block_e, row_tok.reshape(n_blocks, MOE_BLOCK),
                                row_g.reshape(n_blocks, MOE_BLOCK)))
    out = jnp.zeros((T, D), h.dtype).at[row_tok].add(yb.reshape(R, D))
    return out.reshape(B, L, D)


def setup_inputs(seed: int = 0) -> dict:
    key = jax.random.key(seed)
    ks = jax.random.split(key, 26)
    D, E, F = D_MODEL, N_EXPERTS, D_FF
    nrm = lambda k, s: jax.random.normal(k, s, jnp.float32)
    inv = D ** -0.5
    k_part = nrm(ks[13], (D, D)) * inv
    v_part = nrm(ks[14], (D, D)) * inv * BETA_DN
    return {
        "x": nrm(ks[0], (BATCH, SEQ, D)),
        "c": nrm(ks[1], (BATCH, D)),
        "ada_w": nrm(ks[2], (DEPTH, D, 6 * D)) * (0.1 * inv),
        "ada_b": 0.01 * nrm(ks[3], (DEPTH, 6 * D)),
        "ln_g": 1.0 + 0.01 * nrm(ks[4], (DEPTH, 2, D)),
        "ln_b": 0.01 * nrm(ks[5], (DEPTH, 2, D)),
        "cv_w1": nrm(ks[6], (N_A_LAYERS, D, 2 * D)) * inv,
        "cv_b1": 0.01 * nrm(ks[7], (N_A_LAYERS, 2 * D)),
        "cv_dw": nrm(ks[8], (N_A_LAYERS, CONV_WIDTH, D)) * CONV_WIDTH ** -0.5,
        "cv_db": 0.01 * nrm(ks[9], (N_A_LAYERS, D)),
        "cv_ln_g": 1.0 + 0.01 * nrm(ks[10], (N_A_LAYERS, D)),
        "cv_ln_b": 0.01 * nrm(ks[11], (N_A_LAYERS, D)),
        "cv_w2": nrm(ks[12], (N_A_LAYERS, D, D)) * inv * BETA_DN,
        "cv_b2": 0.01 * nrm(ks[15], (N_A_LAYERS, D)),
        "kv_w": jnp.concatenate([k_part, v_part], axis=1),
        "q_w": nrm(ks[16], (N_B_LAYERS, D, D)) * inv,
        "o_w": nrm(ks[17], (N_B_LAYERS, D, D)) * inv * BETA_DN,
        "router_w": nrm(ks[18], (DEPTH, D, E)) * inv,
        "router_b": 0.01 * nrm(ks[19], (DEPTH, E)),
        "moe_w_in": nrm(ks[20], (DEPTH, E, D, 2 * F)) * inv,
        "moe_b_in": 0.01 * nrm(ks[21], (DEPTH, E, 2 * F)),
        "moe_w_out": nrm(ks[22], (DEPTH, E, F, D)) * (F ** -0.5) * BETA_DN,
        "moe_b_out": 0.01 * nrm(ks[23], (DEPTH, E, D)),
    }


def reference(x, c, ada_w, ada_b, ln_g, ln_b, cv_w1, cv_b1, cv_dw, cv_db,
              cv_ln_g, cv_ln_b, cv_w2, cv_b2, kv_w, q_w, o_w, router_w,
              router_b, moe_w_in, moe_b_in, moe_w_out, moe_b_out):
    B, L, D = x.shape
    c_act = jax.nn.silu(c)
    k = v = None
    for l in range(DEPTH):
        mod = c_act @ ada_w[l] + ada_b[l]
        sh1, sc1, g1, sh2, sc2, g2 = jnp.split(mod, 6, axis=-1)
        h = modulate(x, sh1, sc1)
        if l < N_A_LAYERS:
            y = conformer_conv(h, cv_w1[l], cv_b1[l], cv_dw[l], cv_db[l],
                               cv_ln_g[l], cv_ln_b[l], cv_w2[l], cv_b2[l])
        else:
            j = l - N_A_LAYERS
            y = stick_breaking_attention(h, q_w[j], k, v, o_w[j])
        x = layer_norm(ALPHA_DN * x + (1.0 + g1)[:, None, :] * y, ln_g[l, 0], ln_b[l, 0])
        h = modulate(x, sh2, sc2)
        y = moe(h, router_w[l], router_b[l], moe_w_in[l], moe_b_in[l],
                moe_w_out[l], moe_b_out[l])
        x = layer_norm(ALPHA_DN * x + (1.0 + g2)[:, None, :] * y, ln_g[l, 1], ln_b[l, 1])
        if l == N_A_LAYERS - 1:
            kv = x @ kv_w
            k, v = jnp.split(kv, 2, axis=-1)
            k = k.reshape(B, L, N_HEADS, HEAD_DIM).transpose(0, 2, 1, 3)
            v = v.reshape(B, L, N_HEADS, HEAD_DIM).transpose(0, 2, 1, 3)
    return x
```

```python
import functools
import math

import jax
import jax.numpy as jnp
from jax import lax
from jax.experimental import pallas as pl
from jax.experimental.pallas import tpu as pltpu

F32 = jnp.float32
BF16 = jnp.bfloat16
HIGHEST = lax.Precision.HIGHEST

LANES = 128
N_HEADS = 16
TOP_K = 4
CONV_HALO = 32
SWIGLU_LIMIT = 7.0
SWIGLU_ALPHA = 1.702
LN_EPS = 1e-5
CHUNK = 8
VMEM_LIMIT = 56 * 1024 * 1024


def _cparams(n_axes):
    return pltpu.CompilerParams(
        dimension_semantics=("arbitrary",) * n_axes, vmem_limit_bytes=VMEM_LIMIT)


def _layer_norm(x, g, b):
    mu = jnp.mean(x, axis=-1, keepdims=True)
    xc = x - mu
    var = jnp.mean(xc * xc, axis=-1, keepdims=True)
    return xc * lax.rsqrt(var + LN_EPS) * g + b


def _silu(x):
    return x * jax.nn.sigmoid(x)


def _ada_kernel(c_ref, w_ref, b_ref, o_ref):
    o_ref[0] = jnp.dot(_silu(c_ref[...]), w_ref[0], precision=HIGHEST,
                       preferred_element_type=F32) + b_ref[0]


def _ada(c, ada_w, ada_b):
    depth, d, n = ada_w.shape
    bsz = c.shape[0]
    tn = d
    return pl.pallas_call(
        _ada_kernel,
        grid=(depth, n // tn),
        in_specs=[pl.BlockSpec((bsz, d), lambda l, j: (0, 0)),
                  pl.BlockSpec((1, d, tn), lambda l, j: (l, 0, j)),
                  pl.BlockSpec((1, 1, tn), lambda l, j: (l, 0, j))],
        out_specs=pl.BlockSpec((1, bsz, tn), lambda l, j: (l, 0, j)),
        out_shape=jax.ShapeDtypeStruct((depth, bsz, n), F32),
        compiler_params=_cparams(2),
        name="ada",
    )(c, ada_w, ada_b.reshape(depth, 1, n))


def _post_norm_route(xres, y, mod_ref, lng_ref, lnb_ref, rwt_ref, rb_ref,
                     x_out_ref, hlin_ref, logt_ref, alpha):
    tl, d = xres.shape
    g1 = mod_ref[2:3, :]
    sh2 = mod_ref[3:4, :]
    sc2 = mod_ref[4:5, :]
    x1 = _layer_norm(alpha * xres + (1.0 + g1) * y, lng_ref[...], lnb_ref[...])
    x_out_ref[...] = x1
    h2 = x1 * (1.0 + sc2) + sh2
    p = d // LANES
    for s in range(p):
        hlin_ref[pl.ds(s, tl, stride=p), :] = h2[:, s * LANES:(s + 1) * LANES]
    logt_ref[...] = lax.dot_general(
        rwt_ref[...], h2, (((1,), (1,)), ((), ())), precision=HIGHEST,
        preferred_element_type=F32) + rb_ref[...]


def _conv_kernel(x_ref, mod_ref, w1_ref, b1_ref, dw_ref, db_ref, clg_ref, clb_ref,
                 w2_ref, b2_ref, lng_ref, lnb_ref, rwt_ref, rb_ref,
                 x_out_ref, hlin_ref, logt_ref,
                 ubuf, sbuf, dwb, cbuf, *, alpha, width):
    tl, d = x_ref.shape
    first_tile = pl.program_id(1) == 0

    @pl.when(jnp.logical_and(pl.program_id(0) == 0, first_tile))
    def _():
        for j in range(width):
            dwb[j] = jnp.broadcast_to(dw_ref[j:j + 1, :], (8, d))

    x = x_ref[...]
    sh1 = mod_ref[0:1, :]
    sc1 = mod_ref[1:2, :]
    h = (x * (1.0 + sc1) + sh1).astype(BF16)
    u = jnp.dot(h, w1_ref[...], preferred_element_type=F32) + b1_ref[...]
    glu = u[:, :d] * jax.nn.sigmoid(u[:, d:])

    @pl.when(first_tile)
    def _():
        ubuf[0:CONV_HALO, :] = jnp.zeros((CONV_HALO, d), F32)

    ubuf[CONV_HALO:CONV_HALO + tl, :] = glu
    for r in range(1, 8):
        sbuf[r - 1] = ubuf[pl.ds(r, tl + CONV_HALO - 8), :]

    off0 = CONV_HALO - (width - 1)
    cw = 512
    rows = 16

    def body(i, carry):
        base = pl.multiple_of(i * rows, rows)
        for c in range(d // cw):
            cs = slice(c * cw, (c + 1) * cw)
            accs = [jnp.broadcast_to(db_ref[:, cs], (8, cw)) for _ in range(rows // 8)]
            for j in range(width):
                q, r = divmod(off0 + j, 8)
                w = dwb[j, :, cs]
                for a in range(rows // 8):
                    start = base + 8 * (q + a)
                    if r == 0:
                        v = ubuf[pl.ds(start, 8), cs]
                    else:
                        v = sbuf[r - 1, pl.ds(start, 8), cs]
                    accs[a] = accs[a] + w * v
            for a in range(rows // 8):
                cbuf[pl.ds(base + 8 * a, 8), cs] = accs[a]
        return carry

    lax.fori_loop(0, tl // rows, body, 0)
    ubuf[0:CONV_HALO, :] = ubuf[tl:tl + CONV_HALO, :]

    v = _silu(_layer_norm(cbuf[...], clg_ref[...], clb_ref[...]))
    y = jnp.dot(v.astype(BF16), w2_ref[...], preferred_element_type=F32) + b2_ref[...]
    _post_norm_route(x, y, mod_ref, lng_ref, lnb_ref, rwt_ref, rb_ref,
                     x_out_ref, hlin_ref, logt_ref, alpha)


def _conv_mixer(x2d, mod, w1, b1, dw, db, clg, clb, w2, b2, lng, lnb, rwt, rb,
                *, bsz, seq, tl, alpha):
    t, d = x2d.shape
    e = rwt.shape[0]
    width = dw.shape[0]
    p = d // LANES
    nlt = seq // tl
    row = lambda a: a.reshape(1, -1)
    full = lambda shape: pl.BlockSpec(shape, lambda b, l: (0,) * len(shape))
    return pl.pallas_call(
        functools.partial(_conv_kernel, alpha=alpha, width=width),
        grid=(bsz, nlt),
        in_specs=[pl.BlockSpec((tl, d), lambda b, l: (b * nlt + l, 0)),
                  pl.BlockSpec((None, 6, d), lambda b, l: (b, 0, 0)),
                  full((d, 2 * d)), full((1, 2 * d)), full((width, d)), full((1, d)),
                  full((1, d)), full((1, d)), full((d, d)), full((1, d)),
                  full((1, d)), full((1, d)), full((e, d)), full((e, 1))],
        out_specs=[pl.BlockSpec((tl, d), lambda b, l: (b * nlt + l, 0)),
                   pl.BlockSpec((tl * p, LANES), lambda b, l: (b * nlt + l, 0)),
                   pl.BlockSpec((e, tl), lambda b, l: (0, b * nlt + l))],
        out_shape=[jax.ShapeDtypeStruct((t, d), F32),
                   jax.ShapeDtypeStruct((t * p, LANES), F32),
                   jax.ShapeDtypeStruct((e, t), F32)],
        scratch_shapes=[pltpu.VMEM((tl + CONV_HALO, d), F32),
                        pltpu.VMEM((7, tl + CONV_HALO - 8, d), F32),
                        pltpu.VMEM((width, 8, d), F32),
                        pltpu.VMEM((tl, d), F32)],
        compiler_params=_cparams(2),
        name="conv_mixer",
    )(x2d, mod, w1, row(b1), dw, row(db), row(clg), row(clb), w2, row(b2),
      row(lng), row(lnb), rwt, rb.reshape(e, 1))


def _route_kernel(logt_ref, slot_ref, gate_ref, cnt_ref):
    e, tt = logt_ref.shape
    l = logt_ref[...]
    iota_e = lax.broadcasted_iota(jnp.int32, (e, tt), 0).astype(F32)
    sels, vals = [], []
    for _ in range(TOP_K):
        m = jnp.max(l, axis=0, keepdims=True)
        idx = jnp.min(jnp.where(l == m, iota_e, float(e)), axis=0, keepdims=True)
        sels.append(idx)
        vals.append(m)
        l = jnp.where(iota_e == idx, -jnp.inf, l)
    exps = [jnp.exp(v - vals[0]) for v in vals]
    denom = exps[0] + exps[1] + exps[2] + exps[3]
    onehots = [iota_e == s for s in sels]
    member = jnp.zeros((e, tt), F32)
    for oh in onehots:
        member = member + jnp.where(oh, 1.0, 0.0)
    tri = jnp.where(lax.broadcasted_iota(jnp.int32, (tt, tt), 0)
                    < lax.broadcasted_iota(jnp.int32, (tt, tt), 1), 1.0, 0.0).astype(BF16)
    rank = jnp.dot(member.astype(BF16), tri, preferred_element_type=F32)
    cnt = jnp.sum(member, axis=1, keepdims=True)
    padded = jnp.floor((cnt + (CHUNK - 1)) * (1.0 / CHUNK)) * CHUNK
    low = jnp.where(lax.broadcasted_iota(jnp.int32, (e, e), 1)
                    < lax.broadcasted_iota(jnp.int32, (e, e), 0), 1.0, 0.0)
    loff = jnp.dot(low, jnp.broadcast_to(padded, (e, tt)), precision=HIGHEST,
                   preferred_element_type=F32)
    pos = loff + rank
    for k in range(TOP_K):
        slot = jnp.sum(jnp.where(onehots[k], pos, 0.0), axis=0, keepdims=True)
        slot_ref[0, k:k + 1, :] = slot.astype(jnp.int32)
        gate_ref[0, k:k + 1, :] = exps[k] / denom
    cnt_ref[0] = jnp.broadcast_to(cnt, (e, LANES)).astype(jnp.int32)


def _route(logt, tt):
    e, t = logt.shape
    nt = t // tt
    return pl.pallas_call(
        _route_kernel,
        grid=(nt,),
        in_specs=[pl.BlockSpec((e, tt), lambda i: (0, i))],
        out_specs=[pl.BlockSpec((1, TOP_K, tt), lambda i: (i, 0, 0)),
                   pl.BlockSpec((1, TOP_K, tt), lambda i: (i, 0, 0)),
                   pl.BlockSpec((1, e, LANES), lambda i: (i, 0, 0))],
        out_shape=[jax.ShapeDtypeStruct((nt, TOP_K, tt), jnp.int32),
                   jax.ShapeDtypeStruct((nt, TOP_K, tt), F32),
                   jax.ShapeDtypeStruct((nt, e, LANES), jnp.int32)],
        compiler_params=_cparams(1),
        name="route",
    )(logt)


def _plan(cnt, rb, nb):
    nt, e = cnt.shape
    padded = ((cnt + CHUNK - 1) // CHUNK) * CHUNK
    loff = jnp.cumsum(padded, axis=1) - padded
    rows_e = jnp.sum(padded, axis=0)
    blocks_e = (rows_e + rb - 1) // rb
    bend = jnp.cumsum(blocks_e)
    pstart = (bend - blocks_e) * rb
    goff = pstart[None, :] + jnp.cumsum(padded, axis=0) - padded
    nchunks = jnp.sum(padded, axis=1) // CHUNK
    block_e = jnp.minimum(
        jnp.searchsorted(bend, jnp.arange(nb, dtype=jnp.int32), side="right"), e - 1)
    tail_start = pstart + rows_e
    tail_chunks = (blocks_e * rb - rows_e) // CHUNK
    i32 = lambda a: a.astype(jnp.int32)
    return (i32(padded.reshape(-1)), i32(loff.reshape(-1)), i32(goff.reshape(-1)),
            i32(nchunks), i32(tail_start), i32(tail_chunks), i32(block_e), i32(bend[-1:]))


def _chunk_copies(padded_ref, loff_ref, goff_ref, tile, n_experts, p, make):
    def per_expert(ex, carry):
        idx = tile * n_experts + ex
        n = padded_ref[idx] // CHUNK
        lo = loff_ref[idx]
        go = goff_ref[idx]

        def per_chunk(j, c):
            lrow = pl.multiple_of((lo + j * CHUNK) * p, CHUNK * p)
            grow = pl.multiple_of((go + j * CHUNK) * p, CHUNK * p)
            make(lrow, grow).start()
            return c

        return lax.fori_loop(0, n, per_chunk, carry)

    lax.fori_loop(0, n_experts, per_expert, 0)


def _dispatch_kernel(padded_ref, loff_ref, goff_ref, nch_ref, tstart_ref, tchunks_ref,
                     x_ref, slot_ref, xs_ref, stage, sem, *, n_experts):
    tile = pl.program_id(0)
    p = x_ref.shape[0] // slot_ref.shape[2]
    tt = slot_ref.shape[2]
    rows = CHUNK * p
    make = lambda lrow, grow: pltpu.make_async_copy(
        stage.at[pl.ds(lrow, rows), :], xs_ref.at[pl.ds(grow, rows), :], sem)
    stage[...] = jnp.zeros(stage.shape, F32)

    @pl.when(tile == 0)
    def _():
        def per_expert(ex, total):
            n = tchunks_ref[ex]

            def per_chunk(j, c):
                grow = pl.multiple_of((tstart_ref[ex] + j * CHUNK) * p, rows)
                make(0, grow).start()
                return c

            lax.fori_loop(0, n, per_chunk, 0)
            return total + n

        total = lax.fori_loop(0, n_experts, per_expert, 0)

        def wait_tail(j, c):
            make(0, 0).wait()
            return c

        lax.fori_loop(0, total, wait_tail, 0)

    def per_token(t, carry):
        v = x_ref[pl.ds(pl.multiple_of(t * p, p), p), :]
        for k in range(TOP_K):
            s = slot_ref[0, k, t]
            stage[pl.ds(pl.multiple_of(s * p, p), p), :] = v
        return carry

    lax.fori_loop(0, tt, per_token, 0)

    _chunk_copies(padded_ref, loff_ref, goff_ref, tile, n_experts, p, make)

    def wait_one(j, c):
        make(0, 0).wait()
        return c

    lax.fori_loop(0, nch_ref[tile], wait_one, 0)


def _dispatch(hlin, slots, tables, tails, *, n_rows, n_experts, d):
    padded, loff, goff, nch = tables
    tstart, tchunks = tails
    nt, _, tt = slots.shape
    p = d // LANES
    stage_rows = TOP_K * tt + n_experts * (CHUNK - 1)
    stage_rows = -(-stage_rows // CHUNK) * CHUNK
    return pl.pallas_call(
        functools.partial(_dispatch_kernel, n_experts=n_experts),
        grid_spec=pltpu.PrefetchScalarGridSpec(
            num_scalar_prefetch=6,
            grid=(nt,),
            in_specs=[pl.BlockSpec((tt * p, LANES), lambda i, *_: (i, 0)),
                      pl.BlockSpec((1, TOP_K, tt), lambda i, *_: (i, 0, 0),
                                   memory_space=pltpu.SMEM)],
            out_specs=pl.BlockSpec(memory_space=pl.ANY),
            scratch_shapes=[pltpu.VMEM((stage_rows * p, LANES), F32),
                            pltpu.SemaphoreType.DMA(())]),
        out_shape=jax.ShapeDtypeStruct((n_rows * p, LANES), F32),
        compiler_params=_cparams(1),
        name="dispatch",
    )(padded, loff, goff, nch, tstart, tchunks, hlin, slots)


def _expert_kernel(be_ref, nu_ref, x_ref, win_ref, bin_ref, wout_ref, bout_ref, y_ref,
                   win_bf, wout_bf):
    b = pl.program_id(0)
    rb = x_ref.shape[0] * LANES // win_ref.shape[1]
    d = win_ref.shape[1]
    f = wout_ref.shape[1]
    p = d // LANES

    @pl.when(b < nu_ref[0])
    def _():
        prev = be_ref[jnp.maximum(b - 1, 0)]

        @pl.when(jnp.logical_or(b == 0, be_ref[b] != prev))
        def _():
            win_bf[...] = win_ref[0].astype(BF16)
            wout_bf[...] = wout_ref[0].astype(BF16)

        x = jnp.concatenate([x_ref[pl.ds(s, rb, stride=p), :] for s in range(p)], axis=-1)
        u = jnp.dot(x.astype(BF16), win_bf[...], preferred_element_type=F32) + bin_ref[0]
        x_glu = jnp.minimum(u[:, :f], SWIGLU_LIMIT)
        x_lin = jnp.clip(u[:, f:], -SWIGLU_LIMIT, SWIGLU_LIMIT)
        act = (x_lin + 1.0) * (x_glu * jax.nn.sigmoid(SWIGLU_ALPHA * x_glu))
        y = jnp.dot(act.astype(BF16), wout_bf[...], preferred_element_type=F32) + bout_ref[0]
        for s in range(p):
            y_ref[pl.ds(s, rb, stride=p), :] = y[:, s * LANES:(s + 1) * LANES]


def _experts(xs, block_e, n_used, w_in, b_in, w_out, b_out, *, rb, nb):
    e, d, f2 = w_in.shape
    f = f2 // 2
    p = d // LANES
    blk = lambda b, be, nu: (jnp.minimum(b, nu[0] - 1), 0)
    wsel = lambda b, be, nu: (be[jnp.minimum(b, nu[0] - 1)], 0, 0)
    return pl.pallas_call(
        _expert_kernel,
        grid_spec=pltpu.PrefetchScalarGridSpec(
            num_scalar_prefetch=2,
            grid=(nb,),
            in_specs=[pl.BlockSpec((rb * p, LANES), blk),
                      pl.BlockSpec((1, d, f2), wsel),
                      pl.BlockSpec((1, 1, f2), wsel),
                      pl.BlockSpec((1, f, d), wsel),
                      pl.BlockSpec((1, 1, d), wsel)],
            out_specs=pl.BlockSpec((rb * p, LANES), blk),
            scratch_shapes=[pltpu.VMEM((d, f2), BF16), pltpu.VMEM((f, d), BF16)]),
        out_shape=jax.ShapeDtypeStruct(xs.shape, F32),
        compiler_params=_cparams(1),
        name="experts",
    )(block_e, n_used, xs, w_in, b_in.reshape(e, 1, f2), w_out, b_out.reshape(e, 1, d))


def _combine_kernel(padded_ref, loff_ref, goff_ref, nch_ref, ys_ref, slot_ref, gate_ref,
                    xres_ref, mod_ref, lng_ref, lnb_ref, x_out_ref, stage, comb, sem,
                    *, n_experts, alpha):
    tile = pl.program_id(0)
    tt, d = xres_ref.shape
    p = d // LANES
    rows = CHUNK * p
    make = lambda lrow, grow: pltpu.make_async_copy(
        ys_ref.at[pl.ds(grow, rows), :], stage.at[pl.ds(lrow, rows), :], sem)
    _chunk_copies(padded_ref, loff_ref, goff_ref, tile, n_experts, p, make)

    def wait_one(j, c):
        make(0, 0).wait()
        return c

    lax.fori_loop(0, nch_ref[tile], wait_one, 0)

    def per_token(t, carry):
        acc = None
        for k in range(TOP_K):
            s = slot_ref[0, k, t]
            term = gate_ref[0, k, t] * stage[pl.ds(pl.multiple_of(s * p, p), p), :]
            acc = term if acc is None else acc + term
        comb[pl.ds(pl.multiple_of(t * p, p), p), :] = acc
        return carry

    lax.fori_loop(0, tt, per_token, 0)
    moe = jnp.concatenate([comb[pl.ds(s, tt, stride=p), :] for s in range(p)], axis=-1)
    g2 = mod_ref[5:6, :]
    x_out_ref[...] = _layer_norm(alpha * xres_ref[...] + (1.0 + g2) * moe,
                                 lng_ref[...], lnb_ref[...])


def _combine(ys, slots, gates, tables, xres, mod, lng, lnb, *, n_experts, seq, alpha):
    padded, loff, goff, nch = tables
    nt, _, tt = slots.shape
    t, d = xres.shape
    p = d // LANES
    tiles_per_seq = seq // tt
    stage_rows = TOP_K * tt + n_experts * (CHUNK - 1)
    stage_rows = -(-stage_rows // CHUNK) * CHUNK
    smem = lambda: pl.BlockSpec((1, TOP_K, tt), lambda i, *_: (i, 0, 0),
                                memory_space=pltpu.SMEM)
    return pl.pallas_call(
        functools.partial(_combine_kernel, n_experts=n_experts, alpha=alpha),
        grid_spec=pltpu.PrefetchScalarGridSpec(
            num_scalar_prefetch=4,
            grid=(nt,),
            in_specs=[pl.BlockSpec(memory_space=pl.ANY), smem(), smem(),
                      pl.BlockSpec((tt, d), lambda i, *_: (i, 0)),
                      pl.BlockSpec((None, 6, d), lambda i, *_: (i // tiles_per_seq, 0, 0)),
                      pl.BlockSpec((1, d), lambda i, *_: (0, 0)),
                      pl.BlockSpec((1, d), lambda i, *_: (0, 0))],
            out_specs=pl.BlockSpec((tt, d), lambda i, *_: (i, 0)),
            scratch_shapes=[pltpu.VMEM((stage_rows * p, LANES), F32),
                            pltpu.VMEM((tt * p, LANES), F32),
                            pltpu.SemaphoreType.DMA(())]),
        out_shape=jax.ShapeDtypeStruct((t, d), F32),
        compiler_params=_cparams(1),
        name="combine",
    )(padded, loff, goff, nch, ys, slots, gates, xres, mod, lng.reshape(1, d), lnb.reshape(1, d))


def _moe(hlin, logt, xres, mod, lng, lnb, w_in, b_in, w_out, b_out, *, seq, tt, rb, alpha):
    e, t = logt.shape
    d = xres.shape[1]
    nt = t // tt
    slots, gates, cnt = _route(logt, tt)
    max_rows = TOP_K * t + nt * e * (CHUNK - 1)
    nb = -(-max_rows // rb) + e
    padded, loff, goff, nch, tstart, tchunks, block_e, n_used = _plan(cnt[:, :, 0], rb, nb)
    tables = (padded, loff, goff, nch)
    xs = _dispatch(hlin, slots, tables, (tstart, tchunks), n_rows=nb * rb, n_experts=e, d=d)
    ys = _experts(xs, block_e, n_used, w_in, b_in, w_out, b_out, rb=rb, nb=nb)
    return _combine(ys, slots, gates, tables, xres, mod, lng, lnb,
                    n_experts=e, seq=seq, alpha=alpha)


def _qkv_kernel(x_ref, mod_ref, qwt_ref, kw_ref, vwt_ref, qt_ref, k_ref, vt_ref):
    x = x_ref[...]
    sh1 = mod_ref[0:1, :]
    sc1 = mod_ref[1:2, :]
    h = (x * (1.0 + sc1) + sh1).astype(BF16)
    xb = x.astype(BF16)
    nt_dims = (((1,), (1,)), ((), ()))
    qt_ref[0] = lax.dot_general(qwt_ref[...], h, nt_dims,
                                preferred_element_type=F32).astype(BF16)
    k_ref[...] = jnp.dot(xb, kw_ref[...], preferred_element_type=F32).astype(BF16)
    vt = lax.dot_general(vwt_ref[...], xb, nt_dims, preferred_element_type=F32).astype(BF16)
    tk = vt_ref.shape[3]
    for j in range(vt_ref.shape[1]):
        vt_ref[0, j] = vt[:, j * tk:(j + 1) * tk]


def _qkv(x2d, mod, qwt, kw, vwt, *, bsz, seq, tl, tk):
    t, d = x2d.shape
    nlt = seq // tl
    full = pl.BlockSpec((d, d), lambda b, l: (0, 0))
    return pl.pallas_call(
        _qkv_kernel,
        grid=(bsz, nlt),
        in_specs=[pl.BlockSpec((tl, d), lambda b, l: (b * nlt + l, 0)),
                  pl.BlockSpec((None, 6, d), lambda b, l: (b, 0, 0)),
                  full, full, full],
        out_specs=[pl.BlockSpec((1, d, tl), lambda b, l: (b, 0, l)),
                   pl.BlockSpec((tl, d), lambda b, l: (b * nlt + l, 0)),
                   pl.BlockSpec((1, tl // tk, d, tk), lambda b, l: (b, l, 0, 0))],
        out_shape=[jax.ShapeDtypeStruct((bsz, d, seq), BF16),
                   jax.ShapeDtypeStruct((t, d), BF16),
                   jax.ShapeDtypeStruct((bsz, seq // tk, d, tk), BF16)],
        compiler_params=_cparams(2),
        name="qkv",
    )(x2d, mod, qwt, kw, vwt)


def _softplus(z):
    return jnp.maximum(z, 0.0) + jnp.log(1.0 + jnp.exp(-jnp.abs(z)))


def _attn_block(zt, vt_h, carry, t_pos, s0, masked):
    tk, tq = zt.shape
    sub = lax.broadcasted_iota(jnp.int32, (8, tq), 0)
    a_groups = [None] * (tk // 8)
    for g in reversed(range(tk // 8)):
        z = zt[g * 8:(g + 1) * 8, :]
        sp = _softplus(z)
        if masked:
            causal = (s0 + g * 8 + sub) < t_pos
            lo = jnp.where(causal, -sp, 0.0)
        else:
            lo = -sp
        s = lo
        for k in (1, 2, 4):
            s = s + jnp.where(sub < 8 - k, pltpu.roll(s, 8 - k, axis=0), 0.0)
        log_a = (z - sp) + (s - lo) + carry
        a = jnp.exp(log_a)
        if masked:
            a = jnp.where(causal, a, 0.0)
        a_groups[g] = a.astype(BF16)
        carry = carry + s[0:1, :]
    a_full = jnp.concatenate(a_groups, axis=0)
    return jnp.dot(vt_h, a_full, preferred_element_type=F32), carry


def _attn_kernel(qt_ref, k_ref, vt_ref, o_ref, *, tk, head_dim):
    qi = pl.program_id(2)
    tq = qt_ref.shape[2]
    hp = qt_ref.shape[1] // head_dim
    row = lax.broadcasted_iota(jnp.int32, (hp * head_dim, tq), 0)
    t_pos = qi * tq + lax.broadcasted_iota(jnp.int32, (8, tq), 1)
    n_diag = tq // tk
    outs = []
    for hh in range(hp):
        in_head = jnp.logical_and(row >= hh * head_dim, row < (hh + 1) * head_dim)
        qm = jnp.where(in_head, qt_ref[0].astype(F32), 0.0).astype(BF16)

        def block(kb, acc, carry, masked):
            s0 = pl.multiple_of(kb * tk, tk)
            zt = jnp.dot(k_ref[pl.ds(s0, tk), :], qm, preferred_element_type=F32)
            vt_h = vt_ref[0, kb, hh * head_dim:(hh + 1) * head_dim, :]
            pv, carry = _attn_block(zt, vt_h, carry, t_pos, s0, masked)
            return acc + pv, carry

        acc = jnp.zeros((head_dim, tq), F32)
        carry = jnp.zeros((8, tq), F32)
        for dgi in reversed(range(n_diag)):
            acc, carry = block(qi * n_diag + dgi, acc, carry, True)

        def body(i, st):
            kb = qi * n_diag - 1 - i
            return block(kb, st[0], st[1], False)

        acc, carry = lax.fori_loop(0, qi * n_diag, body, (acc, carry))
        outs.append(acc.T)
    o_ref[...] = jnp.concatenate(outs, axis=-1).astype(o_ref.dtype)


def _attention(qt, k2d, vt, *, bsz, seq, tq, tk):
    d = qt.shape[1]
    head_dim = d // N_HEADS
    nq = seq // tq
    ncol = d // LANES
    return pl.pallas_call(
        functools.partial(_attn_kernel, tk=tk, head_dim=head_dim),
        grid=(bsz, ncol, nq),
        in_specs=[pl.BlockSpec((1, LANES, tq), lambda b, c, q: (b, c, q)),
                  pl.BlockSpec((seq, LANES), lambda b, c, q: (b, c)),
                  pl.BlockSpec((1, seq // tk, LANES, tk), lambda b, c, q: (b, 0, c, 0))],
        out_specs=pl.BlockSpec((tq, LANES), lambda b, c, q: (b * nq + q, c)),
        out_shape=jax.ShapeDtypeStruct((bsz * seq, d), BF16),
        compiler_params=_cparams(3),
        name="attention",
    )(qt, k2d, vt)


def _oproj_kernel(o_ref, xres_ref, mod_ref, ow_ref, lng_ref, lnb_ref, rwt_ref, rb_ref,
                  x_out_ref, hlin_ref, logt_ref, *, alpha):
    y = jnp.dot(o_ref[...], ow_ref[...], preferred_element_type=F32)
    _post_norm_route(xres_ref[...], y, mod_ref, lng_ref, lnb_ref, rwt_ref, rb_ref,
                     x_out_ref, hlin_ref, logt_ref, alpha)


def _oproj(o2d, xres, mod, ow, lng, lnb, rwt, rb, *, seq, tl, alpha):
    t, d = xres.shape
    e = rwt.shape[0]
    p = d // LANES
    nlt = seq // tl
    full = lambda shape: pl.BlockSpec(shape, lambda i: (0,) * len(shape))
    return pl.pallas_call(
        functools.partial(_oproj_kernel, alpha=alpha),
        grid=(t // tl,),
        in_specs=[pl.BlockSpec((tl, d), lambda i: (i, 0)),
                  pl.BlockSpec((tl, d), lambda i: (i, 0)),
                  pl.BlockSpec((None, 6, d), lambda i: (i // nlt, 0, 0)),
                  full((d, d)), full((1, d)), full((1, d)), full((e, d)), full((e, 1))],
        out_specs=[pl.BlockSpec((tl, d), lambda i: (i, 0)),
                   pl.BlockSpec((tl * p, LANES), lambda i: (i, 0)),
                   pl.BlockSpec((e, tl), lambda i: (0, i))],
        out_shape=[jax.ShapeDtypeStruct((t, d), F32),
                   jax.ShapeDtypeStruct((t * p, LANES), F32),
                   jax.ShapeDtypeStruct((e, t), F32)],
        compiler_params=_cparams(1),
        name="oproj",
    )(o2d, xres, mod, ow, lng.reshape(1, d), lnb.reshape(1, d), rwt, rb.reshape(e, 1))


def kernel(x, c, ada_w, ada_b, ln_g, ln_b, cv_w1, cv_b1, cv_dw, cv_db, cv_ln_g, cv_ln_b,
           cv_w2, cv_b2, kv_w, q_w, o_w, router_w, router_b, moe_w_in, moe_b_in,
           moe_w_out, moe_b_out):
    bsz, seq, d = x.shape
    depth = ada_w.shape[0]
    n_a = cv_w1.shape[0]
    assert depth == 2 and n_a == 1 and q_w.shape[0] == 1, "one conv layer then one attention layer"
    alpha = (2.0 * depth) ** 0.25
    head_dim = d // N_HEADS
    tl = min(512, seq)
    tt = min(512, seq)
    rb = 256
    tq, tk = min(256, seq), 128

    mod = _ada(c, ada_w, ada_b).reshape(depth, bsz, 6, d)
    x0 = x.reshape(bsz * seq, d)
    rwt = jnp.swapaxes(router_w, 1, 2)

    x1, hlin, logt = _conv_mixer(
        x0, mod[0], cv_w1[0].astype(BF16), cv_b1[0], cv_dw[0], cv_db[0], cv_ln_g[0], cv_ln_b[0],
        cv_w2[0].astype(BF16), cv_b2[0], ln_g[0, 0], ln_b[0, 0], rwt[0], router_b[0],
        bsz=bsz, seq=seq, tl=tl, alpha=alpha)
    x2 = _moe(hlin, logt, x1, mod[0], ln_g[0, 1], ln_b[0, 1], moe_w_in[0], moe_b_in[0],
              moe_w_out[0], moe_b_out[0], seq=seq, tt=tt, rb=rb, alpha=alpha)

    scale = head_dim ** -0.5
    qwt = (q_w[0] * scale).T.astype(BF16)
    kw = kv_w[:, :d].astype(BF16)
    vwt = kv_w[:, d:].T.astype(BF16)
    qt, k2d, vt = _qkv(x2, mod[1], qwt, kw, vwt, bsz=bsz, seq=seq, tl=tl, tk=tk)
    o2d = _attention(qt, k2d, vt, bsz=bsz, seq=seq, tq=tq, tk=tk)
    x3, hlin, logt = _oproj(o2d, x2, mod[1], o_w[0].astype(BF16), ln_g[1, 0], ln_b[1, 0],
                            rwt[1], router_b[1], seq=seq, tl=tl, alpha=alpha)
    x4 = _moe(hlin, logt, x3, mod[1], ln_g[1, 1], ln_b[1, 1], moe_w_in[1], moe_b_in[1],
              moe_w_out[1], moe_b_out[1], seq=seq, tt=tt, rb=rb, alpha=alpha)
    return x4.reshape(bsz, seq, d)
```

```python
import functools
import math

import jax
import jax.numpy as jnp
from jax import lax
from jax.experimental import pallas as pl
from jax.experimental.pallas import tpu as pltpu

F32 = jnp.float32
BF16 = jnp.bfloat16
HIGHEST = lax.Precision.HIGHEST

LANES = 128
N_HEADS = 16
TOP_K = 4
CONV_HALO = 32
SWIGLU_LIMIT = 7.0
SWIGLU_ALPHA = 1.702
LN_EPS = 1e-5
LOG2E = 1.4426950408889634
CHUNK = 8
VMEM_LIMIT = 56 * 1024 * 1024


def _cparams(n_axes):
    return pltpu.CompilerParams(
        dimension_semantics=("arbitrary",) * n_axes, vmem_limit_bytes=VMEM_LIMIT)


def _layer_norm(x, g, b):
    mu = jnp.mean(x, axis=-1, keepdims=True)
    xc = x - mu
    var = jnp.mean(xc * xc, axis=-1, keepdims=True)
    return xc * lax.rsqrt(var + LN_EPS) * g + b


def _silu(x):
    return x * jax.nn.sigmoid(x)


def _ada_kernel(c_ref, w_ref, b_ref, o_ref):
    o_ref[0] = jnp.dot(_silu(c_ref[...]), w_ref[0], precision=HIGHEST,
                       preferred_element_type=F32) + b_ref[0]


def _ada(c, ada_w, ada_b):
    depth, d, n = ada_w.shape
    bsz = c.shape[0]
    tn = d
    return pl.pallas_call(
        _ada_kernel,
        grid=(depth, n // tn),
        in_specs=[pl.BlockSpec((bsz, d), lambda l, j: (0, 0)),
                  pl.BlockSpec((1, d, tn), lambda l, j: (l, 0, j)),
                  pl.BlockSpec((1, 1, tn), lambda l, j: (l, 0, j))],
        out_specs=pl.BlockSpec((1, bsz, tn), lambda l, j: (l, 0, j)),
        out_shape=jax.ShapeDtypeStruct((depth, bsz, n), F32),
        compiler_params=_cparams(2),
        name="ada",
    )(c, ada_w, ada_b.reshape(depth, 1, n))


def _post_norm_route(xres, y, mod_ref, lng_ref, lnb_ref, rwt_ref, rb_ref,
                     x_out_ref, hlin_ref, logt_ref, alpha):
    tl, d = xres.shape
    g1 = mod_ref[2:3, :]
    sh2 = mod_ref[3:4, :]
    sc2 = mod_ref[4:5, :]
    x1 = _layer_norm(alpha * xres + (1.0 + g1) * y, lng_ref[...], lnb_ref[...])
    x_out_ref[...] = x1
    h2 = x1 * (1.0 + sc2) + sh2
    p = d // LANES
    for s in range(p):
        hlin_ref[pl.ds(s, tl, stride=p), :] = h2[:, s * LANES:(s + 1) * LANES]
    logt_ref[...] = lax.dot_general(
        rwt_ref[...], h2, (((1,), (1,)), ((), ())), precision=HIGHEST,
        preferred_element_type=F32) + rb_ref[...]


def _conv_kernel(x_ref, mod_ref, w1_ref, b1_ref, dw_ref, db_ref, clg_ref, clb_ref,
                 w2_ref, b2_ref, lng_ref, lnb_ref, rwt_ref, rb_ref,
                 x_out_ref, hlin_ref, logt_ref,
                 ubuf, sbuf, dwb, cbuf, *, alpha, width):
    tl, d = x_ref.shape
    first_tile = pl.program_id(1) == 0

    @pl.when(jnp.logical_and(pl.program_id(0) == 0, first_tile))
    def _():
        for j in range(width):
            dwb[j] = jnp.broadcast_to(dw_ref[j:j + 1, :], (8, d))

    x = x_ref[...]
    sh1 = mod_ref[0:1, :]
    sc1 = mod_ref[1:2, :]
    h = (x * (1.0 + sc1) + sh1).astype(BF16)
    u = jnp.dot(h, w1_ref[...], preferred_element_type=F32) + b1_ref[...]
    glu = u[:, :d] * jax.nn.sigmoid(u[:, d:])

    @pl.when(first_tile)
    def _():
        ubuf[0:CONV_HALO, :] = jnp.zeros((CONV_HALO, d), F32)

    ubuf[CONV_HALO:CONV_HALO + tl, :] = glu
    for r in range(1, 8):
        sbuf[r - 1] = ubuf[pl.ds(r, tl + CONV_HALO - 8), :]

    off0 = CONV_HALO - (width - 1)
    cw = 512
    rows = 16

    def body(i, carry):
        base = pl.multiple_of(i * rows, rows)
        for c in range(d // cw):
            cs = slice(c * cw, (c + 1) * cw)
            accs = [jnp.broadcast_to(db_ref[:, cs], (8, cw)) for _ in range(rows // 8)]
            for j in range(width):
                q, r = divmod(off0 + j, 8)
                w = dwb[j, :, cs]
                for a in range(rows // 8):
                    start = base + 8 * (q + a)
                    if r == 0:
                        v = ubuf[pl.ds(start, 8), cs]
                    else:
                        v = sbuf[r - 1, pl.ds(start, 8), cs]
                    accs[a] = accs[a] + w * v
            for a in range(rows // 8):
                cbuf[pl.ds(base + 8 * a, 8), cs] = accs[a]
        return carry

    lax.fori_loop(0, tl // rows, body, 0)
    ubuf[0:CONV_HALO, :] = ubuf[tl:tl + CONV_HALO, :]

    v = _silu(_layer_norm(cbuf[...], clg_ref[...], clb_ref[...]))
    y = jnp.dot(v.astype(BF16), w2_ref[...], preferred_element_type=F32) + b2_ref[...]
    _post_norm_route(x, y, mod_ref, lng_ref, lnb_ref, rwt_ref, rb_ref,
                     x_out_ref, hlin_ref, logt_ref, alpha)


def _conv_mixer(x2d, mod, w1, b1, dw, db, clg, clb, w2, b2, lng, lnb, rwt, rb,
                *, bsz, seq, tl, alpha):
    t, d = x2d.shape
    e = rwt.shape[0]
    width = dw.shape[0]
    p = d // LANES
    nlt = seq // tl
    row = lambda a: a.reshape(1, -1)
    full = lambda shape: pl.BlockSpec(shape, lambda b, l: (0,) * len(shape))
    return pl.pallas_call(
        functools.partial(_conv_kernel, alpha=alpha, width=width),
        grid=(bsz, nlt),
        in_specs=[pl.BlockSpec((tl, d), lambda b, l: (b * nlt + l, 0)),
                  pl.BlockSpec((None, 6, d), lambda b, l: (b, 0, 0)),
                  full((d, 2 * d)), full((1, 2 * d)), full((width, d)), full((1, d)),
                  full((1, d)), full((1, d)), full((d, d)), full((1, d)),
                  full((1, d)), full((1, d)), full((e, d)), full((e, 1))],
        out_specs=[pl.BlockSpec((tl, d), lambda b, l: (b * nlt + l, 0)),
                   pl.BlockSpec((tl * p, LANES), lambda b, l: (b * nlt + l, 0)),
                   pl.BlockSpec((e, tl), lambda b, l: (0, b * nlt + l))],
        out_shape=[jax.ShapeDtypeStruct((t, d), F32),
                   jax.ShapeDtypeStruct((t * p, LANES), F32),
                   jax.ShapeDtypeStruct((e, t), F32)],
        scratch_shapes=[pltpu.VMEM((tl + CONV_HALO, d), F32),
                        pltpu.VMEM((7, tl + CONV_HALO - 8, d), F32),
                        pltpu.VMEM((width, 8, d), F32),
                        pltpu.VMEM((tl, d), F32)],
        compiler_params=_cparams(2),
        name="conv_mixer",
    )(x2d, mod, w1, row(b1), dw, row(db), row(clg), row(clb), w2, row(b2),
      row(lng), row(lnb), rwt, rb.reshape(e, 1))


def _route_kernel(logt_ref, slot_ref, gate_ref, cnt_ref):
    e, tt = logt_ref.shape
    l = logt_ref[...]
    iota_e = lax.broadcasted_iota(jnp.int32, (e, tt), 0).astype(F32)
    sels, vals = [], []
    for _ in range(TOP_K):
        m = jnp.max(l, axis=0, keepdims=True)
        idx = jnp.min(jnp.where(l == m, iota_e, float(e)), axis=0, keepdims=True)
        sels.append(idx)
        vals.append(m)
        l = jnp.where(iota_e == idx, -jnp.inf, l)
    exps = [jnp.exp(v - vals[0]) for v in vals]
    denom = exps[0] + exps[1] + exps[2] + exps[3]
    onehots = [iota_e == s for s in sels]
    member = jnp.zeros((e, tt), F32)
    for oh in onehots:
        member = member + jnp.where(oh, 1.0, 0.0)
    tri = jnp.where(lax.broadcasted_iota(jnp.int32, (tt, tt), 0)
                    < lax.broadcasted_iota(jnp.int32, (tt, tt), 1), 1.0, 0.0).astype(BF16)
    rank = jnp.dot(member.astype(BF16), tri, preferred_element_type=F32)
    cnt = jnp.sum(member, axis=1, keepdims=True)
    padded = jnp.floor((cnt + (CHUNK - 1)) * (1.0 / CHUNK)) * CHUNK
    low = jnp.where(lax.broadcasted_iota(jnp.int32, (e, e), 1)
                    < lax.broadcasted_iota(jnp.int32, (e, e), 0), 1.0, 0.0)
    loff = jnp.dot(low, jnp.broadcast_to(padded, (e, tt)), precision=HIGHEST,
                   preferred_element_type=F32)
    pos = loff + rank
    for k in range(TOP_K):
        slot = jnp.sum(jnp.where(onehots[k], pos, 0.0), axis=0, keepdims=True)
        slot_ref[0, k:k + 1, :] = slot.astype(jnp.int32)
        gate_ref[0, k:k + 1, :] = exps[k] / denom
    cnt_ref[0] = jnp.broadcast_to(cnt, (e, LANES)).astype(jnp.int32)


def _route(logt, tt):
    e, t = logt.shape
    nt = t // tt
    return pl.pallas_call(
        _route_kernel,
        grid=(nt,),
        in_specs=[pl.BlockSpec((e, tt), lambda i: (0, i))],
        out_specs=[pl.BlockSpec((1, TOP_K, tt), lambda i: (i, 0, 0)),
                   pl.BlockSpec((1, TOP_K, tt), lambda i: (i, 0, 0)),
                   pl.BlockSpec((1, e, LANES), lambda i: (i, 0, 0))],
        out_shape=[jax.ShapeDtypeStruct((nt, TOP_K, tt), jnp.int32),
                   jax.ShapeDtypeStruct((nt, TOP_K, tt), F32),
                   jax.ShapeDtypeStruct((nt, e, LANES), jnp.int32)],
        compiler_params=_cparams(1),
        name="route",
    )(logt)


def _plan(cnt, rb, nb):
    nt, e = cnt.shape
    padded = ((cnt + CHUNK - 1) // CHUNK) * CHUNK
    loff = jnp.cumsum(padded, axis=1) - padded
    rows_e = jnp.sum(padded, axis=0)
    blocks_e = (rows_e + rb - 1) // rb
    bend = jnp.cumsum(blocks_e)
    pstart = (bend - blocks_e) * rb
    goff = pstart[None, :] + jnp.cumsum(padded, axis=0) - padded
    nchunks = jnp.sum(padded, axis=1) // CHUNK
    block_ids = jnp.arange(nb, dtype=bend.dtype)
    block_e = jnp.minimum(jnp.sum(bend[None, :] <= block_ids[:, None], axis=1), e - 1)
    tail_start = pstart + rows_e
    tail_chunks = (blocks_e * rb - rows_e) // CHUNK
    i32 = lambda a: a.astype(jnp.int32)
    return (i32(padded.reshape(-1)), i32(loff.reshape(-1)), i32(goff.reshape(-1)),
            i32(nchunks), i32(tail_start), i32(tail_chunks), i32(block_e), i32(bend[-1:]))


def _chunk_copies(padded_ref, loff_ref, goff_ref, tile, n_experts, p, make):
    def per_expert(ex, carry):
        idx = tile * n_experts + ex
        n = padded_ref[idx] // CHUNK
        lo = loff_ref[idx]
        go = goff_ref[idx]

        def per_chunk(j, c):
            lrow = pl.multiple_of((lo + j * CHUNK) * p, CHUNK * p)
            grow = pl.multiple_of((go + j * CHUNK) * p, CHUNK * p)
            make(lrow, grow).start()
            return c

        return lax.fori_loop(0, n, per_chunk, carry)

    lax.fori_loop(0, n_experts, per_expert, 0)


def _dispatch_kernel(padded_ref, loff_ref, goff_ref, nch_ref, tstart_ref, tchunks_ref,
                     x_ref, slot_ref, xs_ref, stage, sem, *, n_experts):
    tile = pl.program_id(0)
    p = x_ref.shape[0] // slot_ref.shape[2]
    tt = slot_ref.shape[2]
    rows = CHUNK * p
    make = lambda lrow, grow: pltpu.make_async_copy(
        stage.at[pl.ds(lrow, rows), :], xs_ref.at[pl.ds(grow, rows), :], sem)
    stage[...] = jnp.zeros(stage.shape, F32)

    @pl.when(tile == 0)
    def _():
        def per_expert(ex, total):
            n = tchunks_ref[ex]

            def per_chunk(j, c):
                grow = pl.multiple_of((tstart_ref[ex] + j * CHUNK) * p, rows)
                make(0, grow).start()
                return c

            lax.fori_loop(0, n, per_chunk, 0)
            return total + n

        total = lax.fori_loop(0, n_experts, per_expert, 0)

        def wait_tail(j, c):
            make(0, 0).wait()
            return c

        lax.fori_loop(0, total, wait_tail, 0)

    def per_token(t, carry):
        v = x_ref[pl.ds(pl.multiple_of(t * p, p), p), :]
        for k in range(TOP_K):
            s = slot_ref[0, k, t]
            stage[pl.ds(pl.multiple_of(s * p, p), p), :] = v
        return carry

    lax.fori_loop(0, tt, per_token, 0)

    _chunk_copies(padded_ref, loff_ref, goff_ref, tile, n_experts, p, make)

    def wait_one(j, c):
        make(0, 0).wait()
        return c

    lax.fori_loop(0, nch_ref[tile], wait_one, 0)


def _dispatch(hlin, slots, tables, tails, *, n_rows, n_experts, d):
    padded, loff, goff, nch = tables
    tstart, tchunks = tails
    nt, _, tt = slots.shape
    p = d // LANES
    stage_rows = TOP_K * tt + n_experts * (CHUNK - 1)
    stage_rows = -(-stage_rows // CHUNK) * CHUNK
    return pl.pallas_call(
        functools.partial(_dispatch_kernel, n_experts=n_experts),
        grid_spec=pltpu.PrefetchScalarGridSpec(
            num_scalar_prefetch=6,
            grid=(nt,),
            in_specs=[pl.BlockSpec((tt * p, LANES), lambda i, *_: (i, 0)),
                      pl.BlockSpec((1, TOP_K, tt), lambda i, *_: (i, 0, 0),
                                   memory_space=pltpu.SMEM)],
            out_specs=pl.BlockSpec(memory_space=pl.ANY),
            scratch_shapes=[pltpu.VMEM((stage_rows * p, LANES), F32),
                            pltpu.SemaphoreType.DMA(())]),
        out_shape=jax.ShapeDtypeStruct((n_rows * p, LANES), F32),
        compiler_params=_cparams(1),
        name="dispatch",
    )(padded, loff, goff, nch, tstart, tchunks, hlin, slots)


def _expert_kernel(be_ref, nu_ref, x_ref, win_ref, bin_ref, wout_ref, bout_ref, y_ref,
                   win_bf, wout_bf):
    b = pl.program_id(0)
    rb = x_ref.shape[0] * LANES // win_ref.shape[1]
    d = win_ref.shape[1]
    f = wout_ref.shape[1]
    p = d // LANES

    @pl.when(b < nu_ref[0])
    def _():
        prev = be_ref[jnp.maximum(b - 1, 0)]

        @pl.when(jnp.logical_or(b == 0, be_ref[b] != prev))
        def _():
            win_bf[...] = win_ref[0].astype(BF16)
            wout_bf[...] = wout_ref[0].astype(BF16)

        x = jnp.concatenate([x_ref[pl.ds(s, rb, stride=p), :] for s in range(p)], axis=-1)
        u = jnp.dot(x.astype(BF16), win_bf[...], preferred_element_type=F32) + bin_ref[0]
        x_glu = jnp.minimum(u[:, :f], SWIGLU_LIMIT)
        x_lin = jnp.clip(u[:, f:], -SWIGLU_LIMIT, SWIGLU_LIMIT)
        act = (x_lin + 1.0) * (x_glu * jax.nn.sigmoid(SWIGLU_ALPHA * x_glu))
        y = jnp.dot(act.astype(BF16), wout_bf[...], preferred_element_type=F32) + bout_ref[0]
        for s in range(p):
            y_ref[pl.ds(s, rb, stride=p), :] = y[:, s * LANES:(s + 1) * LANES]


def _experts(xs, block_e, n_used, w_in, b_in, w_out, b_out, *, layer, rb, nb):
    _, e, d, f2 = w_in.shape
    f = f2 // 2
    p = d // LANES
    blk = lambda b, be, nu: (jnp.minimum(b, nu[0] - 1), 0)
    wsel = lambda b, be, nu: (layer, be[jnp.minimum(b, nu[0] - 1)], 0, 0)
    return pl.pallas_call(
        _expert_kernel,
        grid_spec=pltpu.PrefetchScalarGridSpec(
            num_scalar_prefetch=2,
            grid=(nb,),
            in_specs=[pl.BlockSpec((rb * p, LANES), blk),
                      pl.BlockSpec((None, 1, d, f2), wsel),
                      pl.BlockSpec((None, 1, 1, f2), wsel),
                      pl.BlockSpec((None, 1, f, d), wsel),
                      pl.BlockSpec((None, 1, 1, d), wsel)],
            out_specs=pl.BlockSpec((rb * p, LANES), blk),
            scratch_shapes=[pltpu.VMEM((d, f2), BF16), pltpu.VMEM((f, d), BF16)]),
        out_shape=jax.ShapeDtypeStruct(xs.shape, F32),
        compiler_params=_cparams(1),
        name="experts",
    )(block_e, n_used, xs, w_in, b_in.reshape(-1, e, 1, f2), w_out, b_out.reshape(-1, e, 1, d))


def _combine_kernel(padded_ref, loff_ref, goff_ref, nch_ref, ys_ref, slot_ref, gate_ref,
                    xres_ref, mod_ref, lng_ref, lnb_ref, x_out_ref, stage, comb, sem,
                    *, n_experts, alpha):
    tile = pl.program_id(0)
    tt, d = xres_ref.shape
    p = d // LANES
    rows = CHUNK * p
    make = lambda lrow, grow: pltpu.make_async_copy(
        ys_ref.at[pl.ds(grow, rows), :], stage.at[pl.ds(lrow, rows), :], sem)
    _chunk_copies(padded_ref, loff_ref, goff_ref, tile, n_experts, p, make)

    def wait_one(j, c):
        make(0, 0).wait()
        return c

    lax.fori_loop(0, nch_ref[tile], wait_one, 0)

    def per_token(t, carry):
        acc = None
        for k in range(TOP_K):
            s = slot_ref[0, k, t]
            term = gate_ref[0, k, t] * stage[pl.ds(pl.multiple_of(s * p, p), p), :]
            acc = term if acc is None else acc + term
        comb[pl.ds(pl.multiple_of(t * p, p), p), :] = acc
        return carry

    lax.fori_loop(0, tt, per_token, 0)
    moe = jnp.concatenate([comb[pl.ds(s, tt, stride=p), :] for s in range(p)], axis=-1)
    g2 = mod_ref[5:6, :]
    x_out_ref[...] = _layer_norm(alpha * xres_ref[...] + (1.0 + g2) * moe,
                                 lng_ref[...], lnb_ref[...])


def _combine(ys, slots, gates, tables, xres, mod, lng, lnb, *, n_experts, seq, alpha):
    padded, loff, goff, nch = tables
    nt, _, tt = slots.shape
    t, d = xres.shape
    p = d // LANES
    tiles_per_seq = seq // tt
    stage_rows = TOP_K * tt + n_experts * (CHUNK - 1)
    stage_rows = -(-stage_rows // CHUNK) * CHUNK
    smem = lambda: pl.BlockSpec((1, TOP_K, tt), lambda i, *_: (i, 0, 0),
                                memory_space=pltpu.SMEM)
    return pl.pallas_call(
        functools.partial(_combine_kernel, n_experts=n_experts, alpha=alpha),
        grid_spec=pltpu.PrefetchScalarGridSpec(
            num_scalar_prefetch=4,
            grid=(nt,),
            in_specs=[pl.BlockSpec(memory_space=pl.ANY), smem(), smem(),
                      pl.BlockSpec((tt, d), lambda i, *_: (i, 0)),
                      pl.BlockSpec((None, 6, d), lambda i, *_: (i // tiles_per_seq, 0, 0)),
                      pl.BlockSpec((1, d), lambda i, *_: (0, 0)),
                      pl.BlockSpec((1, d), lambda i, *_: (0, 0))],
            out_specs=pl.BlockSpec((tt, d), lambda i, *_: (i, 0)),
            scratch_shapes=[pltpu.VMEM((stage_rows * p, LANES), F32),
                            pltpu.VMEM((tt * p, LANES), F32),
                            pltpu.SemaphoreType.DMA(())]),
        out_shape=jax.ShapeDtypeStruct((t, d), F32),
        compiler_params=_cparams(1),
        name="combine",
    )(padded, loff, goff, nch, ys, slots, gates, xres, mod, lng.reshape(1, d), lnb.reshape(1, d))


def _moe(hlin, logt, xres, mod, lng, lnb, w_in, b_in, w_out, b_out, *, layer, seq, tt, rb,
         alpha):
    e, t = logt.shape
    d = xres.shape[1]
    nt = t // tt
    slots, gates, cnt = _route(logt, tt)
    max_rows = TOP_K * t + nt * e * (CHUNK - 1)
    nb = -(-max_rows // rb) + e
    padded, loff, goff, nch, tstart, tchunks, block_e, n_used = _plan(cnt[:, :, 0], rb, nb)
    tables = (padded, loff, goff, nch)
    xs = _dispatch(hlin, slots, tables, (tstart, tchunks), n_rows=nb * rb, n_experts=e, d=d)
    ys = _experts(xs, block_e, n_used, w_in, b_in, w_out, b_out, layer=layer, rb=rb, nb=nb)
    return _combine(ys, slots, gates, tables, xres, mod, lng, lnb,
                    n_experts=e, seq=seq, alpha=alpha)


def _qkv_kernel(x_ref, mod_ref, qwt_ref, kw_ref, vwt_ref, qt_ref, k_ref, vt_ref):
    x = x_ref[...]
    sh1 = mod_ref[0:1, :]
    sc1 = mod_ref[1:2, :]
    h = (x * (1.0 + sc1) + sh1).astype(BF16)
    nt_dims = (((1,), (1,)), ((), ()))
    qt_ref[0] = lax.dot_general(qwt_ref[...], h, nt_dims,
                                preferred_element_type=F32).astype(BF16)
    tk = vt_ref.shape[3]
    ngrp = tk // 8
    r = lax.broadcasted_iota(jnp.int32, (tk, tk), 0)
    s = lax.broadcasted_iota(jnp.int32, (tk, tk), 1)
    perm = jnp.where(s == (r & 7) * ngrp + (r >> 3), 1.0, 0.0).astype(BF16)
    xb = x.astype(BF16)
    xp = jnp.concatenate(
        [jnp.dot(perm, xb[j * tk:(j + 1) * tk, :], preferred_element_type=F32)
         for j in range(vt_ref.shape[1])], axis=0).astype(BF16)
    k_ref[...] = jnp.dot(xp, kw_ref[...], preferred_element_type=F32).astype(BF16)
    vt = lax.dot_general(vwt_ref[...], xp, nt_dims, preferred_element_type=F32).astype(BF16)
    for j in range(vt_ref.shape[1]):
        vt_ref[0, j] = vt[:, j * tk:(j + 1) * tk]


def _qkv(x2d, mod, qwt, kw, vwt, *, bsz, seq, tl, tk):
    t, d = x2d.shape
    nlt = seq // tl
    full = pl.BlockSpec((d, d), lambda b, l: (0, 0))
    return pl.pallas_call(
        _qkv_kernel,
        grid=(bsz, nlt),
        in_specs=[pl.BlockSpec((tl, d), lambda b, l: (b * nlt + l, 0)),
                  pl.BlockSpec((None, 6, d), lambda b, l: (b, 0, 0)),
                  full, full, full],
        out_specs=[pl.BlockSpec((1, d, tl), lambda b, l: (b, 0, l)),
                   pl.BlockSpec((tl, d), lambda b, l: (b * nlt + l, 0)),
                   pl.BlockSpec((1, tl // tk, d, tk), lambda b, l: (b, l, 0, 0))],
        out_shape=[jax.ShapeDtypeStruct((bsz, d, seq), BF16),
                   jax.ShapeDtypeStruct((t, d), BF16),
                   jax.ShapeDtypeStruct((bsz, seq // tk, d, tk), BF16)],
        compiler_params=_cparams(2),
        name="qkv",
    )(x2d, mod, qwt, kw, vwt)


def _stick_block(z_ref, a_ref, carry, masked):
    tk, tq = z_ref.shape
    ngrp = tk // 8
    sub = lax.broadcasted_iota(jnp.int32, (8, tq), 0)
    lane = lax.broadcasted_iota(jnp.int32, (8, tq), 1)
    causal = lambda g: sub * ngrp + g < lane
    run = None
    for g in reversed(range(ngrp)):
        z = z_ref[g * 8:(g + 1) * 8, :]
        sp = jnp.maximum(z, 0.0) + LOG2E * jnp.log(1.0 + jnp.exp2(-jnp.abs(z)))
        if masked:
            sp = jnp.where(causal(g), sp, 0.0)
        run = sp if run is None else run + sp
        z_ref[g * 8:(g + 1) * 8, :] = z - run
    tot = run
    s = tot
    for k in (1, 2, 4):
        s = s + jnp.where(sub < 8 - k, pltpu.roll(s, 8 - k, axis=0), 0.0)
    later = s - tot + carry
    for g in range(0, ngrp, 2):
        pair = []
        for gg in (g, g + 1):
            a = jnp.exp2(z_ref[gg * 8:(gg + 1) * 8, :] - later)
            if masked:
                a = jnp.where(causal(gg), a, 0.0)
            pair.append(a)
        a_ref[g * 8:(g + 2) * 8, :] = jnp.concatenate(pair, axis=0).astype(BF16)
    return carry + jnp.broadcast_to(s[0:1, :], (8, tq))


def _attn_kernel(qt_ref, k_ref, vt_ref, o_ref, z0, z1, a0, a1, acc_buf, *, head_dim):
    qi = pl.program_id(2)
    tq = qt_ref.shape[2]
    tk = vt_ref.shape[3]
    hp = qt_ref.shape[1] // head_dim
    row = lax.broadcasted_iota(jnp.int32, (hp * head_dim, tq), 0)
    qf = qt_ref[0].astype(F32)
    qms = [jnp.where(jnp.logical_and(row >= h * head_dim, row < (h + 1) * head_dim), qf, 0.0)
           .astype(BF16) for h in range(hp)]

    def scores(kb, z_out):
        kblk = k_ref[pl.ds(pl.multiple_of(kb * tk, tk), tk), :]
        for h in range(hp):
            z_out[h] = jnp.dot(kblk, qms[h], preferred_element_type=F32)

    def weighted_values(kb, a_in):
        return [jnp.dot(vt_ref[0, kb, h * head_dim:(h + 1) * head_dim, :], a_in[h],
                        preferred_element_type=F32) for h in range(hp)]

    def step(kb, z_in, z_out, a_in, a_out, carries):
        pv = weighted_values(kb + 1, a_in)
        scores(jnp.maximum(kb - 1, 0), z_out)
        carries = tuple(_stick_block(z_in.at[h], a_out.at[h], carries[h], False)
                        for h in range(hp))
        for h in range(hp):
            acc_buf[h] += pv[h]
        return carries

    def finish(a_last):
        pv = weighted_values(0, a_last)
        o_ref[...] = jnp.concatenate([(acc_buf[h] + pv[h]).T for h in range(hp)],
                                     axis=-1).astype(o_ref.dtype)

    scores(qi, z0)
    scores(jnp.maximum(qi - 1, 0), z1)
    zero = jnp.zeros((8, tq), F32)
    carries = tuple(_stick_block(z0.at[h], a0.at[h], zero, True) for h in range(hp))
    acc_buf[...] = jnp.zeros(acc_buf.shape, F32)

    def pair(i, carries):
        kb = qi - 1 - 2 * i
        carries = step(kb, z1, z0, a0, a1, carries)
        return step(kb - 1, z0, z1, a1, a0, carries)

    carries = lax.fori_loop(0, qi // 2, pair, carries)

    @pl.when(qi % 2 == 1)
    def _():
        step(0, z1, z0, a0, a1, carries)
        finish(a1)

    @pl.when(qi % 2 == 0)
    def _():
        finish(a0)


def _attention(qt, k2d, vt, *, bsz, seq, tq):
    d = qt.shape[1]
    head_dim = d // N_HEADS
    tk = vt.shape[3]
    assert tk == tq, "the diagonal mask assumes square blocks"
    nq = seq // tq
    ncol = d // LANES
    hp = LANES // head_dim
    return pl.pallas_call(
        functools.partial(_attn_kernel, head_dim=head_dim),
        grid=(bsz, ncol, nq),
        in_specs=[pl.BlockSpec((1, LANES, tq), lambda b, c, q: (b, c, q)),
                  pl.BlockSpec((seq, LANES), lambda b, c, q: (b, c)),
                  pl.BlockSpec((1, seq // tk, LANES, tk), lambda b, c, q: (b, 0, c, 0))],
        out_specs=pl.BlockSpec((tq, LANES), lambda b, c, q: (b * nq + q, c)),
        out_shape=jax.ShapeDtypeStruct((bsz * seq, d), BF16),
        scratch_shapes=[pltpu.VMEM((hp, tk, tq), F32), pltpu.VMEM((hp, tk, tq), F32),
                        pltpu.VMEM((hp, tk, tq), BF16), pltpu.VMEM((hp, tk, tq), BF16),
                        pltpu.VMEM((hp, head_dim, tq), F32)],
        compiler_params=_cparams(3),
        name="attention",
    )(qt, k2d, vt)


def _oproj_kernel(o_ref, xres_ref, mod_ref, ow_ref, lng_ref, lnb_ref, rwt_ref, rb_ref,
                  x_out_ref, hlin_ref, logt_ref, *, alpha):
    y = jnp.dot(o_ref[...], ow_ref[...], preferred_element_type=F32)
    _post_norm_route(xres_ref[...], y, mod_ref, lng_ref, lnb_ref, rwt_ref, rb_ref,
                     x_out_ref, hlin_ref, logt_ref, alpha)


def _oproj(o2d, xres, mod, ow, lng, lnb, rwt, rb, *, seq, tl, alpha):
    t, d = xres.shape
    e = rwt.shape[0]
    p = d // LANES
    nlt = seq // tl
    full = lambda shape: pl.BlockSpec(shape, lambda i: (0,) * len(shape))
    return pl.pallas_call(
        functools.partial(_oproj_kernel, alpha=alpha),
        grid=(t // tl,),
        in_specs=[pl.BlockSpec((tl, d), lambda i: (i, 0)),
                  pl.BlockSpec((tl, d), lambda i: (i, 0)),
                  pl.BlockSpec((None, 6, d), lambda i: (i // nlt, 0, 0)),
                  full((d, d)), full((1, d)), full((1, d)), full((e, d)), full((e, 1))],
        out_specs=[pl.BlockSpec((tl, d), lambda i: (i, 0)),
                   pl.BlockSpec((tl * p, LANES), lambda i: (i, 0)),
                   pl.BlockSpec((e, tl), lambda i: (0, i))],
        out_shape=[jax.ShapeDtypeStruct((t, d), F32),
                   jax.ShapeDtypeStruct((t * p, LANES), F32),
                   jax.ShapeDtypeStruct((e, t), F32)],
        compiler_params=_cparams(1),
        name="oproj",
    )(o2d, xres, mod, ow, lng.reshape(1, d), lnb.reshape(1, d), rwt, rb.reshape(e, 1))


def kernel(x, c, ada_w, ada_b, ln_g, ln_b, cv_w1, cv_b1, cv_dw, cv_db, cv_ln_g, cv_ln_b,
           cv_w2, cv_b2, kv_w, q_w, o_w, router_w, router_b, moe_w_in, moe_b_in,
           moe_w_out, moe_b_out):
    bsz, seq, d = x.shape
    depth = ada_w.shape[0]
    n_a = cv_w1.shape[0]
    assert depth == 2 and n_a == 1 and q_w.shape[0] == 1, "one conv layer then one attention layer"
    alpha = (2.0 * depth) ** 0.25
    head_dim = d // N_HEADS
    tl = min(512, seq)
    tt = min(512, seq)
    rb = 256
    tq = min(256, seq)

    mod = _ada(c, ada_w, ada_b).reshape(depth, bsz, 6, d)
    x0 = x.reshape(bsz * seq, d)
    rwt = jnp.swapaxes(router_w, 1, 2)

    x1, hlin, logt = _conv_mixer(
        x0, mod[0], cv_w1[0].astype(BF16), cv_b1[0], cv_dw[0], cv_db[0], cv_ln_g[0], cv_ln_b[0],
        cv_w2[0].astype(BF16), cv_b2[0], ln_g[0, 0], ln_b[0, 0], rwt[0], router_b[0],
        bsz=bsz, seq=seq, tl=tl, alpha=alpha)
    x2 = _moe(hlin, logt, x1, mod[0], ln_g[0, 1], ln_b[0, 1], moe_w_in, moe_b_in,
              moe_w_out, moe_b_out, layer=0, seq=seq, tt=tt, rb=rb, alpha=alpha)

    scale = head_dim ** -0.5
    qwt = (q_w[0] * (scale * LOG2E)).T.astype(BF16)
    kw = kv_w[:, :d].astype(BF16)
    vwt = kv_w[:, d:].T.astype(BF16)
    qt, k2d, vt = _qkv(x2, mod[1], qwt, kw, vwt, bsz=bsz, seq=seq, tl=tl, tk=tq)
    o2d = _attention(qt, k2d, vt, bsz=bsz, seq=seq, tq=tq)
    x3, hlin, logt = _oproj(o2d, x2, mod[1], o_w[0].astype(BF16), ln_g[1, 0], ln_b[1, 0],
                            rwt[1], router_b[1], seq=seq, tl=tl, alpha=alpha)
    x4 = _moe(hlin, logt, x3, mod[1], ln_g[1, 1], ln_b[1, 1], moe_w_in, moe_b_in,
              moe_w_out, moe_b_out, layer=1, seq=seq, tt=tt, rb=rb, alpha=alpha)
    return x4.reshape(bsz, seq, d)
```

```python
import functools
import math

import jax
import jax.numpy as jnp
from jax import lax
from jax.experimental import pallas as pl
from jax.experimental.pallas import tpu as pltpu

F32 = jnp.float32
BF16 = jnp.bfloat16
HIGHEST = lax.Precision.HIGHEST

LANES = 128
N_HEADS = 16
TOP_K = 4
CONV_HALO = 32
SWIGLU_LIMIT = 7.0
SWIGLU_ALPHA = 1.702
LN_EPS = 1e-5
LOG2E = 1.4426950408889634
CHUNK = 8
VMEM_LIMIT = 56 * 1024 * 1024


def _cparams(n_axes):
    return pltpu.CompilerParams(
        dimension_semantics=("arbitrary",) * n_axes, vmem_limit_bytes=VMEM_LIMIT)


def _layer_norm(x, g, b):
    mu = jnp.mean(x, axis=-1, keepdims=True)
    xc = x - mu
    var = jnp.mean(xc * xc, axis=-1, keepdims=True)
    return xc * lax.rsqrt(var + LN_EPS) * g + b


def _silu(x):
    return x * jax.nn.sigmoid(x)


def _ada_kernel(c_ref, w_ref, b_ref, o_ref):
    o_ref[0] = jnp.dot(_silu(c_ref[...]), w_ref[0], precision=HIGHEST,
                       preferred_element_type=F32) + b_ref[0]


def _ada(c, ada_w, ada_b):
    depth, d, n = ada_w.shape
    bsz = c.shape[0]
    tn = d
    return pl.pallas_call(
        _ada_kernel,
        grid=(depth, n // tn),
        in_specs=[pl.BlockSpec((bsz, d), lambda l, j: (0, 0)),
                  pl.BlockSpec((1, d, tn), lambda l, j: (l, 0, j)),
                  pl.BlockSpec((1, 1, tn), lambda l, j: (l, 0, j))],
        out_specs=pl.BlockSpec((1, bsz, tn), lambda l, j: (l, 0, j)),
        out_shape=jax.ShapeDtypeStruct((depth, bsz, n), F32),
        compiler_params=_cparams(2),
        name="ada",
    )(c, ada_w, ada_b.reshape(depth, 1, n))


def _post_norm_route(xres, y, mod_ref, lng_ref, lnb_ref, rwt_ref, rb_ref,
                     x_out_ref, hlin_ref, logt_ref, alpha):
    tl, d = xres.shape
    g1 = mod_ref[2:3, :]
    sh2 = mod_ref[3:4, :]
    sc2 = mod_ref[4:5, :]
    x1 = _layer_norm(alpha * xres + (1.0 + g1) * y, lng_ref[...], lnb_ref[...])
    x_out_ref[...] = x1
    h2 = x1 * (1.0 + sc2) + sh2
    p = d // LANES
    for s in range(p):
        hlin_ref[pl.ds(s, tl, stride=p), :] = h2[:, s * LANES:(s + 1) * LANES]
    logt_ref[...] = lax.dot_general(
        rwt_ref[...], h2, (((1,), (1,)), ((), ())), precision=HIGHEST,
        preferred_element_type=F32) + rb_ref[...]


def _conv_kernel(x_ref, mod_ref, w1_ref, b1_ref, dw_ref, db_ref, clg_ref, clb_ref,
                 w2_ref, b2_ref, lng_ref, lnb_ref, rwt_ref, rb_ref,
                 x_out_ref, hlin_ref, logt_ref,
                 ubuf, sbuf, dwb, cbuf, *, alpha, width):
    tl, d = x_ref.shape
    first_tile = pl.program_id(1) == 0

    @pl.when(jnp.logical_and(pl.program_id(0) == 0, first_tile))
    def _():
        for j in range(width):
            dwb[j] = jnp.broadcast_to(dw_ref[j:j + 1, :], (8, d))

    x = x_ref[...]
    sh1 = mod_ref[0:1, :]
    sc1 = mod_ref[1:2, :]
    h = (x * (1.0 + sc1) + sh1).astype(BF16)
    u = jnp.dot(h, w1_ref[...], preferred_element_type=F32) + b1_ref[...]
    glu = u[:, :d] * jax.nn.sigmoid(u[:, d:])

    @pl.when(first_tile)
    def _():
        ubuf[0:CONV_HALO, :] = jnp.zeros((CONV_HALO, d), F32)

    ubuf[CONV_HALO:CONV_HALO + tl, :] = glu
    for r in range(1, 8):
        sbuf[r - 1] = ubuf[pl.ds(r, tl + CONV_HALO - 8), :]

    off0 = CONV_HALO - (width - 1)
    cw = 512
    rows = 16

    def body(i, carry):
        base = pl.multiple_of(i * rows, rows)
        for c in range(d // cw):
            cs = slice(c * cw, (c + 1) * cw)
            accs = [jnp.broadcast_to(db_ref[:, cs], (8, cw)) for _ in range(rows // 8)]
            for j in range(width):
                q, r = divmod(off0 + j, 8)
                w = dwb[j, :, cs]
                for a in range(rows // 8):
                    start = base + 8 * (q + a)
                    if r == 0:
                        v = ubuf[pl.ds(start, 8), cs]
                    else:
                        v = sbuf[r - 1, pl.ds(start, 8), cs]
                    accs[a] = accs[a] + w * v
            for a in range(rows // 8):
                cbuf[pl.ds(base + 8 * a, 8), cs] = accs[a]
        return carry

    lax.fori_loop(0, tl // rows, body, 0)
    ubuf[0:CONV_HALO, :] = ubuf[tl:tl + CONV_HALO, :]

    v = _silu(_layer_norm(cbuf[...], clg_ref[...], clb_ref[...]))
    y = jnp.dot(v.astype(BF16), w2_ref[...], preferred_element_type=F32) + b2_ref[...]
    _post_norm_route(x, y, mod_ref, lng_ref, lnb_ref, rwt_ref, rb_ref,
                     x_out_ref, hlin_ref, logt_ref, alpha)


def _conv_mixer(x2d, mod, w1, b1, dw, db, clg, clb, w2, b2, lng, lnb, rwt, rb,
                *, bsz, seq, tl, alpha):
    t, d = x2d.shape
    e = rwt.shape[0]
    width = dw.shape[0]
    p = d // LANES
    nlt = seq // tl
    row = lambda a: a.reshape(1, -1)
    full = lambda shape: pl.BlockSpec(shape, lambda b, l: (0,) * len(shape))
    return pl.pallas_call(
        functools.partial(_conv_kernel, alpha=alpha, width=width),
        grid=(bsz, nlt),
        in_specs=[pl.BlockSpec((tl, d), lambda b, l: (b * nlt + l, 0)),
                  pl.BlockSpec((None, 6, d), lambda b, l: (b, 0, 0)),
                  full((d, 2 * d)), full((1, 2 * d)), full((width, d)), full((1, d)),
                  full((1, d)), full((1, d)), full((d, d)), full((1, d)),
                  full((1, d)), full((1, d)), full((e, d)), full((e, 1))],
        out_specs=[pl.BlockSpec((tl, d), lambda b, l: (b * nlt + l, 0)),
                   pl.BlockSpec((tl * p, LANES), lambda b, l: (b * nlt + l, 0)),
                   pl.BlockSpec((e, tl), lambda b, l: (0, b * nlt + l))],
        out_shape=[jax.ShapeDtypeStruct((t, d), F32),
                   jax.ShapeDtypeStruct((t * p, LANES), F32),
                   jax.ShapeDtypeStruct((e, t), F32)],
        scratch_shapes=[pltpu.VMEM((tl + CONV_HALO, d), F32),
                        pltpu.VMEM((7, tl + CONV_HALO - 8, d), F32),
                        pltpu.VMEM((width, 8, d), F32),
                        pltpu.VMEM((tl, d), F32)],
        compiler_params=_cparams(2),
        name="conv_mixer",
    )(x2d, mod, w1, row(b1), dw, row(db), row(clg), row(clb), w2, row(b2),
      row(lng), row(lnb), rwt, rb.reshape(e, 1))


def _route_kernel(logt_ref, slot_ref, gate_ref, cnt_ref, *, p, stage_rows):
    e, tt = logt_ref.shape
    l = logt_ref[...]
    iota_e = lax.broadcasted_iota(jnp.int32, (e, tt), 0).astype(F32)
    sels, vals = [], []
    for _ in range(TOP_K):
        m = jnp.max(l, axis=0, keepdims=True)
        idx = jnp.min(jnp.where(l == m, iota_e, float(e)), axis=0, keepdims=True)
        sels.append(idx)
        vals.append(m)
        l = jnp.where(iota_e == idx, -jnp.inf, l)
    exps = [jnp.exp(v - vals[0]) for v in vals]
    denom = exps[0] + exps[1] + exps[2] + exps[3]
    onehots = [iota_e == s for s in sels]
    member = jnp.zeros((e, tt), F32)
    for oh in onehots:
        member = member + jnp.where(oh, 1.0, 0.0)
    tri = jnp.where(lax.broadcasted_iota(jnp.int32, (tt, tt), 0)
                    < lax.broadcasted_iota(jnp.int32, (tt, tt), 1), 1.0, 0.0).astype(BF16)
    rank = jnp.dot(member.astype(BF16), tri, preferred_element_type=F32)
    cnt = jnp.sum(member, axis=1, keepdims=True)
    padded = jnp.floor((cnt + (CHUNK - 1)) * (1.0 / CHUNK)) * CHUNK
    low = jnp.where(lax.broadcasted_iota(jnp.int32, (e, e), 1)
                    < lax.broadcasted_iota(jnp.int32, (e, e), 0), 1.0, 0.0)
    loff = jnp.dot(low, jnp.broadcast_to(padded, (e, tt)), precision=HIGHEST,
                   preferred_element_type=F32)
    pos = loff + rank + (pl.program_id(0) % 2 * stage_rows).astype(F32)
    for k in range(TOP_K):
        slot = jnp.sum(jnp.where(onehots[k], pos, 0.0), axis=0, keepdims=True)
        slot_ref[0, k:k + 1, :] = (slot * p).astype(jnp.int32)
        gate_ref[0, k:k + 1, :] = exps[k] / denom
    cnt_ref[0] = jnp.broadcast_to(cnt, (e, LANES)).astype(jnp.int32)


def _route(logt, tt, p, stage_rows):
    e, t = logt.shape
    nt = t // tt
    return pl.pallas_call(
        functools.partial(_route_kernel, p=p, stage_rows=stage_rows),
        grid=(nt,),
        in_specs=[pl.BlockSpec((e, tt), lambda i: (0, i))],
        out_specs=[pl.BlockSpec((1, TOP_K, tt), lambda i: (i, 0, 0)),
                   pl.BlockSpec((1, TOP_K, tt), lambda i: (i, 0, 0)),
                   pl.BlockSpec((1, e, LANES), lambda i: (i, 0, 0))],
        out_shape=[jax.ShapeDtypeStruct((nt, TOP_K, tt), jnp.int32),
                   jax.ShapeDtypeStruct((nt, TOP_K, tt), F32),
                   jax.ShapeDtypeStruct((nt, e, LANES), jnp.int32)],
        compiler_params=_cparams(1),
        name="route",
    )(logt)


def _plan(cnt, rb, nb, n_classes):
    nt, e = cnt.shape
    padded = ((cnt + CHUNK - 1) // CHUNK) * CHUNK
    loff = jnp.cumsum(padded, axis=1) - padded
    rows_e = jnp.sum(padded, axis=0)
    blocks_e = (rows_e + rb - 1) // rb
    bend = jnp.cumsum(blocks_e)
    pstart = (bend - blocks_e) * rb
    goff = pstart[None, :] + jnp.cumsum(padded, axis=0) - padded
    block_ids = jnp.arange(nb, dtype=bend.dtype)
    block_e = jnp.minimum(jnp.sum(bend[None, :] <= block_ids[:, None], axis=1), e - 1)
    chunks = padded // CHUNK
    classcnt = jnp.stack([jnp.sum((chunks >> k) & 1, axis=1) for k in range(n_classes)], axis=1)
    tail_start = pstart + rows_e
    tail_chunks = (blocks_e * rb - rows_e) // CHUNK
    i32 = lambda a: a.astype(jnp.int32)
    return (i32(padded.reshape(-1)), i32(loff.reshape(-1)), i32(goff.reshape(-1)),
            i32(classcnt.reshape(-1)), i32(tail_start), i32(tail_chunks), i32(block_e), i32(bend[-1:]))


def _size_classes(tt):
    return (tt // CHUNK).bit_length()


def _start_group_dmas(padded_ref, loff_ref, goff_ref, tile, n_experts, p, n_classes, make):
    def per_expert(ex, carry):
        idx = tile * n_experts + ex
        n = padded_ref[idx] // CHUNK
        lo = loff_ref[idx]
        go = goff_ref[idx]
        for k in range(n_classes):
            @pl.when((n >> k) & 1 == 1)
            def _():
                done = (n >> (k + 1)) << (k + 1)
                lrow = pl.multiple_of((lo + done * CHUNK) * p, CHUNK * p)
                grow = pl.multiple_of((go + done * CHUNK) * p, CHUNK * p)
                make(lrow, grow, (CHUNK << k) * p).start()
        return carry

    lax.fori_loop(0, n_experts, per_expert, 0)


def _wait_group_dmas(classcnt_ref, tile, p, n_classes, make):
    for k in range(n_classes):
        def wait_one(j, c, k=k):
            make(0, 0, (CHUNK << k) * p).wait()
            return c

        lax.fori_loop(0, classcnt_ref[tile * n_classes + k], wait_one, 0)


def _dispatch_kernel(padded_ref, loff_ref, goff_ref, classcnt_ref, tstart_ref, tchunks_ref,
                     x_ref, slot_ref, xs_ref, stage, sem, *, n_experts):
    tile = pl.program_id(0)
    n_tiles = pl.num_programs(0)
    tt = slot_ref.shape[0] // TOP_K
    p = x_ref.shape[0] // tt
    buf_rows = stage.shape[0] // 2
    n_classes = _size_classes(tt)
    buf = tile % 2

    def make_for(which_buf):
        base = pl.multiple_of(which_buf * buf_rows, CHUNK * p)
        return lambda lrow, grow, nrows: pltpu.make_async_copy(
            stage.at[pl.ds(base + lrow, nrows), :], xs_ref.at[pl.ds(grow, nrows), :],
            sem.at[which_buf])

    @pl.when(tile < 2)
    def _():
        start = pl.multiple_of(buf * buf_rows, CHUNK * p)
        stage[pl.ds(start, buf_rows), :] = jnp.zeros((buf_rows, LANES), F32)

    @pl.when(tile >= 2)
    def _():
        _wait_group_dmas(classcnt_ref, tile - 2, p, n_classes, make_for(buf))

    @pl.when(tile == 0)
    def _():
        rows = CHUNK * p
        tail = lambda grow: pltpu.make_async_copy(
            stage.at[pl.ds(0, rows), :], xs_ref.at[pl.ds(grow, rows), :], sem.at[0])

        def per_expert(ex, total):
            n = tchunks_ref[ex]

            def per_chunk(j, c):
                tail(pl.multiple_of((tstart_ref[ex] + j * CHUNK) * p, rows)).start()
                return c

            lax.fori_loop(0, n, per_chunk, 0)
            return total + n

        total = lax.fori_loop(0, n_experts, per_expert, 0)

        def wait_tail(j, c):
            tail(0).wait()
            return c

        lax.fori_loop(0, total, wait_tail, 0)

    unroll = 8

    def per_tokens(g, carry):
        for u in range(unroll):
            t = g * unroll + u
            v = x_ref[pl.ds(pl.multiple_of(t * p, p), p), :]
            for k in range(TOP_K):
                s = slot_ref[k * tt + t]
                stage[pl.ds(pl.multiple_of(s, p), p), :] = v
        return carry

    lax.fori_loop(0, tt // unroll, per_tokens, 0)
    _start_group_dmas(padded_ref, loff_ref, goff_ref, tile, n_experts, p, n_classes,
                      make_for(buf))

    @pl.when(tile == n_tiles - 1)
    def _():
        @pl.when(tile >= 1)
        def _():
            _wait_group_dmas(classcnt_ref, tile - 1, p, n_classes, make_for(1 - buf))

        _wait_group_dmas(classcnt_ref, tile, p, n_classes, make_for(buf))


def _stage_rows(tt, n_experts):
    rows = TOP_K * tt + n_experts * (CHUNK - 1)
    return -(-rows // CHUNK) * CHUNK


def _dispatch(hlin, slots, tables, tails, *, n_rows, n_experts, d, tt):
    padded, loff, goff, classcnt = tables
    tstart, tchunks = tails
    nt = slots.shape[0] // (TOP_K * tt)
    p = d // LANES
    return pl.pallas_call(
        functools.partial(_dispatch_kernel, n_experts=n_experts),
        grid_spec=pltpu.PrefetchScalarGridSpec(
            num_scalar_prefetch=6,
            grid=(nt,),
            in_specs=[pl.BlockSpec((tt * p, LANES), lambda i, *_: (i, 0)),
                      pl.BlockSpec((TOP_K * tt,), lambda i, *_: (i,),
                                   memory_space=pltpu.SMEM)],
            out_specs=pl.BlockSpec(memory_space=pl.ANY),
            scratch_shapes=[pltpu.VMEM((2 * _stage_rows(tt, n_experts) * p, LANES), F32),
                            pltpu.SemaphoreType.DMA((2,))]),
        out_shape=jax.ShapeDtypeStruct((n_rows * p, LANES), F32),
        compiler_params=_cparams(1),
        name="dispatch",
    )(padded, loff, goff, classcnt, tstart, tchunks, hlin, slots)


def _expert_kernel(be_ref, nu_ref, x_ref, win_ref, bin_ref, wout_ref, bout_ref, y_ref,
                   win_bf, wout_bf):
    b = pl.program_id(0)
    rb = x_ref.shape[0] * LANES // win_ref.shape[1]
    d = win_ref.shape[1]
    f = wout_ref.shape[1]
    p = d // LANES

    @pl.when(b < nu_ref[0])
    def _():
        prev = be_ref[jnp.maximum(b - 1, 0)]

        @pl.when(jnp.logical_or(b == 0, be_ref[b] != prev))
        def _():
            win_bf[...] = win_ref[0].astype(BF16)
            wout_bf[...] = wout_ref[0].astype(BF16)

        x = jnp.concatenate([x_ref[pl.ds(s, rb, stride=p), :] for s in range(p)], axis=-1)
        u = jnp.dot(x.astype(BF16), win_bf[...], preferred_element_type=F32) + bin_ref[0]
        x_glu = jnp.minimum(u[:, :f], SWIGLU_LIMIT)
        x_lin = jnp.clip(u[:, f:], -SWIGLU_LIMIT, SWIGLU_LIMIT)
        act = (x_lin + 1.0) * (x_glu * jax.nn.sigmoid(SWIGLU_ALPHA * x_glu))
        y = jnp.dot(act.astype(BF16), wout_bf[...], preferred_element_type=F32) + bout_ref[0]
        for s in range(p):
            y_ref[pl.ds(s, rb, stride=p), :] = y[:, s * LANES:(s + 1) * LANES]


def _experts(xs, block_e, n_used, w_in, b_in, w_out, b_out, *, layer, rb, nb):
    _, e, d, f2 = w_in.shape
    f = f2 // 2
    p = d // LANES
    last = lambda b, nu: jnp.maximum(jnp.minimum(b, nu[0] - 1), 0)
    blk = lambda b, be, nu: (last(b, nu), 0)
    wsel = lambda b, be, nu: (layer, be[last(b, nu)], 0, 0)
    return pl.pallas_call(
        _expert_kernel,
        grid_spec=pltpu.PrefetchScalarGridSpec(
            num_scalar_prefetch=2,
            grid=(nb,),
            in_specs=[pl.BlockSpec((rb * p, LANES), blk),
                      pl.BlockSpec((None, 1, d, f2), wsel),
                      pl.BlockSpec((None, 1, 1, f2), wsel),
                      pl.BlockSpec((None, 1, f, d), wsel),
                      pl.BlockSpec((None, 1, 1, d), wsel)],
            out_specs=pl.BlockSpec((rb * p, LANES), blk),
            scratch_shapes=[pltpu.VMEM((d, f2), BF16), pltpu.VMEM((f, d), BF16)]),
        out_shape=jax.ShapeDtypeStruct(xs.shape, F32),
        compiler_params=_cparams(1),
        name="experts",
    )(block_e, n_used, xs, w_in, b_in.reshape(-1, e, 1, f2), w_out, b_out.reshape(-1, e, 1, d))


def _combine_kernel(padded_ref, loff_ref, goff_ref, classcnt_ref, ys_ref, slot_ref, gate_ref,
                    xres_ref, mod_ref, lng_ref, lnb_ref, x_out_ref, stage, comb, sem,
                    *, n_experts, alpha):
    tile = pl.program_id(0)
    n_tiles = pl.num_programs(0)
    tt, d = xres_ref.shape
    p = d // LANES
    buf_rows = stage.shape[0] // 2
    n_classes = _size_classes(tt)
    buf = tile % 2

    def make_for(which_buf):
        base = pl.multiple_of(which_buf * buf_rows, CHUNK * p)
        return lambda lrow, grow, nrows: pltpu.make_async_copy(
            ys_ref.at[pl.ds(grow, nrows), :], stage.at[pl.ds(base + lrow, nrows), :],
            sem.at[which_buf])

    def start_tile(which_tile, which_buf):
        _start_group_dmas(padded_ref, loff_ref, goff_ref, which_tile, n_experts, p, n_classes,
                          make_for(which_buf))

    @pl.when(tile == 0)
    def _():
        start_tile(tile, buf)

    @pl.when(tile + 1 < n_tiles)
    def _():
        start_tile(tile + 1, 1 - buf)

    _wait_group_dmas(classcnt_ref, tile, p, n_classes, make_for(buf))
    unroll = 4

    def per_tokens(g, carry):
        for u in range(unroll):
            t = g * unroll + u
            acc = None
            for k in range(TOP_K):
                s = slot_ref[k * tt + t]
                term = gate_ref[k * tt + t] * stage[pl.ds(pl.multiple_of(s, p), p), :]
                acc = term if acc is None else acc + term
            comb[pl.ds(pl.multiple_of(t * p, p), p), :] = acc
        return carry

    lax.fori_loop(0, tt // unroll, per_tokens, 0)
    moe = jnp.concatenate([comb[pl.ds(s, tt, stride=p), :] for s in range(p)], axis=-1)
    g2 = mod_ref[5:6, :]
    x_out_ref[...] = _layer_norm(alpha * xres_ref[...] + (1.0 + g2) * moe,
                                 lng_ref[...], lnb_ref[...])


def _combine(ys, slots, gates, tables, xres, mod, lng, lnb, *, n_experts, seq, tt, alpha):
    padded, loff, goff, classcnt = tables
    t, d = xres.shape
    nt = t // tt
    p = d // LANES
    tiles_per_seq = seq // tt
    smem = lambda: pl.BlockSpec((TOP_K * tt,), lambda i, *_: (i,), memory_space=pltpu.SMEM)
    return pl.pallas_call(
        functools.partial(_combine_kernel, n_experts=n_experts, alpha=alpha),
        grid_spec=pltpu.PrefetchScalarGridSpec(
            num_scalar_prefetch=4,
            grid=(nt,),
            in_specs=[pl.BlockSpec(memory_space=pl.ANY), smem(), smem(),
                      pl.BlockSpec((tt, d), lambda i, *_: (i, 0)),
                      pl.BlockSpec((None, 6, d), lambda i, *_: (i // tiles_per_seq, 0, 0)),
                      pl.BlockSpec((1, d), lambda i, *_: (0, 0)),
                      pl.BlockSpec((1, d), lambda i, *_: (0, 0))],
            out_specs=pl.BlockSpec((tt, d), lambda i, *_: (i, 0)),
            scratch_shapes=[pltpu.VMEM((2 * _stage_rows(tt, n_experts) * p, LANES), F32),
                            pltpu.VMEM((tt * p, LANES), F32),
                            pltpu.SemaphoreType.DMA((2,))]),
        out_shape=jax.ShapeDtypeStruct((t, d), F32),
        compiler_params=_cparams(1),
        name="combine",
    )(padded, loff, goff, classcnt, ys, slots, gates, xres, mod, lng.reshape(1, d), lnb.reshape(1, d))


def _moe(hlin, logt, xres, mod, lng, lnb, w_in, b_in, w_out, b_out, *, layer, seq, tt, rb,
         alpha):
    e, t = logt.shape
    d = xres.shape[1]
    nt = t // tt
    slots, gates, cnt = _route(logt, tt, d // LANES, _stage_rows(tt, e))
    slots = slots.reshape(-1)
    gates = gates.reshape(-1)
    max_rows = TOP_K * t + nt * e * (CHUNK - 1)
    nb = -(-max_rows // rb) + e
    padded, loff, goff, classcnt, tstart, tchunks, block_e, n_used = _plan(
        cnt[:, :, 0], rb, nb, _size_classes(tt))
    tables = (padded, loff, goff, classcnt)
    xs = _dispatch(hlin, slots, tables, (tstart, tchunks), n_rows=nb * rb, n_experts=e, d=d,
                   tt=tt)
    ys = _experts(xs, block_e, n_used, w_in, b_in, w_out, b_out, layer=layer, rb=rb, nb=nb)
    return _combine(ys, slots, gates, tables, xres, mod, lng, lnb,
                    n_experts=e, seq=seq, tt=tt, alpha=alpha)


def _qkv_kernel(x_ref, mod_ref, qwt_ref, kw_ref, vwt_ref, qt_ref, k_ref, vt_ref):
    x = x_ref[...]
    sh1 = mod_ref[0:1, :]
    sc1 = mod_ref[1:2, :]
    h = (x * (1.0 + sc1) + sh1).astype(BF16)
    nt_dims = (((1,), (1,)), ((), ()))
    qt_ref[0] = lax.dot_general(qwt_ref[...], h, nt_dims,
                                preferred_element_type=F32).astype(BF16)
    tk = vt_ref.shape[3]
    ngrp = tk // 8
    r = lax.broadcasted_iota(jnp.int32, (tk, tk), 0)
    s = lax.broadcasted_iota(jnp.int32, (tk, tk), 1)
    perm = jnp.where(s == (r & 7) * ngrp + (r >> 3), 1.0, 0.0).astype(BF16)
    xb = x.astype(BF16)
    xp = jnp.concatenate(
        [jnp.dot(perm, xb[j * tk:(j + 1) * tk, :], preferred_element_type=F32)
         for j in range(vt_ref.shape[1])], axis=0).astype(BF16)
    k_ref[...] = jnp.dot(xp, kw_ref[...], preferred_element_type=F32).astype(BF16)
    vt = lax.dot_general(vwt_ref[...], xp, nt_dims, preferred_element_type=F32).astype(BF16)
    for j in range(vt_ref.shape[1]):
        vt_ref[0, j] = vt[:, j * tk:(j + 1) * tk]


def _qkv(x2d, mod, qwt, kw, vwt, *, bsz, seq, tl, tk):
    t, d = x2d.shape
    nlt = seq // tl
    full = pl.BlockSpec((d, d), lambda b, l: (0, 0))
    return pl.pallas_call(
        _qkv_kernel,
        grid=(bsz, nlt),
        in_specs=[pl.BlockSpec((tl, d), lambda b, l: (b * nlt + l, 0)),
                  pl.BlockSpec((None, 6, d), lambda b, l: (b, 0, 0)),
                  full, full, full],
        out_specs=[pl.BlockSpec((1, d, tl), lambda b, l: (b, 0, l)),
                   pl.BlockSpec((tl, d), lambda b, l: (b * nlt + l, 0)),
                   pl.BlockSpec((1, tl // tk, d, tk), lambda b, l: (b, l, 0, 0))],
        out_shape=[jax.ShapeDtypeStruct((bsz, d, seq), BF16),
                   jax.ShapeDtypeStruct((t, d), BF16),
                   jax.ShapeDtypeStruct((bsz, seq // tk, d, tk), BF16)],
        compiler_params=_cparams(2),
        name="qkv",
    )(x2d, mod, qwt, kw, vwt)


def _stick_block(z_ref, a_ref, carry, masked):
    tk, tq = z_ref.shape
    ngrp = tk // 8
    sub = lax.broadcasted_iota(jnp.int32, (8, tq), 0)
    lane = lax.broadcasted_iota(jnp.int32, (8, tq), 1)
    causal = lambda g: sub * ngrp + g < lane
    run = None
    for g in reversed(range(ngrp)):
        z = z_ref[g * 8:(g + 1) * 8, :]
        sp = jnp.maximum(z, 0.0) + LOG2E * jnp.log(1.0 + jnp.exp2(-jnp.abs(z)))
        if masked:
            sp = jnp.where(causal(g), sp, 0.0)
        run = sp if run is None else run + sp
        z_ref[g * 8:(g + 1) * 8, :] = z - run
    tot = run
    s = tot
    for k in (1, 2, 4):
        s = s + jnp.where(sub < 8 - k, pltpu.roll(s, 8 - k, axis=0), 0.0)
    later = s - tot + carry
    for g in range(0, ngrp, 2):
        pair = []
        for gg in (g, g + 1):
            a = jnp.exp2(z_ref[gg * 8:(gg + 1) * 8, :] - later)
            if masked:
                a = jnp.where(causal(gg), a, 0.0)
            pair.append(a)
        a_ref[g * 8:(g + 2) * 8, :] = jnp.concatenate(pair, axis=0).astype(BF16)
    return carry + jnp.broadcast_to(s[0:1, :], (8, tq))


def _attn_kernel(qt_ref, k_ref, vt_ref, o_ref, z0, z1, a0, a1, acc_buf, *, head_dim):
    qi = pl.program_id(2)
    tq = qt_ref.shape[2]
    tk = vt_ref.shape[3]
    hp = qt_ref.shape[1] // head_dim
    row = lax.broadcasted_iota(jnp.int32, (hp * head_dim, tq), 0)
    qf = qt_ref[0].astype(F32)
    qms = [jnp.where(jnp.logical_and(row >= h * head_dim, row < (h + 1) * head_dim), qf, 0.0)
           .astype(BF16) for h in range(hp)]

    def scores(kb, z_out):
        kblk = k_ref[pl.ds(pl.multiple_of(kb * tk, tk), tk), :]
        for h in range(hp):
            z_out[h] = jnp.dot(kblk, qms[h], preferred_element_type=F32)

    def weighted_values(kb, a_in):
        return [jnp.dot(vt_ref[0, kb, h * head_dim:(h + 1) * head_dim, :], a_in[h],
                        preferred_element_type=F32) for h in range(hp)]

    def step(kb, z_in, z_out, a_in, a_out, carries):
        pv = weighted_values(kb + 1, a_in)
        scores(jnp.maximum(kb - 1, 0), z_out)
        carries = tuple(_stick_block(z_in.at[h], a_out.at[h], carries[h], False)
                        for h in range(hp))
        for h in range(hp):
            acc_buf[h] += pv[h]
        return carries

    def finish(a_last):
        pv = weighted_values(0, a_last)
        o_ref[...] = jnp.concatenate([(acc_buf[h] + pv[h]).T for h in range(hp)],
                                     axis=-1).astype(o_ref.dtype)

    scores(qi, z0)
    scores(jnp.maximum(qi - 1, 0), z1)
    zero = jnp.zeros((8, tq), F32)
    carries = tuple(_stick_block(z0.at[h], a0.at[h], zero, True) for h in range(hp))
    acc_buf[...] = jnp.zeros(acc_buf.shape, F32)

    def pair(i, carries):
        kb = qi - 1 - 2 * i
        carries = step(kb, z1, z0, a0, a1, carries)
        return step(kb - 1, z0, z1, a1, a0, carries)

    carries = lax.fori_loop(0, qi // 2, pair, carries)

    @pl.when(qi % 2 == 1)
    def _():
        step(0, z1, z0, a0, a1, carries)
        finish(a1)

    @pl.when(qi % 2 == 0)
    def _():
        finish(a0)


def _attention(qt, k2d, vt, *, bsz, seq, tq):
    d = qt.shape[1]
    head_dim = d // N_HEADS
    tk = vt.shape[3]
    assert tk == tq, "the diagonal mask assumes square blocks"
    nq = seq // tq
    ncol = d // LANES
    hp = LANES // head_dim
    return pl.pallas_call(
        functools.partial(_attn_kernel, head_dim=head_dim),
        grid=(bsz, ncol, nq),
        in_specs=[pl.BlockSpec((1, LANES, tq), lambda b, c, q: (b, c, q)),
                  pl.BlockSpec((seq, LANES), lambda b, c, q: (b, c)),
                  pl.BlockSpec((1, seq // tk, LANES, tk), lambda b, c, q: (b, 0, c, 0))],
        out_specs=pl.BlockSpec((tq, LANES), lambda b, c, q: (b * nq + q, c)),
        out_shape=jax.ShapeDtypeStruct((bsz * seq, d), BF16),
        scratch_shapes=[pltpu.VMEM((hp, tk, tq), F32), pltpu.VMEM((hp, tk, tq), F32),
                        pltpu.VMEM((hp, tk, tq), BF16), pltpu.VMEM((hp, tk, tq), BF16),
                        pltpu.VMEM((hp, head_dim, tq), F32)],
        compiler_params=_cparams(3),
        name="attention",
    )(qt, k2d, vt)


def _oproj_kernel(o_ref, xres_ref, mod_ref, ow_ref, lng_ref, lnb_ref, rwt_ref, rb_ref,
                  x_out_ref, hlin_ref, logt_ref, *, alpha):
    y = jnp.dot(o_ref[...], ow_ref[...], preferred_element_type=F32)
    _post_norm_route(xres_ref[...], y, mod_ref, lng_ref, lnb_ref, rwt_ref, rb_ref,
                     x_out_ref, hlin_ref, logt_ref, alpha)


def _oproj(o2d, xres, mod, ow, lng, lnb, rwt, rb, *, seq, tl, alpha):
    t, d = xres.shape
    e = rwt.shape[0]
    p = d // LANES
    nlt = seq // tl
    full = lambda shape: pl.BlockSpec(shape, lambda i: (0,) * len(shape))
    return pl.pallas_call(
        functools.partial(_oproj_kernel, alpha=alpha),
        grid=(t // tl,),
        in_specs=[pl.BlockSpec((tl, d), lambda i: (i, 0)),
                  pl.BlockSpec((tl, d), lambda i: (i, 0)),
                  pl.BlockSpec((None, 6, d), lambda i: (i // nlt, 0, 0)),
                  full((d, d)), full((1, d)), full((1, d)), full((e, d)), full((e, 1))],
        out_specs=[pl.BlockSpec((tl, d), lambda i: (i, 0)),
                   pl.BlockSpec((tl * p, LANES), lambda i: (i, 0)),
                   pl.BlockSpec((e, tl), lambda i: (0, i))],
        out_shape=[jax.ShapeDtypeStruct((t, d), F32),
                   jax.ShapeDtypeStruct((t * p, LANES), F32),
                   jax.ShapeDtypeStruct((e, t), F32)],
        compiler_params=_cparams(1),
        name="oproj",
    )(o2d, xres, mod, ow, lng.reshape(1, d), lnb.reshape(1, d), rwt, rb.reshape(e, 1))


def kernel(x, c, ada_w, ada_b, ln_g, ln_b, cv_w1, cv_b1, cv_dw, cv_db, cv_ln_g, cv_ln_b,
           cv_w2, cv_b2, kv_w, q_w, o_w, router_w, router_b, moe_w_in, moe_b_in,
           moe_w_out, moe_b_out):
    bsz, seq, d = x.shape
    depth = ada_w.shape[0]
    n_a = cv_w1.shape[0]
    assert depth == 2 and n_a == 1 and q_w.shape[0] == 1, "one conv layer then one attention layer"
    alpha = (2.0 * depth) ** 0.25
    head_dim = d // N_HEADS
    tl = min(512, seq)
    tt = min(512, seq)
    rb = 512
    tq = min(256, seq)

    mod = _ada(c, ada_w, ada_b).reshape(depth, bsz, 6, d)
    x0 = x.reshape(bsz * seq, d)
    rwt = jnp.swapaxes(router_w, 1, 2)

    x1, hlin, logt = _conv_mixer(
        x0, mod[0], cv_w1[0].astype(BF16), cv_b1[0], cv_dw[0], cv_db[0], cv_ln_g[0], cv_ln_b[0],
        cv_w2[0].astype(BF16), cv_b2[0], ln_g[0, 0], ln_b[0, 0], rwt[0], router_b[0],
        bsz=bsz, seq=seq, tl=tl, alpha=alpha)
    x2 = _moe(hlin, logt, x1, mod[0], ln_g[0, 1], ln_b[0, 1], moe_w_in, moe_b_in,
              moe_w_out, moe_b_out, layer=0, seq=seq, tt=tt, rb=rb, alpha=alpha)

    scale = head_dim ** -0.5
    qwt = (q_w[0] * (scale * LOG2E)).T.astype(BF16)
    kw = kv_w[:, :d].astype(BF16)
    vwt = kv_w[:, d:].T.astype(BF16)
    qt, k2d, vt = _qkv(x2, mod[1], qwt, kw, vwt, bsz=bsz, seq=seq, tl=tl, tk=tq)
    o2d = _attention(qt, k2d, vt, bsz=bsz, seq=seq, tq=tq)
    x3, hlin, logt = _oproj(o2d, x2, mod[1], o_w[0].astype(BF16), ln_g[1, 0], ln_b[1, 0],
                            rwt[1], router_b[1], seq=seq, tl=tl, alpha=alpha)
    x4 = _moe(hlin, logt, x3, mod[1], ln_g[1, 1], ln_b[1, 1], moe_w_in, moe_b_in,
              moe_w_out, moe_b_out, layer=1, seq=seq, tt=tt, rb=rb, alpha=alpha)
    return x4.reshape(bsz, seq, d)
```

```python
import functools
import math

import jax
import jax.numpy as jnp
from jax import lax
from jax.experimental import pallas as pl
from jax.experimental.pallas import tpu as pltpu

F32 = jnp.float32
BF16 = jnp.bfloat16
HIGHEST = lax.Precision.HIGHEST

LANES = 128
N_HEADS = 16
TOP_K = 4
CONV_HALO = 32
SWIGLU_LIMIT = 7.0
SWIGLU_ALPHA = 1.702
LN_EPS = 1e-5
LOG2E = 1.4426950408889634
CHUNK = 8
VMEM_LIMIT = 56 * 1024 * 1024


def _cparams(n_axes):
    return pltpu.CompilerParams(
        dimension_semantics=("arbitrary",) * n_axes, vmem_limit_bytes=VMEM_LIMIT)


def _layer_norm(x, g, b):
    mu = jnp.mean(x, axis=-1, keepdims=True)
    xc = x - mu
    var = jnp.mean(xc * xc, axis=-1, keepdims=True)
    return xc * lax.rsqrt(var + LN_EPS) * g + b


def _silu(x):
    return x * jax.nn.sigmoid(x)


def _ada_kernel(c_ref, w_ref, b_ref, o_ref):
    o_ref[0] = jnp.dot(_silu(c_ref[...]), w_ref[0], precision=HIGHEST,
                       preferred_element_type=F32) + b_ref[0]


def _ada(c, ada_w, ada_b):
    depth, d, n = ada_w.shape
    bsz = c.shape[0]
    tn = d
    return pl.pallas_call(
        _ada_kernel,
        grid=(depth, n // tn),
        in_specs=[pl.BlockSpec((bsz, d), lambda l, j: (0, 0)),
                  pl.BlockSpec((1, d, tn), lambda l, j: (l, 0, j)),
                  pl.BlockSpec((1, 1, tn), lambda l, j: (l, 0, j))],
        out_specs=pl.BlockSpec((1, bsz, tn), lambda l, j: (l, 0, j)),
        out_shape=jax.ShapeDtypeStruct((depth, bsz, n), F32),
        compiler_params=_cparams(2),
        name="ada",
    )(c, ada_w, ada_b.reshape(depth, 1, n))


def _post_norm_route(xres, y, mod_ref, lng_ref, lnb_ref, rwt_ref, rb_ref,
                     x_out_ref, hlin_ref, logt_ref, alpha):
    tl, d = xres.shape
    g1 = mod_ref[2:3, :]
    sh2 = mod_ref[3:4, :]
    sc2 = mod_ref[4:5, :]
    x1 = _layer_norm(alpha * xres + (1.0 + g1) * y, lng_ref[...], lnb_ref[...])
    x_out_ref[...] = x1
    h2 = x1 * (1.0 + sc2) + sh2
    p = d // LANES
    for s in range(p):
        hlin_ref[pl.ds(s, tl, stride=p), :] = h2[:, s * LANES:(s + 1) * LANES]
    logt_ref[...] = lax.dot_general(
        rwt_ref[...], h2, (((1,), (1,)), ((), ())), precision=HIGHEST,
        preferred_element_type=F32) + rb_ref[...]


def _conv_kernel(x_ref, mod_ref, w1_ref, b1_ref, dw_ref, db_ref, clg_ref, clb_ref,
                 w2_ref, b2_ref, lng_ref, lnb_ref, rwt_ref, rb_ref,
                 x_out_ref, hlin_ref, logt_ref,
                 ubuf, sbuf, dwb, cbuf, *, alpha, width):
    tl, d = x_ref.shape
    first_tile = pl.program_id(1) == 0

    @pl.when(jnp.logical_and(pl.program_id(0) == 0, first_tile))
    def _():
        for j in range(width):
            dwb[j] = jnp.broadcast_to(dw_ref[j:j + 1, :], (8, d))

    x = x_ref[...]
    sh1 = mod_ref[0:1, :]
    sc1 = mod_ref[1:2, :]
    h = (x * (1.0 + sc1) + sh1).astype(BF16)
    u = jnp.dot(h, w1_ref[...], preferred_element_type=F32) + b1_ref[...]
    glu = u[:, :d] * jax.nn.sigmoid(u[:, d:])

    @pl.when(first_tile)
    def _():
        ubuf[0:CONV_HALO, :] = jnp.zeros((CONV_HALO, d), F32)

    ubuf[CONV_HALO:CONV_HALO + tl, :] = glu
    for r in range(1, 8):
        sbuf[r - 1] = ubuf[pl.ds(r, tl + CONV_HALO - 8), :]

    off0 = CONV_HALO - (width - 1)
    cw = 512
    rows = 16

    def body(i, carry):
        base = pl.multiple_of(i * rows, rows)
        for c in range(d // cw):
            cs = slice(c * cw, (c + 1) * cw)
            accs = [jnp.broadcast_to(db_ref[:, cs], (8, cw)) for _ in range(rows // 8)]
            for j in range(width):
                q, r = divmod(off0 + j, 8)
                w = dwb[j, :, cs]
                for a in range(rows // 8):
                    start = base + 8 * (q + a)
                    if r == 0:
                        v = ubuf[pl.ds(start, 8), cs]
                    else:
                        v = sbuf[r - 1, pl.ds(start, 8), cs]
                    accs[a] = accs[a] + w * v
            for a in range(rows // 8):
                cbuf[pl.ds(base + 8 * a, 8), cs] = accs[a]
        return carry

    lax.fori_loop(0, tl // rows, body, 0)
    ubuf[0:CONV_HALO, :] = ubuf[tl:tl + CONV_HALO, :]

    v = _silu(_layer_norm(cbuf[...], clg_ref[...], clb_ref[...]))
    y = jnp.dot(v.astype(BF16), w2_ref[...], preferred_element_type=F32) + b2_ref[...]
    _post_norm_route(x, y, mod_ref, lng_ref, lnb_ref, rwt_ref, rb_ref,
                     x_out_ref, hlin_ref, logt_ref, alpha)


def _conv_mixer(x2d, mod, w1, b1, dw, db, clg, clb, w2, b2, lng, lnb, rwt, rb,
                *, bsz, seq, tl, alpha):
    t, d = x2d.shape
    e = rwt.shape[0]
    width = dw.shape[0]
    p = d // LANES
    nlt = seq // tl
    row = lambda a: a.reshape(1, -1)
    full = lambda shape: pl.BlockSpec(shape, lambda b, l: (0,) * len(shape))
    return pl.pallas_call(
        functools.partial(_conv_kernel, alpha=alpha, width=width),
        grid=(bsz, nlt),
        in_specs=[pl.BlockSpec((tl, d), lambda b, l: (b * nlt + l, 0)),
                  pl.BlockSpec((None, 6, d), lambda b, l: (b, 0, 0)),
                  full((d, 2 * d)), full((1, 2 * d)), full((width, d)), full((1, d)),
                  full((1, d)), full((1, d)), full((d, d)), full((1, d)),
                  full((1, d)), full((1, d)), full((e, d)), full((e, 1))],
        out_specs=[pl.BlockSpec((tl, d), lambda b, l: (b * nlt + l, 0)),
                   pl.BlockSpec((tl * p, LANES), lambda b, l: (b * nlt + l, 0)),
                   pl.BlockSpec((e, tl), lambda b, l: (0, b * nlt + l))],
        out_shape=[jax.ShapeDtypeStruct((t, d), F32),
                   jax.ShapeDtypeStruct((t * p, LANES), F32),
                   jax.ShapeDtypeStruct((e, t), F32)],
        scratch_shapes=[pltpu.VMEM((tl + CONV_HALO, d), F32),
                        pltpu.VMEM((7, tl + CONV_HALO - 8, d), F32),
                        pltpu.VMEM((width, 8, d), F32),
                        pltpu.VMEM((tl, d), F32)],
        compiler_params=_cparams(2),
        name="conv_mixer",
    )(x2d, mod, w1, row(b1), dw, row(db), row(clg), row(clb), w2, row(b2),
      row(lng), row(lnb), rwt, rb.reshape(e, 1))


def _route_kernel(logt_ref, slot_ref, gate_ref, cnt_ref, *, p, stage_rows):
    e, tt = logt_ref.shape
    l = logt_ref[...]
    iota_e = lax.broadcasted_iota(jnp.int32, (e, tt), 0).astype(F32)
    sels, vals = [], []
    for _ in range(TOP_K):
        m = jnp.max(l, axis=0, keepdims=True)
        idx = jnp.min(jnp.where(l == m, iota_e, float(e)), axis=0, keepdims=True)
        sels.append(idx)
        vals.append(m)
        l = jnp.where(iota_e == idx, -jnp.inf, l)
    exps = [jnp.exp(v - vals[0]) for v in vals]
    denom = exps[0] + exps[1] + exps[2] + exps[3]
    onehots = [iota_e == s for s in sels]
    member = jnp.zeros((e, tt), F32)
    for oh in onehots:
        member = member + jnp.where(oh, 1.0, 0.0)
    tri = jnp.where(lax.broadcasted_iota(jnp.int32, (tt, tt), 0)
                    < lax.broadcasted_iota(jnp.int32, (tt, tt), 1), 1.0, 0.0).astype(BF16)
    rank = jnp.dot(member.astype(BF16), tri, preferred_element_type=F32)
    cnt = jnp.sum(member, axis=1, keepdims=True)
    padded = jnp.floor((cnt + (CHUNK - 1)) * (1.0 / CHUNK)) * CHUNK
    low = jnp.where(lax.broadcasted_iota(jnp.int32, (e, e), 1)
                    < lax.broadcasted_iota(jnp.int32, (e, e), 0), 1.0, 0.0)
    loff = jnp.dot(low, jnp.broadcast_to(padded, (e, tt)), precision=HIGHEST,
                   preferred_element_type=F32)
    pos = loff + rank + (pl.program_id(0) % 2 * stage_rows).astype(F32)
    for k in range(TOP_K):
        slot = jnp.sum(jnp.where(onehots[k], pos, 0.0), axis=0, keepdims=True)
        slot_ref[0, k:k + 1, :] = (slot * p).astype(jnp.int32)
        gate_ref[0, k:k + 1, :] = exps[k] / denom
    cnt_ref[0] = jnp.broadcast_to(cnt, (e, LANES)).astype(jnp.int32)


def _route(logt, tt, p, stage_rows):
    e, t = logt.shape
    nt = t // tt
    return pl.pallas_call(
        functools.partial(_route_kernel, p=p, stage_rows=stage_rows),
        grid=(nt,),
        in_specs=[pl.BlockSpec((e, tt), lambda i: (0, i))],
        out_specs=[pl.BlockSpec((1, TOP_K, tt), lambda i: (i, 0, 0)),
                   pl.BlockSpec((1, TOP_K, tt), lambda i: (i, 0, 0)),
                   pl.BlockSpec((1, e, LANES), lambda i: (i, 0, 0))],
        out_shape=[jax.ShapeDtypeStruct((nt, TOP_K, tt), jnp.int32),
                   jax.ShapeDtypeStruct((nt, TOP_K, tt), F32),
                   jax.ShapeDtypeStruct((nt, e, LANES), jnp.int32)],
        compiler_params=_cparams(1),
        name="route",
    )(logt)


def _plan(cnt, rb, nb, n_classes):
    nt, e = cnt.shape
    padded = ((cnt + CHUNK - 1) // CHUNK) * CHUNK
    loff = jnp.cumsum(padded, axis=1) - padded
    rows_e = jnp.sum(padded, axis=0)
    blocks_e = (rows_e + rb - 1) // rb
    bend = jnp.cumsum(blocks_e)
    pstart = (bend - blocks_e) * rb
    goff = pstart[None, :] + jnp.cumsum(padded, axis=0) - padded
    block_ids = jnp.arange(nb, dtype=bend.dtype)
    block_e = jnp.minimum(jnp.sum(bend[None, :] <= block_ids[:, None], axis=1), e - 1)
    chunks = padded // CHUNK
    classcnt = jnp.stack([jnp.sum((chunks >> k) & 1, axis=1) for k in range(n_classes)], axis=1)
    tail_start = pstart + rows_e
    tail_chunks = (blocks_e * rb - rows_e) // CHUNK
    i32 = lambda a: a.astype(jnp.int32)
    return (i32(padded.reshape(-1)), i32(loff.reshape(-1)), i32(goff.reshape(-1)),
            i32(classcnt.reshape(-1)), i32(tail_start), i32(tail_chunks), i32(block_e), i32(bend[-1:]))


def _size_classes(tt):
    return (tt // CHUNK).bit_length()


def _start_group_dmas(padded_ref, loff_ref, goff_ref, tile, n_experts, p, n_classes, make):
    def per_expert(ex, carry):
        idx = tile * n_experts + ex
        n = padded_ref[idx] // CHUNK
        lo = loff_ref[idx]
        go = goff_ref[idx]
        for k in range(n_classes):
            @pl.when((n >> k) & 1 == 1)
            def _():
                done = (n >> (k + 1)) << (k + 1)
                lrow = pl.multiple_of((lo + done * CHUNK) * p, CHUNK * p)
                grow = pl.multiple_of((go + done * CHUNK) * p, CHUNK * p)
                make(lrow, grow, (CHUNK << k) * p).start()
        return carry

    lax.fori_loop(0, n_experts, per_expert, 0)


def _wait_group_dmas(classcnt_ref, tile, p, n_classes, make):
    for k in range(n_classes):
        def wait_one(j, c, k=k):
            make(0, 0, (CHUNK << k) * p).wait()
            return c

        lax.fori_loop(0, classcnt_ref[tile * n_classes + k], wait_one, 0)


def _dispatch_kernel(padded_ref, loff_ref, goff_ref, classcnt_ref, tstart_ref, tchunks_ref,
                     x_ref, slot_ref, xs_ref, stage, sem, *, n_experts):
    tile = pl.program_id(0)
    n_tiles = pl.num_programs(0)
    tt = slot_ref.shape[0] // TOP_K
    p = x_ref.shape[0] // tt
    buf_rows = stage.shape[0] // 2
    n_classes = _size_classes(tt)
    buf = tile % 2

    def make_for(which_buf):
        base = pl.multiple_of(which_buf * buf_rows, CHUNK * p)
        return lambda lrow, grow, nrows: pltpu.make_async_copy(
            stage.at[pl.ds(base + lrow, nrows), :], xs_ref.at[pl.ds(grow, nrows), :],
            sem.at[which_buf])

    @pl.when(tile < 2)
    def _():
        start = pl.multiple_of(buf * buf_rows, CHUNK * p)
        stage[pl.ds(start, buf_rows), :] = jnp.zeros((buf_rows, LANES), F32)

    @pl.when(tile >= 2)
    def _():
        _wait_group_dmas(classcnt_ref, tile - 2, p, n_classes, make_for(buf))

    @pl.when(tile == 0)
    def _():
        rows = CHUNK * p
        tail = lambda grow: pltpu.make_async_copy(
            stage.at[pl.ds(0, rows), :], xs_ref.at[pl.ds(grow, rows), :], sem.at[0])

        def per_expert(ex, total):
            n = tchunks_ref[ex]

            def per_chunk(j, c):
                tail(pl.multiple_of((tstart_ref[ex] + j * CHUNK) * p, rows)).start()
                return c

            lax.fori_loop(0, n, per_chunk, 0)
            return total + n

        total = lax.fori_loop(0, n_experts, per_expert, 0)

        def wait_tail(j, c):
            tail(0).wait()
            return c

        lax.fori_loop(0, total, wait_tail, 0)

    unroll = 8

    def per_tokens(g, carry):
        for u in range(unroll):
            t = g * unroll + u
            v = x_ref[pl.ds(pl.multiple_of(t * p, p), p), :]
            for k in range(TOP_K):
                s = slot_ref[k * tt + t]
                stage[pl.ds(pl.multiple_of(s, p), p), :] = v
        return carry

    lax.fori_loop(0, tt // unroll, per_tokens, 0)
    _start_group_dmas(padded_ref, loff_ref, goff_ref, tile, n_experts, p, n_classes,
                      make_for(buf))

    @pl.when(tile == n_tiles - 1)
    def _():
        @pl.when(tile >= 1)
        def _():
            _wait_group_dmas(classcnt_ref, tile - 1, p, n_classes, make_for(1 - buf))

        _wait_group_dmas(classcnt_ref, tile, p, n_classes, make_for(buf))


def _stage_rows(tt, n_experts):
    rows = TOP_K * tt + n_experts * (CHUNK - 1)
    return -(-rows // CHUNK) * CHUNK


def _dispatch(hlin, slots, tables, tails, *, n_rows, n_experts, d, tt):
    padded, loff, goff, classcnt = tables
    tstart, tchunks = tails
    nt = slots.shape[0] // (TOP_K * tt)
    p = d // LANES
    return pl.pallas_call(
        functools.partial(_dispatch_kernel, n_experts=n_experts),
        grid_spec=pltpu.PrefetchScalarGridSpec(
            num_scalar_prefetch=6,
            grid=(nt,),
            in_specs=[pl.BlockSpec((tt * p, LANES), lambda i, *_: (i, 0)),
                      pl.BlockSpec((TOP_K * tt,), lambda i, *_: (i,),
                                   memory_space=pltpu.SMEM)],
            out_specs=pl.BlockSpec(memory_space=pl.ANY),
            scratch_shapes=[pltpu.VMEM((2 * _stage_rows(tt, n_experts) * p, LANES), F32),
                            pltpu.SemaphoreType.DMA((2,))]),
        out_shape=jax.ShapeDtypeStruct((n_rows * p, LANES), F32),
        compiler_params=_cparams(1),
        name="dispatch",
    )(padded, loff, goff, classcnt, tstart, tchunks, hlin, slots)


def _expert_kernel(be_ref, nu_ref, x_ref, win_ref, bin_ref, wout_ref, bout_ref, y_ref,
                   win_bf, wout_bf):
    b = pl.program_id(0)
    rb = x_ref.shape[0] * LANES // win_ref.shape[1]
    d = win_ref.shape[1]
    f = wout_ref.shape[1]
    p = d // LANES

    @pl.when(b < nu_ref[0])
    def _():
        prev = be_ref[jnp.maximum(b - 1, 0)]

        @pl.when(jnp.logical_or(b == 0, be_ref[b] != prev))
        def _():
            win_bf[...] = win_ref[0].astype(BF16)
            wout_bf[...] = wout_ref[0].astype(BF16)

        x = jnp.concatenate([x_ref[pl.ds(s, rb, stride=p), :] for s in range(p)], axis=-1)
        u = jnp.dot(x.astype(BF16), win_bf[...], preferred_element_type=F32) + bin_ref[0]
        x_glu = jnp.minimum(u[:, :f], SWIGLU_LIMIT)
        x_lin = jnp.clip(u[:, f:], -SWIGLU_LIMIT, SWIGLU_LIMIT)
        act = (x_lin + 1.0) * (x_glu * jax.nn.sigmoid(SWIGLU_ALPHA * x_glu))
        y = jnp.dot(act.astype(BF16), wout_bf[...], preferred_element_type=F32) + bout_ref[0]
        for s in range(p):
            y_ref[pl.ds(s, rb, stride=p), :] = y[:, s * LANES:(s + 1) * LANES]


def _experts(xs, block_e, n_used, w_in, b_in, w_out, b_out, *, layer, rb, nb):
    _, e, d, f2 = w_in.shape
    f = f2 // 2
    p = d // LANES
    last = lambda b, nu: jnp.maximum(jnp.minimum(b, nu[0] - 1), 0)
    blk = lambda b, be, nu: (last(b, nu), 0)
    wsel = lambda b, be, nu: (layer, be[last(b, nu)], 0, 0)
    return pl.pallas_call(
        _expert_kernel,
        grid_spec=pltpu.PrefetchScalarGridSpec(
            num_scalar_prefetch=2,
            grid=(nb,),
            in_specs=[pl.BlockSpec((rb * p, LANES), blk),
                      pl.BlockSpec((None, 1, d, f2), wsel),
                      pl.BlockSpec((None, 1, 1, f2), wsel),
                      pl.BlockSpec((None, 1, f, d), wsel),
                      pl.BlockSpec((None, 1, 1, d), wsel)],
            out_specs=pl.BlockSpec((rb * p, LANES), blk),
            scratch_shapes=[pltpu.VMEM((d, f2), BF16), pltpu.VMEM((f, d), BF16)]),
        out_shape=jax.ShapeDtypeStruct(xs.shape, F32),
        compiler_params=_cparams(1),
        name="experts",
    )(block_e, n_used, xs, w_in, b_in.reshape(-1, e, 1, f2), w_out, b_out.reshape(-1, e, 1, d))


def _combine_kernel(padded_ref, loff_ref, goff_ref, classcnt_ref, ys_ref, slot_ref, gate_ref,
                    xres_ref, mod_ref, lng_ref, lnb_ref, x_out_ref, stage, comb, sem,
                    *, n_experts, alpha):
    tile = pl.program_id(0)
    n_tiles = pl.num_programs(0)
    tt, d = xres_ref.shape
    p = d // LANES
    buf_rows = stage.shape[0] // 2
    n_classes = _size_classes(tt)
    buf = tile % 2

    def make_for(which_buf):
        base = pl.multiple_of(which_buf * buf_rows, CHUNK * p)
        return lambda lrow, grow, nrows: pltpu.make_async_copy(
            ys_ref.at[pl.ds(grow, nrows), :], stage.at[pl.ds(base + lrow, nrows), :],
            sem.at[which_buf])

    def start_tile(which_tile, which_buf):
        _start_group_dmas(padded_ref, loff_ref, goff_ref, which_tile, n_experts, p, n_classes,
                          make_for(which_buf))

    @pl.when(tile == 0)
    def _():
        start_tile(tile, buf)

    @pl.when(tile + 1 < n_tiles)
    def _():
        start_tile(tile + 1, 1 - buf)

    _wait_group_dmas(classcnt_ref, tile, p, n_classes, make_for(buf))
    unroll = 4

    def per_tokens(g, carry):
        for u in range(unroll):
            t = g * unroll + u
            acc = None
            for k in range(TOP_K):
                s = slot_ref[k * tt + t]
                term = gate_ref[k * tt + t] * stage[pl.ds(pl.multiple_of(s, p), p), :]
                acc = term if acc is None else acc + term
            comb[pl.ds(pl.multiple_of(t * p, p), p), :] = acc
        return carry

    lax.fori_loop(0, tt // unroll, per_tokens, 0)
    moe = jnp.concatenate([comb[pl.ds(s, tt, stride=p), :] for s in range(p)], axis=-1)
    g2 = mod_ref[5:6, :]
    x_out_ref[...] = _layer_norm(alpha * xres_ref[...] + (1.0 + g2) * moe,
                                 lng_ref[...], lnb_ref[...])


def _combine(ys, slots, gates, tables, xres, mod, lng, lnb, *, n_experts, seq, tt, alpha):
    padded, loff, goff, classcnt = tables
    t, d = xres.shape
    nt = t // tt
    p = d // LANES
    tiles_per_seq = seq // tt
    smem = lambda: pl.BlockSpec((TOP_K * tt,), lambda i, *_: (i,), memory_space=pltpu.SMEM)
    return pl.pallas_call(
        functools.partial(_combine_kernel, n_experts=n_experts, alpha=alpha),
        grid_spec=pltpu.PrefetchScalarGridSpec(
            num_scalar_prefetch=4,
            grid=(nt,),
            in_specs=[pl.BlockSpec(memory_space=pl.ANY), smem(), smem(),
                      pl.BlockSpec((tt, d), lambda i, *_: (i, 0)),
                      pl.BlockSpec((None, 6, d), lambda i, *_: (i // tiles_per_seq, 0, 0)),
                      pl.BlockSpec((1, d), lambda i, *_: (0, 0)),
                      pl.BlockSpec((1, d), lambda i, *_: (0, 0))],
            out_specs=pl.BlockSpec((tt, d), lambda i, *_: (i, 0)),
            scratch_shapes=[pltpu.VMEM((2 * _stage_rows(tt, n_experts) * p, LANES), F32),
                            pltpu.VMEM((tt * p, LANES), F32),
                            pltpu.SemaphoreType.DMA((2,))]),
        out_shape=jax.ShapeDtypeStruct((t, d), F32),
        compiler_params=_cparams(1),
        name="combine",
    )(padded, loff, goff, classcnt, ys, slots, gates, xres, mod, lng.reshape(1, d), lnb.reshape(1, d))


def _moe(hlin, logt, xres, mod, lng, lnb, w_in, b_in, w_out, b_out, *, layer, seq, tt, rb,
         alpha):
    e, t = logt.shape
    d = xres.shape[1]
    nt = t // tt
    slots, gates, cnt = _route(logt, tt, d // LANES, _stage_rows(tt, e))
    slots = slots.reshape(-1)
    gates = gates.reshape(-1)
    max_rows = TOP_K * t + nt * e * (CHUNK - 1)
    nb = -(-max_rows // rb) + e
    padded, loff, goff, classcnt, tstart, tchunks, block_e, n_used = _plan(
        cnt[:, :, 0], rb, nb, _size_classes(tt))
    tables = (padded, loff, goff, classcnt)
    xs = _dispatch(hlin, slots, tables, (tstart, tchunks), n_rows=nb * rb, n_experts=e, d=d,
                   tt=tt)
    ys = _experts(xs, block_e, n_used, w_in, b_in, w_out, b_out, layer=layer, rb=rb, nb=nb)
    return _combine(ys, slots, gates, tables, xres, mod, lng, lnb,
                    n_experts=e, seq=seq, tt=tt, alpha=alpha)


def _qkv_kernel(x_ref, mod_ref, qwt_ref, kw_ref, vwt_ref, qt_ref, k_ref, vt_ref):
    x = x_ref[...]
    sh1 = mod_ref[0:1, :]
    sc1 = mod_ref[1:2, :]
    h = (x * (1.0 + sc1) + sh1).astype(BF16)
    nt_dims = (((1,), (1,)), ((), ()))
    qt = lax.dot_general(qwt_ref[...], h, nt_dims, preferred_element_type=F32).astype(BF16)
    tq = qt_ref.shape[3]
    for j in range(qt_ref.shape[1]):
        qt_ref[0, j] = qt[:, j * tq:(j + 1) * tq]
    tk = vt_ref.shape[3]
    ngrp = tk // 8
    r = lax.broadcasted_iota(jnp.int32, (tk, tk), 0)
    s = lax.broadcasted_iota(jnp.int32, (tk, tk), 1)
    perm = jnp.where(s == (r & 7) * ngrp + (r >> 3), 1.0, 0.0).astype(BF16)
    xb = x.astype(BF16)
    xp = jnp.concatenate(
        [jnp.dot(perm, xb[j * tk:(j + 1) * tk, :], preferred_element_type=F32)
         for j in range(vt_ref.shape[1])], axis=0).astype(BF16)
    k_ref[...] = jnp.dot(xp, kw_ref[...], preferred_element_type=F32).astype(BF16)
    vt = lax.dot_general(vwt_ref[...], xp, nt_dims, preferred_element_type=F32).astype(BF16)
    for j in range(vt_ref.shape[1]):
        vt_ref[0, j] = vt[:, j * tk:(j + 1) * tk]


def _qkv(x2d, mod, qwt, kw, vwt, *, bsz, seq, tl, tk):
    t, d = x2d.shape
    nlt = seq // tl
    full = pl.BlockSpec((d, d), lambda b, l: (0, 0))
    return pl.pallas_call(
        _qkv_kernel,
        grid=(bsz, nlt),
        in_specs=[pl.BlockSpec((tl, d), lambda b, l: (b * nlt + l, 0)),
                  pl.BlockSpec((None, 6, d), lambda b, l: (b, 0, 0)),
                  full, full, full],
        out_specs=[pl.BlockSpec((1, tl // tk, d, tk), lambda b, l: (b, l, 0, 0)),
                   pl.BlockSpec((tl, d), lambda b, l: (b * nlt + l, 0)),
                   pl.BlockSpec((1, tl // tk, d, tk), lambda b, l: (b, l, 0, 0))],
        out_shape=[jax.ShapeDtypeStruct((bsz, seq // tk, d, tk), BF16),
                   jax.ShapeDtypeStruct((t, d), BF16),
                   jax.ShapeDtypeStruct((bsz, seq // tk, d, tk), BF16)],
        compiler_params=_cparams(2),
        name="qkv",
    )(x2d, mod, qwt, kw, vwt)


def _stick_block(z_ref, a_ref, carry, masked):
    tk, tq = z_ref.shape
    ngrp = tk // 8
    sub = lax.broadcasted_iota(jnp.int32, (8, tq), 0)
    lane = lax.broadcasted_iota(jnp.int32, (8, tq), 1)
    causal = lambda g: sub * ngrp + g < lane
    run = None
    for g in reversed(range(ngrp)):
        z = z_ref[g * 8:(g + 1) * 8, :]
        sp = jnp.maximum(z, 0.0) + LOG2E * jnp.log(1.0 + jnp.exp2(-jnp.abs(z)))
        if masked:
            sp = jnp.where(causal(g), sp, 0.0)
        run = sp if run is None else run + sp
        z_ref[g * 8:(g + 1) * 8, :] = z - run
    tot = run
    s = tot
    for k in (1, 2, 4):
        s = s + jnp.where(sub < 8 - k, pltpu.roll(s, 8 - k, axis=0), 0.0)
    later = s - tot + carry
    for g in range(0, ngrp, 2):
        pair = []
        for gg in (g, g + 1):
            a = jnp.exp2(z_ref[gg * 8:(gg + 1) * 8, :] - later)
            if masked:
                a = jnp.where(causal(gg), a, 0.0)
            pair.append(a)
        a_ref[g * 8:(g + 2) * 8, :] = jnp.concatenate(pair, axis=0).astype(BF16)
    return carry + jnp.broadcast_to(s[0:1, :], (8, tq))


def _attn_kernel(qt_ref, k_ref, vt_ref, o_ref, z0, z1, a0, a1, acc_buf, qm_buf, carry_buf,
                 *, head_dim):
    nq, _, tq = qt_ref.shape[1:]
    tk = vt_ref.shape[3]
    hp = qt_ref.shape[2] // head_dim
    row = lax.broadcasted_iota(jnp.int32, (hp * head_dim, tq), 0)

    def set_queries(qi):
        qf = qt_ref[0, qi].astype(F32)
        for h in range(hp):
            in_head = jnp.logical_and(row >= h * head_dim, row < (h + 1) * head_dim)
            qm_buf[h] = jnp.where(in_head, qf, 0.0).astype(BF16)

    def scores(kb, z_out):
        kblk = k_ref[pl.ds(pl.multiple_of(kb * tk, tk), tk), :]
        for h in range(hp):
            z_out[h] = jnp.dot(kblk, qm_buf[h], preferred_element_type=F32)

    def weighted_values(kb, a_in):
        return [jnp.dot(vt_ref[0, kb, h * head_dim:(h + 1) * head_dim, :], a_in[h],
                        preferred_element_type=F32) for h in range(hp)]

    def diagonal(z_in, a_out):
        zero = jnp.zeros((8, tq), F32)
        for h in range(hp):
            carry_buf[h] = _stick_block(z_in.at[h], a_out.at[h], zero, True)

    def step(kb, z_in, z_out, a_in, a_out, carries, issue_next=True):
        pv = weighted_values(kb + 1, a_in)
        if issue_next:
            scores(jnp.maximum(kb - 1, 0), z_out)
        carries = tuple(_stick_block(z_in.at[h], a_out.at[h], carries[h], False)
                        for h in range(hp))
        for h in range(hp):
            acc_buf[h] += pv[h]
        return carries

    def finish_and_open_next(qi, a_last):
        nxt = jnp.minimum(qi + 1, nq - 1)
        pv = weighted_values(0, a_last)
        set_queries(nxt)
        scores(nxt, z0)
        scores(jnp.maximum(nxt - 1, 0), z1)
        out = jnp.concatenate([(acc_buf[h] + pv[h]).T for h in range(hp)], axis=-1)
        o_ref[pl.ds(pl.multiple_of(qi * tq, tq), tq), :] = out.astype(o_ref.dtype)
        diagonal(z0, a0)
        acc_buf[...] = jnp.zeros(acc_buf.shape, F32)

    set_queries(0)
    scores(0, z0)
    diagonal(z0, a0)
    acc_buf[...] = jnp.zeros(acc_buf.shape, F32)

    def query_tile(qi, c):
        def pair(i, carries):
            kb = qi - 1 - 2 * i
            carries = step(kb, z1, z0, a0, a1, carries)
            return step(kb - 1, z0, z1, a1, a0, carries)

        carries = lax.fori_loop(0, qi // 2, pair, tuple(carry_buf[h] for h in range(hp)))

        @pl.when(qi % 2 == 1)
        def _():
            step(0, z1, z0, a0, a1, carries, issue_next=False)
            finish_and_open_next(qi, a1)

        @pl.when(qi % 2 == 0)
        def _():
            finish_and_open_next(qi, a0)

        return c

    lax.fori_loop(0, nq, query_tile, 0)


def _attention(qt, k2d, vt, *, bsz, seq):
    _, nq, d, tq = qt.shape
    head_dim = d // N_HEADS
    tk = vt.shape[3]
    assert tk == tq, "the diagonal mask assumes square blocks"
    ncol = d // LANES
    hp = LANES // head_dim
    return pl.pallas_call(
        functools.partial(_attn_kernel, head_dim=head_dim),
        grid=(bsz, ncol),
        in_specs=[pl.BlockSpec((1, nq, LANES, tq), lambda b, c: (b, 0, c, 0)),
                  pl.BlockSpec((seq, LANES), lambda b, c: (b, c)),
                  pl.BlockSpec((1, seq // tk, LANES, tk), lambda b, c: (b, 0, c, 0))],
        out_specs=pl.BlockSpec((seq, LANES), lambda b, c: (b, c)),
        out_shape=jax.ShapeDtypeStruct((bsz * seq, d), BF16),
        scratch_shapes=[pltpu.VMEM((hp, tk, tq), F32), pltpu.VMEM((hp, tk, tq), F32),
                        pltpu.VMEM((hp, tk, tq), BF16), pltpu.VMEM((hp, tk, tq), BF16),
                        pltpu.VMEM((hp, head_dim, tq), F32),
                        pltpu.VMEM((hp, hp * head_dim, tq), BF16),
                        pltpu.VMEM((hp, 8, tq), F32)],
        compiler_params=_cparams(2),
        name="attention",
    )(qt, k2d, vt)


def _oproj_kernel(o_ref, xres_ref, mod_ref, ow_ref, lng_ref, lnb_ref, rwt_ref, rb_ref,
                  x_out_ref, hlin_ref, logt_ref, *, alpha):
    y = jnp.dot(o_ref[...], ow_ref[...], preferred_element_type=F32)
    _post_norm_route(xres_ref[...], y, mod_ref, lng_ref, lnb_ref, rwt_ref, rb_ref,
                     x_out_ref, hlin_ref, logt_ref, alpha)


def _oproj(o2d, xres, mod, ow, lng, lnb, rwt, rb, *, seq, tl, alpha):
    t, d = xres.shape
    e = rwt.shape[0]
    p = d // LANES
    nlt = seq // tl
    full = lambda shape: pl.BlockSpec(shape, lambda i: (0,) * len(shape))
    return pl.pallas_call(
        functools.partial(_oproj_kernel, alpha=alpha),
        grid=(t // tl,),
        in_specs=[pl.BlockSpec((tl, d), lambda i: (i, 0)),
                  pl.BlockSpec((tl, d), lambda i: (i, 0)),
                  pl.BlockSpec((None, 6, d), lambda i: (i // nlt, 0, 0)),
                  full((d, d)), full((1, d)), full((1, d)), full((e, d)), full((e, 1))],
        out_specs=[pl.BlockSpec((tl, d), lambda i: (i, 0)),
                   pl.BlockSpec((tl * p, LANES), lambda i: (i, 0)),
                   pl.BlockSpec((e, tl), lambda i: (0, i))],
        out_shape=[jax.ShapeDtypeStruct((t, d), F32),
                   jax.ShapeDtypeStruct((t * p, LANES), F32),
                   jax.ShapeDtypeStruct((e, t), F32)],
        compiler_params=_cparams(1),
        name="oproj",
    )(o2d, xres, mod, ow, lng.reshape(1, d), lnb.reshape(1, d), rwt, rb.reshape(e, 1))


def kernel(x, c, ada_w, ada_b, ln_g, ln_b, cv_w1, cv_b1, cv_dw, cv_db, cv_ln_g, cv_ln_b,
           cv_w2, cv_b2, kv_w, q_w, o_w, router_w, router_b, moe_w_in, moe_b_in,
           moe_w_out, moe_b_out):
    bsz, seq, d = x.shape
    depth = ada_w.shape[0]
    n_a = cv_w1.shape[0]
    assert depth == 2 and n_a == 1 and q_w.shape[0] == 1, "one conv layer then one attention layer"
    alpha = (2.0 * depth) ** 0.25
    head_dim = d // N_HEADS
    tl = min(512, seq)
    tt = min(512, seq)
    rb = 512
    tq = min(256, seq)

    mod = _ada(c, ada_w, ada_b).reshape(depth, bsz, 6, d)
    x0 = x.reshape(bsz * seq, d)
    rwt = jnp.swapaxes(router_w, 1, 2)

    x1, hlin, logt = _conv_mixer(
        x0, mod[0], cv_w1[0].astype(BF16), cv_b1[0], cv_dw[0], cv_db[0], cv_ln_g[0], cv_ln_b[0],
        cv_w2[0].astype(BF16), cv_b2[0], ln_g[0, 0], ln_b[0, 0], rwt[0], router_b[0],
        bsz=bsz, seq=seq, tl=tl, alpha=alpha)
    x2 = _moe(hlin, logt, x1, mod[0], ln_g[0, 1], ln_b[0, 1], moe_w_in, moe_b_in,
              moe_w_out, moe_b_out, layer=0, seq=seq, tt=tt, rb=rb, alpha=alpha)

    scale = head_dim ** -0.5
    qwt = (q_w[0] * (scale * LOG2E)).T.astype(BF16)
    kw = kv_w[:, :d].astype(BF16)
    vwt = kv_w[:, d:].T.astype(BF16)
    qt, k2d, vt = _qkv(x2, mod[1], qwt, kw, vwt, bsz=bsz, seq=seq, tl=tl, tk=tq)
    o2d = _attention(qt, k2d, vt, bsz=bsz, seq=seq)
    x3, hlin, logt = _oproj(o2d, x2, mod[1], o_w[0].astype(BF16), ln_g[1, 0], ln_b[1, 0],
                            rwt[1], router_b[1], seq=seq, tl=tl, alpha=alpha)
    x4 = _moe(hlin, logt, x3, mod[1], ln_g[1, 1], ln_b[1, 1], moe_w_in, moe_b_in,
              moe_w_out, moe_b_out, layer=1, seq=seq, tt=tt, rb=rb, alpha=alpha)
    return x4.reshape(bsz, seq, d)
```

```python
import functools
import math

import jax
import jax.numpy as jnp
from jax import lax
from jax.experimental import pallas as pl
from jax.experimental.pallas import tpu as pltpu

F32 = jnp.float32
BF16 = jnp.bfloat16
HIGHEST = lax.Precision.HIGHEST

LANES = 128
N_HEADS = 16
TOP_K = 4
CONV_HALO = 32
SWIGLU_LIMIT = 7.0
SWIGLU_ALPHA = 1.702
LN_EPS = 1e-5
LOG2E = 1.4426950408889634
CHUNK = 8
VMEM_LIMIT = 56 * 1024 * 1024


def _cparams(n_axes):
    return pltpu.CompilerParams(
        dimension_semantics=("arbitrary",) * n_axes, vmem_limit_bytes=VMEM_LIMIT)


def _layer_norm(x, g, b):
    mu = jnp.mean(x, axis=-1, keepdims=True)
    xc = x - mu
    var = jnp.mean(xc * xc, axis=-1, keepdims=True)
    return xc * lax.rsqrt(var + LN_EPS) * g + b


def _silu(x):
    return x * jax.nn.sigmoid(x)


def _ada_kernel(c_ref, w_ref, b_ref, o_ref):
    o_ref[0] = jnp.dot(_silu(c_ref[...]), w_ref[0], precision=HIGHEST,
                       preferred_element_type=F32) + b_ref[0]


def _ada(c, ada_w, ada_b):
    depth, d, n = ada_w.shape
    bsz = c.shape[0]
    tn = d
    return pl.pallas_call(
        _ada_kernel,
        grid=(depth, n // tn),
        in_specs=[pl.BlockSpec((bsz, d), lambda l, j: (0, 0)),
                  pl.BlockSpec((1, d, tn), lambda l, j: (l, 0, j)),
                  pl.BlockSpec((1, 1, tn), lambda l, j: (l, 0, j))],
        out_specs=pl.BlockSpec((1, bsz, tn), lambda l, j: (l, 0, j)),
        out_shape=jax.ShapeDtypeStruct((depth, bsz, n), F32),
        compiler_params=_cparams(2),
        name="ada",
    )(c, ada_w, ada_b.reshape(depth, 1, n))


def _post_norm_route(xres, y, mod_ref, lng_ref, lnb_ref, rwt_ref, rb_ref,
                     x_out_ref, hlin_ref, logt_ref, alpha):
    tl, d = xres.shape
    g1 = mod_ref[2:3, :]
    sh2 = mod_ref[3:4, :]
    sc2 = mod_ref[4:5, :]
    x1 = _layer_norm(alpha * xres + (1.0 + g1) * y, lng_ref[...], lnb_ref[...])
    x_out_ref[...] = x1
    h2 = x1 * (1.0 + sc2) + sh2
    p = d // LANES
    for s in range(p):
        hlin_ref[pl.ds(s, tl, stride=p), :] = h2[:, s * LANES:(s + 1) * LANES]
    logt_ref[...] = lax.dot_general(
        rwt_ref[...], h2, (((1,), (1,)), ((), ())), precision=HIGHEST,
        preferred_element_type=F32) + rb_ref[...]


def _conv_kernel(x_ref, mod_ref, w1_ref, b1_ref, dw_ref, db_ref, clg_ref, clb_ref,
                 w2_ref, b2_ref, lng_ref, lnb_ref, rwt_ref, rb_ref,
                 x_out_ref, hlin_ref, logt_ref,
                 ubuf, sbuf, dwb, cbuf, *, alpha, width):
    tl, d = x_ref.shape
    first_tile = pl.program_id(1) == 0

    @pl.when(jnp.logical_and(pl.program_id(0) == 0, first_tile))
    def _():
        for j in range(width):
            dwb[j] = jnp.broadcast_to(dw_ref[j:j + 1, :], (8, d))

    x = x_ref[...]
    sh1 = mod_ref[0:1, :]
    sc1 = mod_ref[1:2, :]
    h = (x * (1.0 + sc1) + sh1).astype(BF16)
    u = jnp.dot(h, w1_ref[...], preferred_element_type=F32) + b1_ref[...]
    glu = u[:, :d] * jax.nn.sigmoid(u[:, d:])

    @pl.when(first_tile)
    def _():
        ubuf[0:CONV_HALO, :] = jnp.zeros((CONV_HALO, d), F32)

    ubuf[CONV_HALO:CONV_HALO + tl, :] = glu
    for r in range(1, 8):
        sbuf[r - 1] = ubuf[pl.ds(r, tl + CONV_HALO - 8), :]

    off0 = CONV_HALO - (width - 1)
    cw = 512
    rows = 32

    def body(i, carry):
        base = pl.multiple_of(i * rows, rows)
        for c in range(d // cw):
            cs = slice(c * cw, (c + 1) * cw)
            accs = [jnp.broadcast_to(db_ref[:, cs], (8, cw)) for _ in range(rows // 8)]
            for j in range(width):
                q, r = divmod(off0 + j, 8)
                w = dwb[j, :, cs]
                for a in range(rows // 8):
                    start = base + 8 * (q + a)
                    if r == 0:
                        v = ubuf[pl.ds(start, 8), cs]
                    else:
                        v = sbuf[r - 1, pl.ds(start, 8), cs]
                    accs[a] = accs[a] + w * v
            for a in range(rows // 8):
                cbuf[pl.ds(base + 8 * a, 8), cs] = accs[a]
        return carry

    lax.fori_loop(0, tl // rows, body, 0)
    ubuf[0:CONV_HALO, :] = ubuf[tl:tl + CONV_HALO, :]

    v = _silu(_layer_norm(cbuf[...], clg_ref[...], clb_ref[...]))
    y = jnp.dot(v.astype(BF16), w2_ref[...], preferred_element_type=F32) + b2_ref[...]
    _post_norm_route(x, y, mod_ref, lng_ref, lnb_ref, rwt_ref, rb_ref,
                     x_out_ref, hlin_ref, logt_ref, alpha)


def _conv_mixer(x2d, mod, w1, b1, dw, db, clg, clb, w2, b2, lng, lnb, rwt, rb,
                *, bsz, seq, tl, alpha):
    t, d = x2d.shape
    e = rwt.shape[0]
    width = dw.shape[0]
    p = d // LANES
    nlt = seq // tl
    row = lambda a: a.reshape(1, -1)
    full = lambda shape: pl.BlockSpec(shape, lambda b, l: (0,) * len(shape))
    return pl.pallas_call(
        functools.partial(_conv_kernel, alpha=alpha, width=width),
        grid=(bsz, nlt),
        in_specs=[pl.BlockSpec((tl, d), lambda b, l: (b * nlt + l, 0)),
                  pl.BlockSpec((None, 6, d), lambda b, l: (b, 0, 0)),
                  full((d, 2 * d)), full((1, 2 * d)), full((width, d)), full((1, d)),
                  full((1, d)), full((1, d)), full((d, d)), full((1, d)),
                  full((1, d)), full((1, d)), full((e, d)), full((e, 1))],
        out_specs=[pl.BlockSpec((tl, d), lambda b, l: (b * nlt + l, 0)),
                   pl.BlockSpec((tl * p, LANES), lambda b, l: (b * nlt + l, 0)),
                   pl.BlockSpec((e, tl), lambda b, l: (0, b * nlt + l))],
        out_shape=[jax.ShapeDtypeStruct((t, d), F32),
                   jax.ShapeDtypeStruct((t * p, LANES), F32),
                   jax.ShapeDtypeStruct((e, t), F32)],
        scratch_shapes=[pltpu.VMEM((tl + CONV_HALO, d), F32),
                        pltpu.VMEM((7, tl + CONV_HALO - 8, d), F32),
                        pltpu.VMEM((width, 8, d), F32),
                        pltpu.VMEM((tl, d), F32)],
        compiler_params=_cparams(2),
        name="conv_mixer",
    )(x2d, mod, w1, row(b1), dw, row(db), row(clg), row(clb), w2, row(b2),
      row(lng), row(lnb), rwt, rb.reshape(e, 1))


def _route_kernel(logt_ref, slot_ref, gate_ref, cnt_ref, *, p, stage_rows):
    e, tt = logt_ref.shape
    l = logt_ref[...]
    iota_e = lax.broadcasted_iota(jnp.int32, (e, tt), 0).astype(F32)
    sels, vals = [], []
    for _ in range(TOP_K):
        m = jnp.max(l, axis=0, keepdims=True)
        idx = jnp.min(jnp.where(l == m, iota_e, float(e)), axis=0, keepdims=True)
        sels.append(idx)
        vals.append(m)
        l = jnp.where(iota_e == idx, -jnp.inf, l)
    exps = [jnp.exp(v - vals[0]) for v in vals]
    denom = exps[0] + exps[1] + exps[2] + exps[3]
    onehots = [iota_e == s for s in sels]
    member = jnp.zeros((e, tt), F32)
    for oh in onehots:
        member = member + jnp.where(oh, 1.0, 0.0)
    tri = jnp.where(lax.broadcasted_iota(jnp.int32, (tt, tt), 0)
                    < lax.broadcasted_iota(jnp.int32, (tt, tt), 1), 1.0, 0.0).astype(BF16)
    rank = jnp.dot(member.astype(BF16), tri, preferred_element_type=F32)
    cnt = jnp.sum(member, axis=1, keepdims=True)
    padded = jnp.floor((cnt + (CHUNK - 1)) * (1.0 / CHUNK)) * CHUNK
    low = jnp.where(lax.broadcasted_iota(jnp.int32, (e, e), 1)
                    < lax.broadcasted_iota(jnp.int32, (e, e), 0), 1.0, 0.0)
    loff = jnp.dot(low, jnp.broadcast_to(padded, (e, tt)), precision=HIGHEST,
                   preferred_element_type=F32)
    pos = loff + rank + (pl.program_id(0) % 2 * stage_rows).astype(F32)
    for k in range(TOP_K):
        slot = jnp.sum(jnp.where(onehots[k], pos, 0.0), axis=0, keepdims=True)
        slot_ref[0, k:k + 1, :] = (slot * p).astype(jnp.int32)
        gate_ref[0, k:k + 1, :] = exps[k] / denom
    cnt_ref[0] = jnp.broadcast_to(cnt, (e, LANES)).astype(jnp.int32)


def _route(logt, tt, p, stage_rows):
    e, t = logt.shape
    nt = t // tt
    return pl.pallas_call(
        functools.partial(_route_kernel, p=p, stage_rows=stage_rows),
        grid=(nt,),
        in_specs=[pl.BlockSpec((e, tt), lambda i: (0, i))],
        out_specs=[pl.BlockSpec((1, TOP_K, tt), lambda i: (i, 0, 0)),
                   pl.BlockSpec((1, TOP_K, tt), lambda i: (i, 0, 0)),
                   pl.BlockSpec((1, e, LANES), lambda i: (i, 0, 0))],
        out_shape=[jax.ShapeDtypeStruct((nt, TOP_K, tt), jnp.int32),
                   jax.ShapeDtypeStruct((nt, TOP_K, tt), F32),
                   jax.ShapeDtypeStruct((nt, e, LANES), jnp.int32)],
        compiler_params=_cparams(1),
        name="route",
    )(logt)


def _plan(cnt, rb, nb, n_classes):
    nt, e = cnt.shape
    padded = ((cnt + CHUNK - 1) // CHUNK) * CHUNK
    loff = jnp.cumsum(padded, axis=1) - padded
    rows_e = jnp.sum(padded, axis=0)
    blocks_e = (rows_e + rb - 1) // rb
    bend = jnp.cumsum(blocks_e)
    pstart = (bend - blocks_e) * rb
    goff = pstart[None, :] + jnp.cumsum(padded, axis=0) - padded
    block_ids = jnp.arange(nb, dtype=bend.dtype)
    block_e = jnp.minimum(jnp.sum(bend[None, :] <= block_ids[:, None], axis=1), e - 1)
    chunks = padded // CHUNK
    classcnt = jnp.stack([jnp.sum((chunks >> k) & 1, axis=1) for k in range(n_classes)], axis=1)
    tail_start = pstart + rows_e
    tail_chunks = (blocks_e * rb - rows_e) // CHUNK
    i32 = lambda a: a.astype(jnp.int32)
    return (i32(padded.reshape(-1)), i32(loff.reshape(-1)), i32(goff.reshape(-1)),
            i32(classcnt.reshape(-1)), i32(tail_start), i32(tail_chunks), i32(block_e), i32(bend[-1:]))


def _size_classes(tt):
    return (tt // CHUNK).bit_length()


def _start_group_dmas(padded_ref, loff_ref, goff_ref, tile, n_experts, p, n_classes, make):
    def per_expert(ex, carry):
        idx = tile * n_experts + ex
        n = padded_ref[idx] // CHUNK
        lo = loff_ref[idx]
        go = goff_ref[idx]
        def pieces(classes):
            for k in classes:
                @pl.when((n >> k) & 1 == 1)
                def _():
                    done = (n >> (k + 1)) << (k + 1)
                    lrow = pl.multiple_of((lo + done * CHUNK) * p, CHUNK * p)
                    grow = pl.multiple_of((go + done * CHUNK) * p, CHUNK * p)
                    make(lrow, grow, (CHUNK << k) * p).start()

        split = min(4, n_classes)
        pieces(range(split))

        @pl.when(n >= (1 << split))
        def _():
            pieces(range(split, n_classes))

        return carry

    lax.fori_loop(0, n_experts, per_expert, 0)


def _wait_group_dmas(classcnt_ref, tile, p, n_classes, make):
    for k in range(n_classes):
        def wait_one(j, c, k=k):
            make(0, 0, (CHUNK << k) * p).wait()
            return c

        lax.fori_loop(0, classcnt_ref[tile * n_classes + k], wait_one, 0)


def _dispatch_kernel(padded_ref, loff_ref, goff_ref, classcnt_ref, tstart_ref, tchunks_ref,
                     x_ref, slot_ref, xs_ref, stage, sem, *, n_experts):
    tile = pl.program_id(0)
    n_tiles = pl.num_programs(0)
    tt = slot_ref.shape[0] // TOP_K
    p = x_ref.shape[0] // tt
    buf_rows = stage.shape[0] // 2
    n_classes = _size_classes(tt)
    buf = tile % 2

    def make_for(which_buf):
        base = pl.multiple_of(which_buf * buf_rows, CHUNK * p)
        return lambda lrow, grow, nrows: pltpu.make_async_copy(
            stage.at[pl.ds(base + lrow, nrows), :], xs_ref.at[pl.ds(grow, nrows), :],
            sem.at[which_buf])

    @pl.when(tile < 2)
    def _():
        start = pl.multiple_of(buf * buf_rows, CHUNK * p)
        stage[pl.ds(start, buf_rows), :] = jnp.zeros((buf_rows, LANES), F32)

    @pl.when(tile >= 2)
    def _():
        _wait_group_dmas(classcnt_ref, tile - 2, p, n_classes, make_for(buf))

    @pl.when(tile == 0)
    def _():
        rows = CHUNK * p
        tail = lambda grow: pltpu.make_async_copy(
            stage.at[pl.ds(0, rows), :], xs_ref.at[pl.ds(grow, rows), :], sem.at[0])

        def per_expert(ex, total):
            n = tchunks_ref[ex]

            def per_chunk(j, c):
                tail(pl.multiple_of((tstart_ref[ex] + j * CHUNK) * p, rows)).start()
                return c

            lax.fori_loop(0, n, per_chunk, 0)
            return total + n

        total = lax.fori_loop(0, n_experts, per_expert, 0)

        def wait_tail(j, c):
            tail(0).wait()
            return c

        lax.fori_loop(0, total, wait_tail, 0)

    unroll = 8

    def per_tokens(g, carry):
        for u in range(unroll):
            t = g * unroll + u
            v = x_ref[pl.ds(pl.multiple_of(t * p, p), p), :]
            for k in range(TOP_K):
                s = slot_ref[k * tt + t]
                stage[pl.ds(pl.multiple_of(s, p), p), :] = v
        return carry

    lax.fori_loop(0, tt // unroll, per_tokens, 0)
    _start_group_dmas(padded_ref, loff_ref, goff_ref, tile, n_experts, p, n_classes,
                      make_for(buf))

    @pl.when(tile == n_tiles - 1)
    def _():
        @pl.when(tile >= 1)
        def _():
            _wait_group_dmas(classcnt_ref, tile - 1, p, n_classes, make_for(1 - buf))

        _wait_group_dmas(classcnt_ref, tile, p, n_classes, make_for(buf))


def _stage_rows(tt, n_experts):
    rows = TOP_K * tt + n_experts * (CHUNK - 1)
    return -(-rows // CHUNK) * CHUNK


def _dispatch(hlin, slots, tables, tails, *, n_rows, n_experts, d, tt):
    padded, loff, goff, classcnt = tables
    tstart, tchunks = tails
    nt = slots.shape[0] // (TOP_K * tt)
    p = d // LANES
    return pl.pallas_call(
        functools.partial(_dispatch_kernel, n_experts=n_experts),
        grid_spec=pltpu.PrefetchScalarGridSpec(
            num_scalar_prefetch=6,
            grid=(nt,),
            in_specs=[pl.BlockSpec((tt * p, LANES), lambda i, *_: (i, 0)),
                      pl.BlockSpec((TOP_K * tt,), lambda i, *_: (i,),
                                   memory_space=pltpu.SMEM)],
            out_specs=pl.BlockSpec(memory_space=pl.ANY),
            scratch_shapes=[pltpu.VMEM((2 * _stage_rows(tt, n_experts) * p, LANES), F32),
                            pltpu.SemaphoreType.DMA((2,))]),
        out_shape=jax.ShapeDtypeStruct((n_rows * p, LANES), F32),
        compiler_params=_cparams(1),
        name="dispatch",
    )(padded, loff, goff, classcnt, tstart, tchunks, hlin, slots)


def _expert_kernel(be_ref, nu_ref, x_ref, win_ref, bin_ref, wout_ref, bout_ref, y_ref,
                   win_bf, wout_bf):
    b = pl.program_id(0)
    rb = x_ref.shape[0] * LANES // win_ref.shape[1]
    d = win_ref.shape[1]
    f = wout_ref.shape[1]
    p = d // LANES

    @pl.when(b < nu_ref[0])
    def _():
        prev = be_ref[jnp.maximum(b - 1, 0)]

        @pl.when(jnp.logical_or(b == 0, be_ref[b] != prev))
        def _():
            win_bf[...] = win_ref[0].astype(BF16)
            wout_bf[...] = wout_ref[0].astype(BF16)

        x = jnp.concatenate([x_ref[pl.ds(s, rb, stride=p), :] for s in range(p)], axis=-1)
        u = jnp.dot(x.astype(BF16), win_bf[...], preferred_element_type=F32) + bin_ref[0]
        x_glu = jnp.minimum(u[:, :f], SWIGLU_LIMIT)
        x_lin = jnp.clip(u[:, f:], -SWIGLU_LIMIT, SWIGLU_LIMIT)
        act = (x_lin + 1.0) * (x_glu * jax.nn.sigmoid(SWIGLU_ALPHA * x_glu))
        y = jnp.dot(act.astype(BF16), wout_bf[...], preferred_element_type=F32) + bout_ref[0]
        for s in range(p):
            y_ref[pl.ds(s, rb, stride=p), :] = y[:, s * LANES:(s + 1) * LANES]


def _experts(xs, block_e, n_used, w_in, b_in, w_out, b_out, *, layer, rb, nb):
    _, e, d, f2 = w_in.shape
    f = f2 // 2
    p = d // LANES
    last = lambda b, nu: jnp.maximum(jnp.minimum(b, nu[0] - 1), 0)
    blk = lambda b, be, nu: (last(b, nu), 0)
    wsel = lambda b, be, nu: (layer, be[last(b, nu)], 0, 0)
    return pl.pallas_call(
        _expert_kernel,
        grid_spec=pltpu.PrefetchScalarGridSpec(
            num_scalar_prefetch=2,
            grid=(nb,),
            in_specs=[pl.BlockSpec((rb * p, LANES), blk),
                      pl.BlockSpec((None, 1, d, f2), wsel),
                      pl.BlockSpec((None, 1, 1, f2), wsel),
                      pl.BlockSpec((None, 1, f, d), wsel),
                      pl.BlockSpec((None, 1, 1, d), wsel)],
            out_specs=pl.BlockSpec((rb * p, LANES), blk),
            scratch_shapes=[pltpu.VMEM((d, f2), BF16), pltpu.VMEM((f, d), BF16)]),
        out_shape=jax.ShapeDtypeStruct(xs.shape, F32),
        compiler_params=_cparams(1),
        name="experts",
    )(block_e, n_used, xs, w_in, b_in.reshape(-1, e, 1, f2), w_out, b_out.reshape(-1, e, 1, d))


def _combine_kernel(padded_ref, loff_ref, goff_ref, classcnt_ref, ys_ref, slot_ref, gate_ref,
                    xres_ref, mod_ref, lng_ref, lnb_ref, x_out_ref, stage, comb, sem,
                    *, n_experts, alpha):
    tile = pl.program_id(0)
    n_tiles = pl.num_programs(0)
    tt, d = xres_ref.shape
    p = d // LANES
    buf_rows = stage.shape[0] // 2
    n_classes = _size_classes(tt)
    buf = tile % 2

    def make_for(which_buf):
        base = pl.multiple_of(which_buf * buf_rows, CHUNK * p)
        return lambda lrow, grow, nrows: pltpu.make_async_copy(
            ys_ref.at[pl.ds(grow, nrows), :], stage.at[pl.ds(base + lrow, nrows), :],
            sem.at[which_buf])

    def start_tile(which_tile, which_buf):
        _start_group_dmas(padded_ref, loff_ref, goff_ref, which_tile, n_experts, p, n_classes,
                          make_for(which_buf))

    @pl.when(tile == 0)
    def _():
        start_tile(tile, buf)

    @pl.when(tile + 1 < n_tiles)
    def _():
        start_tile(tile + 1, 1 - buf)

    _wait_group_dmas(classcnt_ref, tile, p, n_classes, make_for(buf))
    unroll = 4

    def per_tokens(g, carry):
        for u in range(unroll):
            t = g * unroll + u
            acc = None
            for k in range(TOP_K):
                s = slot_ref[k * tt + t]
                term = gate_ref[k * tt + t] * stage[pl.ds(pl.multiple_of(s, p), p), :]
                acc = term if acc is None else acc + term
            comb[pl.ds(pl.multiple_of(t * p, p), p), :] = acc
        return carry

    lax.fori_loop(0, tt // unroll, per_tokens, 0)
    moe = jnp.concatenate([comb[pl.ds(s, tt, stride=p), :] for s in range(p)], axis=-1)
    g2 = mod_ref[5:6, :]
    x_out_ref[...] = _layer_norm(alpha * xres_ref[...] + (1.0 + g2) * moe,
                                 lng_ref[...], lnb_ref[...])


def _combine(ys, slots, gates, tables, xres, mod, lng, lnb, *, n_experts, seq, tt, alpha):
    padded, loff, goff, classcnt = tables
    t, d = xres.shape
    nt = t // tt
    p = d // LANES
    tiles_per_seq = seq // tt
    smem = lambda: pl.BlockSpec((TOP_K * tt,), lambda i, *_: (i,), memory_space=pltpu.SMEM)
    return pl.pallas_call(
        functools.partial(_combine_kernel, n_experts=n_experts, alpha=alpha),
        grid_spec=pltpu.PrefetchScalarGridSpec(
            num_scalar_prefetch=4,
            grid=(nt,),
            in_specs=[pl.BlockSpec(memory_space=pl.ANY), smem(), smem(),
                      pl.BlockSpec((tt, d), lambda i, *_: (i, 0)),
                      pl.BlockSpec((None, 6, d), lambda i, *_: (i // tiles_per_seq, 0, 0)),
                      pl.BlockSpec((1, d), lambda i, *_: (0, 0)),
                      pl.BlockSpec((1, d), lambda i, *_: (0, 0))],
            out_specs=pl.BlockSpec((tt, d), lambda i, *_: (i, 0)),
            scratch_shapes=[pltpu.VMEM((2 * _stage_rows(tt, n_experts) * p, LANES), F32),
                            pltpu.VMEM((tt * p, LANES), F32),
                            pltpu.SemaphoreType.DMA((2,))]),
        out_shape=jax.ShapeDtypeStruct((t, d), F32),
        compiler_params=_cparams(1),
        name="combine",
    )(padded, loff, goff, classcnt, ys, slots, gates, xres, mod, lng.reshape(1, d), lnb.reshape(1, d))


def _moe(hlin, logt, xres, mod, lng, lnb, w_in, b_in, w_out, b_out, *, layer, seq, tt, rb,
         alpha):
    e, t = logt.shape
    d = xres.shape[1]
    nt = t // tt
    slots, gates, cnt = _route(logt, tt, d // LANES, _stage_rows(tt, e))
    slots = slots.reshape(-1)
    gates = gates.reshape(-1)
    max_rows = TOP_K * t + nt * e * (CHUNK - 1)
    nb = -(-max_rows // rb) + e
    padded, loff, goff, classcnt, tstart, tchunks, block_e, n_used = _plan(
        cnt[:, :, 0], rb, nb, _size_classes(tt))
    tables = (padded, loff, goff, classcnt)
    xs = _dispatch(hlin, slots, tables, (tstart, tchunks), n_rows=nb * rb, n_experts=e, d=d,
                   tt=tt)
    ys = _experts(xs, block_e, n_used, w_in, b_in, w_out, b_out, layer=layer, rb=rb, nb=nb)
    return _combine(ys, slots, gates, tables, xres, mod, lng, lnb,
                    n_experts=e, seq=seq, tt=tt, alpha=alpha)


def _qkv_kernel(x_ref, mod_ref, qwt_ref, kw_ref, vwt_ref, qt_ref, k_ref, vt_ref):
    x = x_ref[...]
    sh1 = mod_ref[0:1, :]
    sc1 = mod_ref[1:2, :]
    h = (x * (1.0 + sc1) + sh1).astype(BF16)
    nt_dims = (((1,), (1,)), ((), ()))
    qt = lax.dot_general(qwt_ref[...], h, nt_dims, preferred_element_type=F32).astype(BF16)
    tq = qt_ref.shape[3]
    for j in range(qt_ref.shape[1]):
        qt_ref[0, j] = qt[:, j * tq:(j + 1) * tq]
    tk = vt_ref.shape[3]
    ngrp = tk // 8
    r = lax.broadcasted_iota(jnp.int32, (tk, tk), 0)
    s = lax.broadcasted_iota(jnp.int32, (tk, tk), 1)
    perm = jnp.where(s == (r & 7) * ngrp + (r >> 3), 1.0, 0.0).astype(BF16)
    xb = x.astype(BF16)
    xp = jnp.concatenate(
        [jnp.dot(perm, xb[j * tk:(j + 1) * tk, :], preferred_element_type=F32)
         for j in range(vt_ref.shape[1])], axis=0).astype(BF16)
    k_ref[...] = jnp.dot(xp, kw_ref[...], preferred_element_type=F32).astype(BF16)
    vt = lax.dot_general(vwt_ref[...], xp, nt_dims, preferred_element_type=F32).astype(BF16)
    for j in range(vt_ref.shape[1]):
        vt_ref[0, j] = vt[:, j * tk:(j + 1) * tk]


def _qkv(x2d, mod, qwt, kw, vwt, *, bsz, seq, tl, tk):
    t, d = x2d.shape
    nlt = seq // tl
    full = pl.BlockSpec((d, d), lambda b, l: (0, 0))
    return pl.pallas_call(
        _qkv_kernel,
        grid=(bsz, nlt),
        in_specs=[pl.BlockSpec((tl, d), lambda b, l: (b * nlt + l, 0)),
                  pl.BlockSpec((None, 6, d), lambda b, l: (b, 0, 0)),
                  full, full, full],
        out_specs=[pl.BlockSpec((1, tl // tk, d, tk), lambda b, l: (b, l, 0, 0)),
                   pl.BlockSpec((tl, d), lambda b, l: (b * nlt + l, 0)),
                   pl.BlockSpec((1, tl // tk, d, tk), lambda b, l: (b, l, 0, 0))],
        out_shape=[jax.ShapeDtypeStruct((bsz, seq // tk, d, tk), BF16),
                   jax.ShapeDtypeStruct((t, d), BF16),
                   jax.ShapeDtypeStruct((bsz, seq // tk, d, tk), BF16)],
        compiler_params=_cparams(2),
        name="qkv",
    )(x2d, mod, qwt, kw, vwt)


def _stick_block(z_ref, a_ref, carry, masked):
    tk, tq = z_ref.shape
    ngrp = tk // 8
    sub = lax.broadcasted_iota(jnp.int32, (8, tq), 0)
    lane = lax.broadcasted_iota(jnp.int32, (8, tq), 1)
    causal = lambda g: sub * ngrp + g < lane
    run = None
    for g in reversed(range(ngrp)):
        z = z_ref[g * 8:(g + 1) * 8, :]
        rest = 1.0 / (1.0 + jnp.exp2(z))
        beta = 1.0 - rest
        if masked:
            rest = jnp.where(causal(g), rest, 1.0)
            beta = jnp.where(causal(g), beta, 0.0)
        z_ref[g * 8:(g + 1) * 8, :] = beta if run is None else beta * run
        run = rest if run is None else run * rest
    s = run
    for k in (1, 2, 4):
        s = s * jnp.where(sub < 8 - k, pltpu.roll(s, 8 - k, axis=0), 1.0)
    later = jnp.where(sub < 7, pltpu.roll(s, 7, axis=0), 1.0) * carry
    for g in range(0, ngrp, 2):
        a_ref[g * 8:(g + 2) * 8, :] = jnp.concatenate(
            [z_ref[gg * 8:(gg + 1) * 8, :] * later for gg in (g, g + 1)], axis=0).astype(BF16)
    return carry * jnp.broadcast_to(s[0:1, :], (8, tq))


def _attn_kernel(qt_ref, k_ref, vt_ref, o_ref, z0, z1, a0, a1, acc_buf, qm_buf, carry_buf,
                 *, head_dim):
    nq, _, tq = qt_ref.shape[1:]
    tk = vt_ref.shape[3]
    hp = qt_ref.shape[2] // head_dim
    row = lax.broadcasted_iota(jnp.int32, (hp * head_dim, tq), 0)

    def set_queries(qi):
        qf = qt_ref[0, qi].astype(F32)
        for h in range(hp):
            in_head = jnp.logical_and(row >= h * head_dim, row < (h + 1) * head_dim)
            qm_buf[h] = jnp.where(in_head, qf, 0.0).astype(BF16)

    def scores(kb, z_out):
        kblk = k_ref[pl.ds(pl.multiple_of(kb * tk, tk), tk), :]
        for h in range(hp):
            z_out[h] = jnp.dot(kblk, qm_buf[h], preferred_element_type=F32)

    def weighted_values(kb, a_in):
        return [jnp.dot(vt_ref[0, kb, h * head_dim:(h + 1) * head_dim, :], a_in[h],
                        preferred_element_type=F32) for h in range(hp)]

    def diagonal(z_in, a_out):
        one = jnp.ones((8, tq), F32)
        for h in range(hp):
            carry_buf[h] = _stick_block(z_in.at[h], a_out.at[h], one, True)

    def step(kb, z_in, z_out, a_in, a_out, carries, issue_next=True):
        pv = weighted_values(kb + 1, a_in)
        if issue_next:
            scores(jnp.maximum(kb - 1, 0), z_out)
        carries = tuple(_stick_block(z_in.at[h], a_out.at[h], carries[h], False)
                        for h in range(hp))
        for h in range(hp):
            acc_buf[h] += pv[h]
        return carries

    def finish_and_open_next(qi, a_last):
        nxt = jnp.minimum(qi + 1, nq - 1)
        pv = weighted_values(0, a_last)
        set_queries(nxt)
        scores(nxt, z0)
        scores(jnp.maximum(nxt - 1, 0), z1)
        out = jnp.concatenate([(acc_buf[h] + pv[h]).T for h in range(hp)], axis=-1)
        o_ref[pl.ds(pl.multiple_of(qi * tq, tq), tq), :] = out.astype(o_ref.dtype)
        diagonal(z0, a0)
        acc_buf[...] = jnp.zeros(acc_buf.shape, F32)

    set_queries(0)
    scores(0, z0)
    diagonal(z0, a0)
    acc_buf[...] = jnp.zeros(acc_buf.shape, F32)

    def query_tile(qi, c):
        def pair(i, carries):
            kb = qi - 1 - 2 * i
            carries = step(kb, z1, z0, a0, a1, carries)
            return step(kb - 1, z0, z1, a1, a0, carries)

        carries = lax.fori_loop(0, qi // 2, pair, tuple(carry_buf[h] for h in range(hp)))

        @pl.when(qi % 2 == 1)
        def _():
            step(0, z1, z0, a0, a1, carries, issue_next=False)
            finish_and_open_next(qi, a1)

        @pl.when(qi % 2 == 0)
        def _():
            finish_and_open_next(qi, a0)

        return c

    lax.fori_loop(0, nq, query_tile, 0)


def _attention(qt, k2d, vt, *, bsz, seq):
    _, nq, d, tq = qt.shape
    head_dim = d // N_HEADS
    tk = vt.shape[3]
    assert tk == tq, "the diagonal mask assumes square blocks"
    ncol = d // LANES
    hp = LANES // head_dim
    return pl.pallas_call(
        functools.partial(_attn_kernel, head_dim=head_dim),
        grid=(bsz, ncol),
        in_specs=[pl.BlockSpec((1, nq, LANES, tq), lambda b, c: (b, 0, c, 0)),
                  pl.BlockSpec((seq, LANES), lambda b, c: (b, c)),
                  pl.BlockSpec((1, seq // tk, LANES, tk), lambda b, c: (b, 0, c, 0))],
        out_specs=pl.BlockSpec((seq, LANES), lambda b, c: (b, c)),
        out_shape=jax.ShapeDtypeStruct((bsz * seq, d), BF16),
        scratch_shapes=[pltpu.VMEM((hp, tk, tq), F32), pltpu.VMEM((hp, tk, tq), F32),
                        pltpu.VMEM((hp, tk, tq), BF16), pltpu.VMEM((hp, tk, tq), BF16),
                        pltpu.VMEM((hp, head_dim, tq), F32),
                        pltpu.VMEM((hp, hp * head_dim, tq), BF16),
                        pltpu.VMEM((hp, 8, tq), F32)],
        compiler_params=_cparams(2),
        name="attention",
    )(qt, k2d, vt)


def _oproj_kernel(o_ref, xres_ref, mod_ref, ow_ref, lng_ref, lnb_ref, rwt_ref, rb_ref,
                  x_out_ref, hlin_ref, logt_ref, *, alpha):
    y = jnp.dot(o_ref[...], ow_ref[...], preferred_element_type=F32)
    _post_norm_route(xres_ref[...], y, mod_ref, lng_ref, lnb_ref, rwt_ref, rb_ref,
                     x_out_ref, hlin_ref, logt_ref, alpha)


def _oproj(o2d, xres, mod, ow, lng, lnb, rwt, rb, *, seq, tl, alpha):
    t, d = xres.shape
    e = rwt.shape[0]
    p = d // LANES
    nlt = seq // tl
    full = lambda shape: pl.BlockSpec(shape, lambda i: (0,) * len(shape))
    return pl.pallas_call(
        functools.partial(_oproj_kernel, alpha=alpha),
        grid=(t // tl,),
        in_specs=[pl.BlockSpec((tl, d), lambda i: (i, 0)),
                  pl.BlockSpec((tl, d), lambda i: (i, 0)),
                  pl.BlockSpec((None, 6, d), lambda i: (i // nlt, 0, 0)),
                  full((d, d)), full((1, d)), full((1, d)), full((e, d)), full((e, 1))],
        out_specs=[pl.BlockSpec((tl, d), lambda i: (i, 0)),
                   pl.BlockSpec((tl * p, LANES), lambda i: (i, 0)),
                   pl.BlockSpec((e, tl), lambda i: (0, i))],
        out_shape=[jax.ShapeDtypeStruct((t, d), F32),
                   jax.ShapeDtypeStruct((t * p, LANES), F32),
                   jax.ShapeDtypeStruct((e, t), F32)],
        compiler_params=_cparams(1),
        name="oproj",
    )(o2d, xres, mod, ow, lng.reshape(1, d), lnb.reshape(1, d), rwt, rb.reshape(e, 1))


def kernel(x, c, ada_w, ada_b, ln_g, ln_b, cv_w1, cv_b1, cv_dw, cv_db, cv_ln_g, cv_ln_b,
           cv_w2, cv_b2, kv_w, q_w, o_w, router_w, router_b, moe_w_in, moe_b_in,
           moe_w_out, moe_b_out):
    bsz, seq, d = x.shape
    depth = ada_w.shape[0]
    n_a = cv_w1.shape[0]
    assert depth == 2 and n_a == 1 and q_w.shape[0] == 1, "one conv layer then one attention layer"
    alpha = (2.0 * depth) ** 0.25
    head_dim = d // N_HEADS
    tl = min(512, seq)
    tt = min(512, seq)
    rb = 512
    tq = min(256, seq)

    mod = _ada(c, ada_w, ada_b).reshape(depth, bsz, 6, d)
    x0 = x.reshape(bsz * seq, d)
    rwt = jnp.swapaxes(router_w, 1, 2)

    x1, hlin, logt = _conv_mixer(
        x0, mod[0], cv_w1[0].astype(BF16), cv_b1[0], cv_dw[0], cv_db[0], cv_ln_g[0], cv_ln_b[0],
        cv_w2[0].astype(BF16), cv_b2[0], ln_g[0, 0], ln_b[0, 0], rwt[0], router_b[0],
        bsz=bsz, seq=seq, tl=tl, alpha=alpha)
    x2 = _moe(hlin, logt, x1, mod[0], ln_g[0, 1], ln_b[0, 1], moe_w_in, moe_b_in,
              moe_w_out, moe_b_out, layer=0, seq=seq, tt=tt, rb=rb, alpha=alpha)

    scale = head_dim ** -0.5
    qwt = (q_w[0] * (scale * LOG2E)).T.astype(BF16)
    kw = kv_w[:, :d].astype(BF16)
    vwt = kv_w[:, d:].T.astype(BF16)
    qt, k2d, vt = _qkv(x2, mod[1], qwt, kw, vwt, bsz=bsz, seq=seq, tl=tl, tk=tq)
    o2d = _attention(qt, k2d, vt, bsz=bsz, seq=seq)
    x3, hlin, logt = _oproj(o2d, x2, mod[1], o_w[0].astype(BF16), ln_g[1, 0], ln_b[1, 0],
                            rwt[1], router_b[1], seq=seq, tl=tl, alpha=alpha)
    x4 = _moe(hlin, logt, x3, mod[1], ln_g[1, 1], ln_b[1, 1], moe_w_in, moe_b_in,
              moe_w_out, moe_b_out, layer=1, seq=seq, tt=tt, rb=rb, alpha=alpha)
    return x4.reshape(bsz, seq, d)
```

```python
import functools
import math

import jax
import jax.numpy as jnp
from jax import lax
from jax.experimental import pallas as pl
from jax.experimental.pallas import tpu as pltpu

F32 = jnp.float32
BF16 = jnp.bfloat16
HIGHEST = lax.Precision.HIGHEST

LANES = 128
N_HEADS = 16
TOP_K = 4
CONV_HALO = 32
SWIGLU_LIMIT = 7.0
SWIGLU_ALPHA = 1.702
LN_EPS = 1e-5
LOG2E = 1.4426950408889634
CHUNK = 8
VMEM_LIMIT = 56 * 1024 * 1024


def _cparams(n_axes):
    return pltpu.CompilerParams(
        dimension_semantics=("arbitrary",) * n_axes, vmem_limit_bytes=VMEM_LIMIT)


def _layer_norm(x, g, b):
    mu = jnp.mean(x, axis=-1, keepdims=True)
    xc = x - mu
    var = jnp.mean(xc * xc, axis=-1, keepdims=True)
    return xc * lax.rsqrt(var + LN_EPS) * g + b


def _silu(x):
    return x * jax.nn.sigmoid(x)


def _ada_kernel(c_ref, w_ref, b_ref, o_ref):
    o_ref[0] = jnp.dot(_silu(c_ref[...]), w_ref[0], precision=HIGHEST,
                       preferred_element_type=F32) + b_ref[0]


def _ada(c, ada_w, ada_b):
    depth, d, n = ada_w.shape
    bsz = c.shape[0]
    tn = d
    return pl.pallas_call(
        _ada_kernel,
        grid=(depth, n // tn),
        in_specs=[pl.BlockSpec((bsz, d), lambda l, j: (0, 0)),
                  pl.BlockSpec((1, d, tn), lambda l, j: (l, 0, j)),
                  pl.BlockSpec((1, 1, tn), lambda l, j: (l, 0, j))],
        out_specs=pl.BlockSpec((1, bsz, tn), lambda l, j: (l, 0, j)),
        out_shape=jax.ShapeDtypeStruct((depth, bsz, n), F32),
        compiler_params=_cparams(2),
        name="ada",
    )(c, ada_w, ada_b.reshape(depth, 1, n))


def _post_norm_route(xres, y, mod_ref, lng_ref, lnb_ref, rwt_ref, rb_ref,
                     x_out_ref, hlin_ref, logt_ref, alpha):
    tl, d = xres.shape
    g1 = mod_ref[2:3, :]
    sh2 = mod_ref[3:4, :]
    sc2 = mod_ref[4:5, :]
    x1 = _layer_norm(alpha * xres + (1.0 + g1) * y, lng_ref[...], lnb_ref[...])
    x_out_ref[...] = x1
    h2 = x1 * (1.0 + sc2) + sh2
    p = d // LANES
    for s in range(p):
        hlin_ref[pl.ds(s, tl, stride=p), :] = h2[:, s * LANES:(s + 1) * LANES]
    logt_ref[...] = lax.dot_general(
        rwt_ref[...], h2, (((1,), (1,)), ((), ())), precision=HIGHEST,
        preferred_element_type=F32) + rb_ref[...]


def _conv_kernel(x_ref, mod_ref, w1_ref, b1_ref, dw_ref, db_ref, clg_ref, clb_ref,
                 w2_ref, b2_ref, lng_ref, lnb_ref, rwt_ref, rb_ref,
                 x_out_ref, hlin_ref, logt_ref,
                 ubuf, sbuf, dwb, cbuf, *, alpha, width):
    tl, d = x_ref.shape
    first_tile = pl.program_id(1) == 0

    @pl.when(jnp.logical_and(pl.program_id(0) == 0, first_tile))
    def _():
        for j in range(width):
            dwb[j] = jnp.broadcast_to(dw_ref[j:j + 1, :], (8, d))

    x = x_ref[...]
    sh1 = mod_ref[0:1, :]
    sc1 = mod_ref[1:2, :]
    h = (x * (1.0 + sc1) + sh1).astype(BF16)
    u = jnp.dot(h, w1_ref[...], preferred_element_type=F32) + b1_ref[...]
    glu = u[:, :d] * jax.nn.sigmoid(u[:, d:])

    @pl.when(first_tile)
    def _():
        ubuf[0:CONV_HALO, :] = jnp.zeros((CONV_HALO, d), F32)

    ubuf[CONV_HALO:CONV_HALO + tl, :] = glu
    for r in range(1, 8):
        sbuf[r - 1] = ubuf[pl.ds(r, tl + CONV_HALO - 8), :]

    off0 = CONV_HALO - (width - 1)
    cw = 512
    rows = 32

    def body(i, carry):
        base = pl.multiple_of(i * rows, rows)
        for c in range(d // cw):
            cs = slice(c * cw, (c + 1) * cw)
            accs = [jnp.broadcast_to(db_ref[:, cs], (8, cw)) for _ in range(rows // 8)]
            for j in range(width):
                q, r = divmod(off0 + j, 8)
                w = dwb[j, :, cs]
                for a in range(rows // 8):
                    start = base + 8 * (q + a)
                    if r == 0:
                        v = ubuf[pl.ds(start, 8), cs]
                    else:
                        v = sbuf[r - 1, pl.ds(start, 8), cs]
                    accs[a] = accs[a] + w * v
            for a in range(rows // 8):
                cbuf[pl.ds(base + 8 * a, 8), cs] = accs[a]
        return carry

    lax.fori_loop(0, tl // rows, body, 0)
    ubuf[0:CONV_HALO, :] = ubuf[tl:tl + CONV_HALO, :]

    v = _silu(_layer_norm(cbuf[...], clg_ref[...], clb_ref[...]))
    y = jnp.dot(v.astype(BF16), w2_ref[...], preferred_element_type=F32) + b2_ref[...]
    _post_norm_route(x, y, mod_ref, lng_ref, lnb_ref, rwt_ref, rb_ref,
                     x_out_ref, hlin_ref, logt_ref, alpha)


def _conv_mixer(x2d, mod, w1, b1, dw, db, clg, clb, w2, b2, lng, lnb, rwt, rb,
                *, bsz, seq, tl, alpha):
    t, d = x2d.shape
    e = rwt.shape[0]
    width = dw.shape[0]
    p = d // LANES
    nlt = seq // tl
    row = lambda a: a.reshape(1, -1)
    full = lambda shape: pl.BlockSpec(shape, lambda b, l: (0,) * len(shape))
    return pl.pallas_call(
        functools.partial(_conv_kernel, alpha=alpha, width=width),
        grid=(bsz, nlt),
        in_specs=[pl.BlockSpec((tl, d), lambda b, l: (b * nlt + l, 0)),
                  pl.BlockSpec((None, 6, d), lambda b, l: (b, 0, 0)),
                  full((d, 2 * d)), full((1, 2 * d)), full((width, d)), full((1, d)),
                  full((1, d)), full((1, d)), full((d, d)), full((1, d)),
                  full((1, d)), full((1, d)), full((e, d)), full((e, 1))],
        out_specs=[pl.BlockSpec((tl, d), lambda b, l: (b * nlt + l, 0)),
                   pl.BlockSpec((tl * p, LANES), lambda b, l: (b * nlt + l, 0)),
                   pl.BlockSpec((e, tl), lambda b, l: (0, b * nlt + l))],
        out_shape=[jax.ShapeDtypeStruct((t, d), F32),
                   jax.ShapeDtypeStruct((t * p, LANES), F32),
                   jax.ShapeDtypeStruct((e, t), F32)],
        scratch_shapes=[pltpu.VMEM((tl + CONV_HALO, d), F32),
                        pltpu.VMEM((7, tl + CONV_HALO - 8, d), F32),
                        pltpu.VMEM((width, 8, d), F32),
                        pltpu.VMEM((tl, d), F32)],
        compiler_params=_cparams(2),
        name="conv_mixer",
    )(x2d, mod, w1, row(b1), dw, row(db), row(clg), row(clb), w2, row(b2),
      row(lng), row(lnb), rwt, rb.reshape(e, 1))


def _route_kernel(logt_ref, slot_ref, gate_ref, cnt_ref, *, p, stage_rows):
    e, tt = logt_ref.shape
    l = logt_ref[...]
    iota_e = lax.broadcasted_iota(jnp.int32, (e, tt), 0).astype(F32)
    sels, vals = [], []
    for _ in range(TOP_K):
        m = jnp.max(l, axis=0, keepdims=True)
        idx = jnp.min(jnp.where(l == m, iota_e, float(e)), axis=0, keepdims=True)
        sels.append(idx)
        vals.append(m)
        l = jnp.where(iota_e == idx, -jnp.inf, l)
    exps = [jnp.exp(v - vals[0]) for v in vals]
    denom = exps[0] + exps[1] + exps[2] + exps[3]
    onehots = [iota_e == s for s in sels]
    member = jnp.zeros((e, tt), F32)
    for oh in onehots:
        member = member + jnp.where(oh, 1.0, 0.0)
    tri = jnp.where(lax.broadcasted_iota(jnp.int32, (tt, tt), 0)
                    < lax.broadcasted_iota(jnp.int32, (tt, tt), 1), 1.0, 0.0).astype(BF16)
    rank = jnp.dot(member.astype(BF16), tri, preferred_element_type=F32)
    cnt = jnp.sum(member, axis=1, keepdims=True)
    padded = jnp.floor((cnt + (CHUNK - 1)) * (1.0 / CHUNK)) * CHUNK
    low = jnp.where(lax.broadcasted_iota(jnp.int32, (e, e), 1)
                    < lax.broadcasted_iota(jnp.int32, (e, e), 0), 1.0, 0.0)
    loff = jnp.dot(low, jnp.broadcast_to(padded, (e, tt)), precision=HIGHEST,
                   preferred_element_type=F32)
    pos = loff + rank + (pl.program_id(0) % 2 * stage_rows).astype(F32)
    for k in range(TOP_K):
        slot = jnp.sum(jnp.where(onehots[k], pos, 0.0), axis=0, keepdims=True)
        slot_ref[0, k:k + 1, :] = (slot * p).astype(jnp.int32)
        gate_ref[0, k:k + 1, :] = exps[k] / denom
    cnt_ref[0] = jnp.broadcast_to(cnt, (e, LANES)).astype(jnp.int32)


def _route(logt, tt, p, stage_rows):
    e, t = logt.shape
    nt = t // tt
    return pl.pallas_call(
        functools.partial(_route_kernel, p=p, stage_rows=stage_rows),
        grid=(nt,),
        in_specs=[pl.BlockSpec((e, tt), lambda i: (0, i))],
        out_specs=[pl.BlockSpec((1, TOP_K, tt), lambda i: (i, 0, 0)),
                   pl.BlockSpec((1, TOP_K, tt), lambda i: (i, 0, 0)),
                   pl.BlockSpec((1, e, LANES), lambda i: (i, 0, 0))],
        out_shape=[jax.ShapeDtypeStruct((nt, TOP_K, tt), jnp.int32),
                   jax.ShapeDtypeStruct((nt, TOP_K, tt), F32),
                   jax.ShapeDtypeStruct((nt, e, LANES), jnp.int32)],
        compiler_params=_cparams(1),
        name="route",
    )(logt)


def _plan(cnt, rb, nb, n_classes, p):
    nt, e = cnt.shape
    padded = ((cnt + CHUNK - 1) // CHUNK) * CHUNK
    loff = jnp.cumsum(padded, axis=1) - padded
    rows_e = jnp.sum(padded, axis=0)
    blocks_e = (rows_e + rb - 1) // rb
    bend = jnp.cumsum(blocks_e)
    pstart = (bend - blocks_e) * rb
    goff = pstart[None, :] + jnp.cumsum(padded, axis=0) - padded
    block_ids = jnp.arange(nb, dtype=bend.dtype)
    block_e = jnp.minimum(jnp.sum(bend[None, :] <= block_ids[:, None], axis=1), e - 1)
    chunks = padded // CHUNK
    classcnt = jnp.stack([jnp.sum((chunks >> k) & 1, axis=1) for k in range(n_classes)], axis=1)
    tail_start = pstart + rows_e
    tail_chunks = (blocks_e * rb - rows_e) // CHUNK
    i32 = lambda a: a.astype(jnp.int32)
    return (i32(chunks.reshape(-1)), i32(loff.reshape(-1) * p), i32(goff.reshape(-1) * p),
            i32(classcnt.reshape(-1)), i32(tail_start), i32(tail_chunks), i32(block_e), i32(bend[-1:]))


def _size_classes(tt):
    return (tt // CHUNK).bit_length()


def _start_group_dmas(chunks_ref, loff_ref, goff_ref, tile, n_experts, p, n_classes, make):
    unroll = 4
    assert n_experts % unroll == 0
    rows = CHUNK * p

    def per_experts(g, carry):
        for u in range(unroll):
            idx = tile * n_experts + g * unroll + u
            n = chunks_ref[idx]
            lo = loff_ref[idx]
            go = goff_ref[idx]
            for k in range(n_classes):
                @pl.when(n & (1 << k) != 0)
                def _():
                    done = (n & ~((2 << k) - 1)) * rows
                    make(pl.multiple_of(lo + done, rows), pl.multiple_of(go + done, rows),
                         rows << k).start()
        return carry

    lax.fori_loop(0, n_experts // unroll, per_experts, 0)


def _wait_group_dmas(classcnt_ref, tile, p, n_classes, make):
    for k in range(n_classes):
        def wait_one(j, c, k=k):
            make(0, 0, (CHUNK << k) * p).wait()
            return c

        lax.fori_loop(0, classcnt_ref[tile * n_classes + k], wait_one, 0)


def _dispatch_kernel(padded_ref, loff_ref, goff_ref, classcnt_ref, tstart_ref, tchunks_ref,
                     x_ref, slot_ref, xs_ref, stage, sem, *, n_experts):
    tile = pl.program_id(0)
    n_tiles = pl.num_programs(0)
    tt = slot_ref.shape[0] // TOP_K
    p = x_ref.shape[0] // tt
    buf_rows = stage.shape[0] // 2
    n_classes = _size_classes(tt)
    buf = tile % 2

    def make_for(which_buf):
        base = pl.multiple_of(which_buf * buf_rows, CHUNK * p)
        return lambda lrow, grow, nrows: pltpu.make_async_copy(
            stage.at[pl.ds(base + lrow, nrows), :], xs_ref.at[pl.ds(grow, nrows), :],
            sem.at[which_buf])

    @pl.when(tile < 2)
    def _():
        start = pl.multiple_of(buf * buf_rows, CHUNK * p)
        stage[pl.ds(start, buf_rows), :] = jnp.zeros((buf_rows, LANES), F32)

    @pl.when(tile >= 2)
    def _():
        _wait_group_dmas(classcnt_ref, tile - 2, p, n_classes, make_for(buf))

    @pl.when(tile == 0)
    def _():
        rows = CHUNK * p
        tail = lambda grow: pltpu.make_async_copy(
            stage.at[pl.ds(0, rows), :], xs_ref.at[pl.ds(grow, rows), :], sem.at[0])

        def per_expert(ex, total):
            n = tchunks_ref[ex]

            def per_chunk(j, c):
                tail(pl.multiple_of((tstart_ref[ex] + j * CHUNK) * p, rows)).start()
                return c

            lax.fori_loop(0, n, per_chunk, 0)
            return total + n

        total = lax.fori_loop(0, n_experts, per_expert, 0)

        def wait_tail(j, c):
            tail(0).wait()
            return c

        lax.fori_loop(0, total, wait_tail, 0)

    unroll = 8

    def per_tokens(g, carry):
        for u in range(unroll):
            t = g * unroll + u
            v = x_ref[pl.ds(pl.multiple_of(t * p, p), p), :]
            for k in range(TOP_K):
                s = slot_ref[k * tt + t]
                stage[pl.ds(pl.multiple_of(s, p), p), :] = v
        return carry

    lax.fori_loop(0, tt // unroll, per_tokens, 0)
    _start_group_dmas(padded_ref, loff_ref, goff_ref, tile, n_experts, p, n_classes,
                      make_for(buf))

    @pl.when(tile == n_tiles - 1)
    def _():
        @pl.when(tile >= 1)
        def _():
            _wait_group_dmas(classcnt_ref, tile - 1, p, n_classes, make_for(1 - buf))

        _wait_group_dmas(classcnt_ref, tile, p, n_classes, make_for(buf))


def _stage_rows(tt, n_experts):
    rows = TOP_K * tt + n_experts * (CHUNK - 1)
    return -(-rows // CHUNK) * CHUNK


def _dispatch(hlin, slots, tables, tails, *, n_rows, n_experts, d, tt):
    padded, loff, goff, classcnt = tables
    tstart, tchunks = tails
    nt = slots.shape[0] // (TOP_K * tt)
    p = d // LANES
    return pl.pallas_call(
        functools.partial(_dispatch_kernel, n_experts=n_experts),
        grid_spec=pltpu.PrefetchScalarGridSpec(
            num_scalar_prefetch=6,
            grid=(nt,),
            in_specs=[pl.BlockSpec((tt * p, LANES), lambda i, *_: (i, 0)),
                      pl.BlockSpec((TOP_K * tt,), lambda i, *_: (i,),
                                   memory_space=pltpu.SMEM)],
            out_specs=pl.BlockSpec(memory_space=pl.ANY),
            scratch_shapes=[pltpu.VMEM((2 * _stage_rows(tt, n_experts) * p, LANES), F32),
                            pltpu.SemaphoreType.DMA((2,))]),
        out_shape=jax.ShapeDtypeStruct((n_rows * p, LANES), F32),
        compiler_params=_cparams(1),
        name="dispatch",
    )(padded, loff, goff, classcnt, tstart, tchunks, hlin, slots)


def _expert_kernel(be_ref, nu_ref, x_ref, win_ref, bin_ref, wout_ref, bout_ref, y_ref,
                   win_bf, wout_bf):
    b = pl.program_id(0)
    rb = x_ref.shape[0] * LANES // win_ref.shape[1]
    d = win_ref.shape[1]
    f = wout_ref.shape[1]
    p = d // LANES

    @pl.when(b < nu_ref[0])
    def _():
        prev = be_ref[jnp.maximum(b - 1, 0)]

        @pl.when(jnp.logical_or(b == 0, be_ref[b] != prev))
        def _():
            win_bf[...] = win_ref[0].astype(BF16)
            wout_bf[...] = wout_ref[0].astype(BF16)

        x = jnp.concatenate([x_ref[pl.ds(s, rb, stride=p), :] for s in range(p)], axis=-1)
        u = jnp.dot(x.astype(BF16), win_bf[...], preferred_element_type=F32) + bin_ref[0]
        x_glu = jnp.minimum(u[:, :f], SWIGLU_LIMIT)
        x_lin = jnp.clip(u[:, f:], -SWIGLU_LIMIT, SWIGLU_LIMIT)
        act = (x_lin + 1.0) * (x_glu * jax.nn.sigmoid(SWIGLU_ALPHA * x_glu))
        y = jnp.dot(act.astype(BF16), wout_bf[...], preferred_element_type=F32) + bout_ref[0]
        for s in range(p):
            y_ref[pl.ds(s, rb, stride=p), :] = y[:, s * LANES:(s + 1) * LANES]


def _experts(xs, block_e, n_used, w_in, b_in, w_out, b_out, *, layer, rb, nb):
    _, e, d, f2 = w_in.shape
    f = f2 // 2
    p = d // LANES
    last = lambda b, nu: jnp.maximum(jnp.minimum(b, nu[0] - 1), 0)
    blk = lambda b, be, nu: (last(b, nu), 0)
    wsel = lambda b, be, nu: (layer, be[last(b, nu)], 0, 0)
    return pl.pallas_call(
        _expert_kernel,
        grid_spec=pltpu.PrefetchScalarGridSpec(
            num_scalar_prefetch=2,
            grid=(nb,),
            in_specs=[pl.BlockSpec((rb * p, LANES), blk),
                      pl.BlockSpec((None, 1, d, f2), wsel),
                      pl.BlockSpec((None, 1, 1, f2), wsel),
                      pl.BlockSpec((None, 1, f, d), wsel),
                      pl.BlockSpec((None, 1, 1, d), wsel)],
            out_specs=pl.BlockSpec((rb * p, LANES), blk),
            scratch_shapes=[pltpu.VMEM((d, f2), BF16), pltpu.VMEM((f, d), BF16)]),
        out_shape=jax.ShapeDtypeStruct(xs.shape, F32),
        compiler_params=_cparams(1),
        name="experts",
    )(block_e, n_used, xs, w_in, b_in.reshape(-1, e, 1, f2), w_out, b_out.reshape(-1, e, 1, d))


def _combine_kernel(padded_ref, loff_ref, goff_ref, classcnt_ref, ys_ref, slot_ref, gate_ref,
                    xres_ref, mod_ref, lng_ref, lnb_ref, x_out_ref, stage, comb, sem,
                    *, n_experts, alpha):
    tile = pl.program_id(0)
    n_tiles = pl.num_programs(0)
    tt, d = xres_ref.shape
    p = d // LANES
    buf_rows = stage.shape[0] // 2
    n_classes = _size_classes(tt)
    buf = tile % 2

    def make_for(which_buf):
        base = pl.multiple_of(which_buf * buf_rows, CHUNK * p)
        return lambda lrow, grow, nrows: pltpu.make_async_copy(
            ys_ref.at[pl.ds(grow, nrows), :], stage.at[pl.ds(base + lrow, nrows), :],
            sem.at[which_buf])

    def start_tile(which_tile, which_buf):
        _start_group_dmas(padded_ref, loff_ref, goff_ref, which_tile, n_experts, p, n_classes,
                          make_for(which_buf))

    @pl.when(tile == 0)
    def _():
        start_tile(tile, buf)

    @pl.when(tile + 1 < n_tiles)
    def _():
        start_tile(tile + 1, 1 - buf)

    _wait_group_dmas(classcnt_ref, tile, p, n_classes, make_for(buf))
    unroll = 4

    def per_tokens(g, carry):
        for u in range(unroll):
            t = g * unroll + u
            acc = None
            for k in range(TOP_K):
                s = slot_ref[k * tt + t]
                term = gate_ref[k * tt + t] * stage[pl.ds(pl.multiple_of(s, p), p), :]
                acc = term if acc is None else acc + term
            comb[pl.ds(pl.multiple_of(t * p, p), p), :] = acc
        return carry

    lax.fori_loop(0, tt // unroll, per_tokens, 0)
    moe = jnp.concatenate([comb[pl.ds(s, tt, stride=p), :] for s in range(p)], axis=-1)
    g2 = mod_ref[5:6, :]
    x_out_ref[...] = _layer_norm(alpha * xres_ref[...] + (1.0 + g2) * moe,
                                 lng_ref[...], lnb_ref[...])


def _combine(ys, slots, gates, tables, xres, mod, lng, lnb, *, n_experts, seq, tt, alpha):
    padded, loff, goff, classcnt = tables
    t, d = xres.shape
    nt = t // tt
    p = d // LANES
    tiles_per_seq = seq // tt
    smem = lambda: pl.BlockSpec((TOP_K * tt,), lambda i, *_: (i,), memory_space=pltpu.SMEM)
    return pl.pallas_call(
        functools.partial(_combine_kernel, n_experts=n_experts, alpha=alpha),
        grid_spec=pltpu.PrefetchScalarGridSpec(
            num_scalar_prefetch=4,
            grid=(nt,),
            in_specs=[pl.BlockSpec(memory_space=pl.ANY), smem(), smem(),
                      pl.BlockSpec((tt, d), lambda i, *_: (i, 0)),
                      pl.BlockSpec((None, 6, d), lambda i, *_: (i // tiles_per_seq, 0, 0)),
                      pl.BlockSpec((1, d), lambda i, *_: (0, 0)),
                      pl.BlockSpec((1, d), lambda i, *_: (0, 0))],
            out_specs=pl.BlockSpec((tt, d), lambda i, *_: (i, 0)),
            scratch_shapes=[pltpu.VMEM((2 * _stage_rows(tt, n_experts) * p, LANES), F32),
                            pltpu.VMEM((tt * p, LANES), F32),
                            pltpu.SemaphoreType.DMA((2,))]),
        out_shape=jax.ShapeDtypeStruct((t, d), F32),
        compiler_params=_cparams(1),
        name="combine",
    )(padded, loff, goff, classcnt, ys, slots, gates, xres, mod, lng.reshape(1, d), lnb.reshape(1, d))


def _moe(hlin, logt, xres, mod, lng, lnb, w_in, b_in, w_out, b_out, *, layer, seq, tt, rb,
         alpha):
    e, t = logt.shape
    d = xres.shape[1]
    nt = t // tt
    slots, gates, cnt = _route(logt, tt, d // LANES, _stage_rows(tt, e))
    slots = slots.reshape(-1)
    gates = gates.reshape(-1)
    max_rows = TOP_K * t + nt * e * (CHUNK - 1)
    nb = -(-max_rows // rb) + e
    padded, loff, goff, classcnt, tstart, tchunks, block_e, n_used = _plan(
        cnt[:, :, 0], rb, nb, _size_classes(tt), d // LANES)
    tables = (padded, loff, goff, classcnt)
    xs = _dispatch(hlin, slots, tables, (tstart, tchunks), n_rows=nb * rb, n_experts=e, d=d,
                   tt=tt)
    ys = _experts(xs, block_e, n_used, w_in, b_in, w_out, b_out, layer=layer, rb=rb, nb=nb)
    return _combine(ys, slots, gates, tables, xres, mod, lng, lnb,
                    n_experts=e, seq=seq, tt=tt, alpha=alpha)


def _qkv_kernel(x_ref, mod_ref, qwt_ref, kw_ref, vwt_ref, qt_ref, k_ref, vt_ref):
    x = x_ref[...]
    sh1 = mod_ref[0:1, :]
    sc1 = mod_ref[1:2, :]
    h = (x * (1.0 + sc1) + sh1).astype(BF16)
    nt_dims = (((1,), (1,)), ((), ()))
    qt = lax.dot_general(qwt_ref[...], h, nt_dims, preferred_element_type=F32).astype(BF16)
    tq = qt_ref.shape[3]
    for j in range(qt_ref.shape[1]):
        qt_ref[0, j] = qt[:, j * tq:(j + 1) * tq]
    tk = vt_ref.shape[3]
    ngrp = tk // 8
    r = lax.broadcasted_iota(jnp.int32, (tk, tk), 0)
    s = lax.broadcasted_iota(jnp.int32, (tk, tk), 1)
    perm = jnp.where(s == (r & 7) * ngrp + (r >> 3), 1.0, 0.0).astype(BF16)
    xb = x.astype(BF16)
    xp = jnp.concatenate(
        [jnp.dot(perm, xb[j * tk:(j + 1) * tk, :], preferred_element_type=F32)
         for j in range(vt_ref.shape[1])], axis=0).astype(BF16)
    k_ref[...] = jnp.dot(xp, kw_ref[...], preferred_element_type=F32).astype(BF16)
    vt = lax.dot_general(vwt_ref[...], xp, nt_dims, preferred_element_type=F32).astype(BF16)
    for j in range(vt_ref.shape[1]):
        vt_ref[0, j] = vt[:, j * tk:(j + 1) * tk]


def _qkv(x2d, mod, qwt, kw, vwt, *, bsz, seq, tl, tk):
    t, d = x2d.shape
    nlt = seq // tl
    full = pl.BlockSpec((d, d), lambda b, l: (0, 0))
    return pl.pallas_call(
        _qkv_kernel,
        grid=(bsz, nlt),
        in_specs=[pl.BlockSpec((tl, d), lambda b, l: (b * nlt + l, 0)),
                  pl.BlockSpec((None, 6, d), lambda b, l: (b, 0, 0)),
                  full, full, full],
        out_specs=[pl.BlockSpec((1, tl // tk, d, tk), lambda b, l: (b, l, 0, 0)),
                   pl.BlockSpec((tl, d), lambda b, l: (b * nlt + l, 0)),
                   pl.BlockSpec((1, tl // tk, d, tk), lambda b, l: (b, l, 0, 0))],
        out_shape=[jax.ShapeDtypeStruct((bsz, seq // tk, d, tk), BF16),
                   jax.ShapeDtypeStruct((t, d), BF16),
                   jax.ShapeDtypeStruct((bsz, seq // tk, d, tk), BF16)],
        compiler_params=_cparams(2),
        name="qkv",
    )(x2d, mod, qwt, kw, vwt)


def _stick_block(z_ref, a_ref, carry, masked):
    tk, tq = z_ref.shape
    ngrp = tk // 8
    sub = lax.broadcasted_iota(jnp.int32, (8, tq), 0)
    lane = lax.broadcasted_iota(jnp.int32, (8, tq), 1)
    causal = lambda g: sub * ngrp + g < lane
    run = None
    for g in reversed(range(ngrp)):
        z = z_ref[g * 8:(g + 1) * 8, :]
        rest = 1.0 / (1.0 + jnp.exp2(z))
        beta = 1.0 - rest
        if masked:
            rest = jnp.where(causal(g), rest, 1.0)
            beta = jnp.where(causal(g), beta, 0.0)
        z_ref[g * 8:(g + 1) * 8, :] = beta if run is None else beta * run
        run = rest if run is None else run * rest
    s = run
    for k in (1, 2, 4):
        s = s * jnp.where(sub < 8 - k, pltpu.roll(s, 8 - k, axis=0), 1.0)
    later = jnp.where(sub < 7, pltpu.roll(s, 7, axis=0), 1.0) * carry
    for g in range(0, ngrp, 2):
        a_ref[g * 8:(g + 2) * 8, :] = jnp.concatenate(
            [z_ref[gg * 8:(gg + 1) * 8, :] * later for gg in (g, g + 1)], axis=0).astype(BF16)
    return carry * jnp.broadcast_to(s[0:1, :], (8, tq))


def _attn_kernel(qt_ref, k_ref, vt_ref, o_ref, z0, z1, z2, a0, a1, acc_buf, qm_buf, carry_buf,
                 *, head_dim):
    nq, _, tq = qt_ref.shape[1:]
    tk = vt_ref.shape[3]
    hp = qt_ref.shape[2] // head_dim
    row = lax.broadcasted_iota(jnp.int32, (hp * head_dim, tq), 0)

    def set_queries(qi):
        qf = qt_ref[0, qi].astype(F32)
        for h in range(hp):
            in_head = jnp.logical_and(row >= h * head_dim, row < (h + 1) * head_dim)
            qm_buf[h] = jnp.where(in_head, qf, 0.0).astype(BF16)

    def scores(kb, z_out):
        kblk = k_ref[pl.ds(pl.multiple_of(kb * tk, tk), tk), :]
        for h in range(hp):
            z_out[h] = jnp.dot(kblk, qm_buf[h], preferred_element_type=F32)

    def weighted_values(kb, a_in):
        return [jnp.dot(vt_ref[0, kb, h * head_dim:(h + 1) * head_dim, :], a_in[h],
                        preferred_element_type=F32) for h in range(hp)]

    def diagonal(z_in, a_out):
        one = jnp.ones((8, tq), F32)
        for h in range(hp):
            carry_buf[h] = _stick_block(z_in.at[h], a_out.at[h], one, True)

    def open_next(qi):
        nxt = jnp.minimum(qi + 1, nq - 1)
        set_queries(nxt)
        scores(nxt, z2)

    def step(kb, z_in, z_out, a_in, a_out, carries, last_of=None):
        pv = weighted_values(kb + 1, a_in)
        if last_of is None:
            scores(jnp.maximum(kb - 1, 0), z_out)
        else:
            open_next(last_of)
        carries = tuple(_stick_block(z_in.at[h], a_out.at[h], carries[h], False)
                        for h in range(hp))
        for h in range(hp):
            acc_buf[h] += pv[h]
        return carries

    def finish(qi, a_last):
        nxt = jnp.minimum(qi + 1, nq - 1)
        pv = weighted_values(0, a_last)
        scores(jnp.maximum(nxt - 1, 0), z1)
        out = jnp.concatenate([(acc_buf[h] + pv[h]).T for h in range(hp)], axis=-1)
        o_ref[pl.ds(pl.multiple_of(qi * tq, tq), tq), :] = out.astype(o_ref.dtype)
        diagonal(z2, a0)
        acc_buf[...] = jnp.zeros(acc_buf.shape, F32)

    set_queries(0)
    scores(0, z2)
    diagonal(z2, a0)
    acc_buf[...] = jnp.zeros(acc_buf.shape, F32)
    open_next(0)
    finish(0, a0)

    def query_tile(qi, c):
        def pair(i, carries):
            kb = qi - 1 - 2 * i
            carries = step(kb, z1, z0, a0, a1, carries)
            return step(kb - 1, z0, z1, a1, a0, carries)

        carries = lax.fori_loop(0, (qi - 1) // 2, pair,
                                tuple(carry_buf[h] for h in range(hp)))

        @pl.when(qi % 2 == 1)
        def _():
            step(0, z1, z0, a0, a1, carries, last_of=qi)
            finish(qi, a1)

        @pl.when(qi % 2 == 0)
        def _():
            mid = step(1, z1, z0, a0, a1, carries)
            step(0, z0, z1, a1, a0, mid, last_of=qi)
            finish(qi, a0)

        return c

    lax.fori_loop(1, nq, query_tile, 0)


def _attention(qt, k2d, vt, *, bsz, seq):
    _, nq, d, tq = qt.shape
    head_dim = d // N_HEADS
    tk = vt.shape[3]
    assert tk == tq, "the diagonal mask assumes square blocks"
    ncol = d // LANES
    hp = LANES // head_dim
    return pl.pallas_call(
        functools.partial(_attn_kernel, head_dim=head_dim),
        grid=(bsz, ncol),
        in_specs=[pl.BlockSpec((1, nq, LANES, tq), lambda b, c: (b, 0, c, 0)),
                  pl.BlockSpec((seq, LANES), lambda b, c: (b, c)),
                  pl.BlockSpec((1, seq // tk, LANES, tk), lambda b, c: (b, 0, c, 0))],
        out_specs=pl.BlockSpec((seq, LANES), lambda b, c: (b, c)),
        out_shape=jax.ShapeDtypeStruct((bsz * seq, d), BF16),
        scratch_shapes=[pltpu.VMEM((hp, tk, tq), F32), pltpu.VMEM((hp, tk, tq), F32),
                        pltpu.VMEM((hp, tk, tq), F32),
                        pltpu.VMEM((hp, tk, tq), BF16), pltpu.VMEM((hp, tk, tq), BF16),
                        pltpu.VMEM((hp, head_dim, tq), F32),
                        pltpu.VMEM((hp, hp * head_dim, tq), BF16),
                        pltpu.VMEM((hp, 8, tq), F32)],
        compiler_params=_cparams(2),
        name="attention",
    )(qt, k2d, vt)


def _oproj_kernel(o_ref, xres_ref, mod_ref, ow_ref, lng_ref, lnb_ref, rwt_ref, rb_ref,
                  x_out_ref, hlin_ref, logt_ref, *, alpha):
    y = jnp.dot(o_ref[...], ow_ref[...], preferred_element_type=F32)
    _post_norm_route(xres_ref[...], y, mod_ref, lng_ref, lnb_ref, rwt_ref, rb_ref,
                     x_out_ref, hlin_ref, logt_ref, alpha)


def _oproj(o2d, xres, mod, ow, lng, lnb, rwt, rb, *, seq, tl, alpha):
    t, d = xres.shape
    e = rwt.shape[0]
    p = d // LANES
    nlt = seq // tl
    full = lambda shape: pl.BlockSpec(shape, lambda i: (0,) * len(shape))
    return pl.pallas_call(
        functools.partial(_oproj_kernel, alpha=alpha),
        grid=(t // tl,),
        in_specs=[pl.BlockSpec((tl, d), lambda i: (i, 0)),
                  pl.BlockSpec((tl, d), lambda i: (i, 0)),
                  pl.BlockSpec((None, 6, d), lambda i: (i // nlt, 0, 0)),
                  full((d, d)), full((1, d)), full((1, d)), full((e, d)), full((e, 1))],
        out_specs=[pl.BlockSpec((tl, d), lambda i: (i, 0)),
                   pl.BlockSpec((tl * p, LANES), lambda i: (i, 0)),
                   pl.BlockSpec((e, tl), lambda i: (0, i))],
        out_shape=[jax.ShapeDtypeStruct((t, d), F32),
                   jax.ShapeDtypeStruct((t * p, LANES), F32),
                   jax.ShapeDtypeStruct((e, t), F32)],
        compiler_params=_cparams(1),
        name="oproj",
    )(o2d, xres, mod, ow, lng.reshape(1, d), lnb.reshape(1, d), rwt, rb.reshape(e, 1))


def kernel(x, c, ada_w, ada_b, ln_g, ln_b, cv_w1, cv_b1, cv_dw, cv_db, cv_ln_g, cv_ln_b,
           cv_w2, cv_b2, kv_w, q_w, o_w, router_w, router_b, moe_w_in, moe_b_in,
           moe_w_out, moe_b_out):
    bsz, seq, d = x.shape
    depth = ada_w.shape[0]
    n_a = cv_w1.shape[0]
    assert depth == 2 and n_a == 1 and q_w.shape[0] == 1, "one conv layer then one attention layer"
    alpha = (2.0 * depth) ** 0.25
    head_dim = d // N_HEADS
    tl = min(512, seq)
    tt = min(512, seq)
    rb = 512
    tq = min(256, seq)

    mod = _ada(c, ada_w, ada_b).reshape(depth, bsz, 6, d)
    x0 = x.reshape(bsz * seq, d)
    rwt = jnp.swapaxes(router_w, 1, 2)

    x1, hlin, logt = _conv_mixer(
        x0, mod[0], cv_w1[0].astype(BF16), cv_b1[0], cv_dw[0], cv_db[0], cv_ln_g[0], cv_ln_b[0],
        cv_w2[0].astype(BF16), cv_b2[0], ln_g[0, 0], ln_b[0, 0], rwt[0], router_b[0],
        bsz=bsz, seq=seq, tl=tl, alpha=alpha)
    x2 = _moe(hlin, logt, x1, mod[0], ln_g[0, 1], ln_b[0, 1], moe_w_in, moe_b_in,
              moe_w_out, moe_b_out, layer=0, seq=seq, tt=tt, rb=rb, alpha=alpha)

    scale = head_dim ** -0.5
    qwt = (q_w[0] * (scale * LOG2E)).T.astype(BF16)
    kw = kv_w[:, :d].astype(BF16)
    vwt = kv_w[:, d:].T.astype(BF16)
    qt, k2d, vt = _qkv(x2, mod[1], qwt, kw, vwt, bsz=bsz, seq=seq, tl=tl, tk=tq)
    o2d = _attention(qt, k2d, vt, bsz=bsz, seq=seq)
    x3, hlin, logt = _oproj(o2d, x2, mod[1], o_w[0].astype(BF16), ln_g[1, 0], ln_b[1, 0],
                            rwt[1], router_b[1], seq=seq, tl=tl, alpha=alpha)
    x4 = _moe(hlin, logt, x3, mod[1], ln_g[1, 1], ln_b[1, 1], moe_w_in, moe_b_in,
              moe_w_out, moe_b_out, layer=1, seq=seq, tt=tt, rb=rb, alpha=alpha)
    return x4.reshape(bsz, seq, d)
```

```python
import functools
import math

import jax
import jax.numpy as jnp
from jax import lax
from jax.experimental import pallas as pl
from jax.experimental.pallas import tpu as pltpu

F32 = jnp.float32
BF16 = jnp.bfloat16
HIGHEST = lax.Precision.HIGHEST

LANES = 128
N_HEADS = 16
TOP_K = 4
CONV_HALO = 32
SWIGLU_LIMIT = 7.0
SWIGLU_ALPHA = 1.702
LN_EPS = 1e-5
LOG2E = 1.4426950408889634
CHUNK = 8
VMEM_LIMIT = 56 * 1024 * 1024


def _cparams(n_axes):
    return pltpu.CompilerParams(
        dimension_semantics=("arbitrary",) * n_axes, vmem_limit_bytes=VMEM_LIMIT)


def _layer_norm(x, g, b):
    mu = jnp.mean(x, axis=-1, keepdims=True)
    xc = x - mu
    var = jnp.mean(xc * xc, axis=-1, keepdims=True)
    return xc * lax.rsqrt(var + LN_EPS) * g + b


def _silu(x):
    return x * jax.nn.sigmoid(x)


def _bf16_bits(a):
    return lax.bitcast_convert_type(a.astype(BF16).astype(F32), jnp.uint32)


def _pack_rows(x):
    pieces = []
    for s in range(x.shape[1] // (2 * LANES)):
        lo = _bf16_bits(x[:, 2 * s * LANES:(2 * s + 1) * LANES])
        hi = _bf16_bits(x[:, (2 * s + 1) * LANES:(2 * s + 2) * LANES])
        pieces.append((hi & jnp.uint32(0xFFFF0000)) | (lo >> 16))
    return pieces


def _unpack_words(w):
    return (lax.bitcast_convert_type(w << 16, F32),
            lax.bitcast_convert_type(w & jnp.uint32(0xFFFF0000), F32))


def _ada_kernel(c_ref, w_ref, b_ref, o_ref):
    o_ref[0] = jnp.dot(_silu(c_ref[...]), w_ref[0], precision=HIGHEST,
                       preferred_element_type=F32) + b_ref[0]


def _ada(c, ada_w, ada_b):
    depth, d, n = ada_w.shape
    bsz = c.shape[0]
    tn = d
    return pl.pallas_call(
        _ada_kernel,
        grid=(depth, n // tn),
        in_specs=[pl.BlockSpec((bsz, d), lambda l, j: (0, 0)),
                  pl.BlockSpec((1, d, tn), lambda l, j: (l, 0, j)),
                  pl.BlockSpec((1, 1, tn), lambda l, j: (l, 0, j))],
        out_specs=pl.BlockSpec((1, bsz, tn), lambda l, j: (l, 0, j)),
        out_shape=jax.ShapeDtypeStruct((depth, bsz, n), F32),
        compiler_params=_cparams(2),
        name="ada",
    )(c, ada_w, ada_b.reshape(depth, 1, n))


def _post_norm_route(xres, y, mod_ref, lng_ref, lnb_ref, rwt_ref, rb_ref,
                     x_out_ref, hlin_ref, logt_ref, alpha):
    tl, d = xres.shape
    g1 = mod_ref[2:3, :]
    sh2 = mod_ref[3:4, :]
    sc2 = mod_ref[4:5, :]
    x1 = _layer_norm(alpha * xres + (1.0 + g1) * y, lng_ref[...], lnb_ref[...])
    x_out_ref[...] = x1
    h2 = x1 * (1.0 + sc2) + sh2
    pieces = _pack_rows(h2)
    for s, w in enumerate(pieces):
        hlin_ref[pl.ds(s, tl, stride=len(pieces)), :] = w
    logt_ref[...] = lax.dot_general(
        rwt_ref[...], h2, (((1,), (1,)), ((), ())), precision=HIGHEST,
        preferred_element_type=F32) + rb_ref[...]


def _conv_kernel(x_ref, mod_ref, w1_ref, b1_ref, dw_ref, db_ref, clg_ref, clb_ref,
                 w2_ref, b2_ref, lng_ref, lnb_ref, rwt_ref, rb_ref,
                 x_out_ref, hlin_ref, logt_ref,
                 ubuf, sbuf, dwb, cbuf, *, alpha, width):
    tl, d = x_ref.shape
    first_tile = pl.program_id(1) == 0

    @pl.when(jnp.logical_and(pl.program_id(0) == 0, first_tile))
    def _():
        for j in range(width):
            dwb[j] = jnp.broadcast_to(dw_ref[j:j + 1, :], (8, d))

    x = x_ref[...]
    sh1 = mod_ref[0:1, :]
    sc1 = mod_ref[1:2, :]
    h = (x * (1.0 + sc1) + sh1).astype(BF16)
    u = jnp.dot(h, w1_ref[...], preferred_element_type=F32) + b1_ref[...]
    glu = u[:, :d] * jax.nn.sigmoid(u[:, d:])

    @pl.when(first_tile)
    def _():
        ubuf[0:CONV_HALO, :] = jnp.zeros((CONV_HALO, d), F32)

    ubuf[CONV_HALO:CONV_HALO + tl, :] = glu
    for r in range(1, 8):
        sbuf[r - 1] = ubuf[pl.ds(r, tl + CONV_HALO - 8), :]

    off0 = CONV_HALO - (width - 1)
    cw = 512
    rows = 32

    def body(i, carry):
        base = pl.multiple_of(i * rows, rows)
        for c in range(d // cw):
            cs = slice(c * cw, (c + 1) * cw)
            accs = [jnp.broadcast_to(db_ref[:, cs], (8, cw)) for _ in range(rows // 8)]
            for j in range(width):
                q, r = divmod(off0 + j, 8)
                w = dwb[j, :, cs]
                for a in range(rows // 8):
                    start = base + 8 * (q + a)
                    if r == 0:
                        v = ubuf[pl.ds(start, 8), cs]
                    else:
                        v = sbuf[r - 1, pl.ds(start, 8), cs]
                    accs[a] = accs[a] + w * v
            for a in range(rows // 8):
                cbuf[pl.ds(base + 8 * a, 8), cs] = accs[a]
        return carry

    lax.fori_loop(0, tl // rows, body, 0)
    ubuf[0:CONV_HALO, :] = ubuf[tl:tl + CONV_HALO, :]

    v = _silu(_layer_norm(cbuf[...], clg_ref[...], clb_ref[...]))
    y = jnp.dot(v.astype(BF16), w2_ref[...], preferred_element_type=F32) + b2_ref[...]
    _post_norm_route(x, y, mod_ref, lng_ref, lnb_ref, rwt_ref, rb_ref,
                     x_out_ref, hlin_ref, logt_ref, alpha)


def _conv_mixer(x2d, mod, w1, b1, dw, db, clg, clb, w2, b2, lng, lnb, rwt, rb,
                *, bsz, seq, tl, alpha):
    t, d = x2d.shape
    e = rwt.shape[0]
    width = dw.shape[0]
    p = d // (2 * LANES)
    nlt = seq // tl
    row = lambda a: a.reshape(1, -1)
    full = lambda shape: pl.BlockSpec(shape, lambda b, l: (0,) * len(shape))
    return pl.pallas_call(
        functools.partial(_conv_kernel, alpha=alpha, width=width),
        grid=(bsz, nlt),
        in_specs=[pl.BlockSpec((tl, d), lambda b, l: (b * nlt + l, 0)),
                  pl.BlockSpec((None, 6, d), lambda b, l: (b, 0, 0)),
                  full((d, 2 * d)), full((1, 2 * d)), full((width, d)), full((1, d)),
                  full((1, d)), full((1, d)), full((d, d)), full((1, d)),
                  full((1, d)), full((1, d)), full((e, d)), full((e, 1))],
        out_specs=[pl.BlockSpec((tl, d), lambda b, l: (b * nlt + l, 0)),
                   pl.BlockSpec((tl * p, LANES), lambda b, l: (b * nlt + l, 0)),
                   pl.BlockSpec((e, tl), lambda b, l: (0, b * nlt + l))],
        out_shape=[jax.ShapeDtypeStruct((t, d), F32),
                   jax.ShapeDtypeStruct((t * p, LANES), jnp.uint32),
                   jax.ShapeDtypeStruct((e, t), F32)],
        scratch_shapes=[pltpu.VMEM((tl + CONV_HALO, d), F32),
                        pltpu.VMEM((7, tl + CONV_HALO - 8, d), F32),
                        pltpu.VMEM((width, 8, d), F32),
                        pltpu.VMEM((tl, d), F32)],
        compiler_params=_cparams(2),
        name="conv_mixer",
    )(x2d, mod, w1, row(b1), dw, row(db), row(clg), row(clb), w2, row(b2),
      row(lng), row(lnb), rwt, rb.reshape(e, 1))


def _route_kernel(logt_ref, slot_ref, gate_ref, cnt_ref, *, p, stage_rows):
    e, tt = logt_ref.shape
    l = logt_ref[...]
    iota_e = lax.broadcasted_iota(jnp.int32, (e, tt), 0).astype(F32)
    sels, vals = [], []
    for _ in range(TOP_K):
        m = jnp.max(l, axis=0, keepdims=True)
        idx = jnp.min(jnp.where(l == m, iota_e, float(e)), axis=0, keepdims=True)
        sels.append(idx)
        vals.append(m)
        l = jnp.where(iota_e == idx, -jnp.inf, l)
    exps = [jnp.exp(v - vals[0]) for v in vals]
    denom = exps[0] + exps[1] + exps[2] + exps[3]
    onehots = [iota_e == s for s in sels]
    member = jnp.zeros((e, tt), F32)
    for oh in onehots:
        member = member + jnp.where(oh, 1.0, 0.0)
    tri = jnp.where(lax.broadcasted_iota(jnp.int32, (tt, tt), 0)
                    < lax.broadcasted_iota(jnp.int32, (tt, tt), 1), 1.0, 0.0).astype(BF16)
    rank = jnp.dot(member.astype(BF16), tri, preferred_element_type=F32)
    cnt = jnp.sum(member, axis=1, keepdims=True)
    padded = jnp.floor((cnt + (CHUNK - 1)) * (1.0 / CHUNK)) * CHUNK
    low = jnp.where(lax.broadcasted_iota(jnp.int32, (e, e), 1)
                    < lax.broadcasted_iota(jnp.int32, (e, e), 0), 1.0, 0.0)
    loff = jnp.dot(low, jnp.broadcast_to(padded, (e, tt)), precision=HIGHEST,
                   preferred_element_type=F32)
    pos = loff + rank + (pl.program_id(0) % 2 * stage_rows).astype(F32)
    for k in range(TOP_K):
        slot = jnp.sum(jnp.where(onehots[k], pos, 0.0), axis=0, keepdims=True)
        slot_ref[0, k:k + 1, :] = (slot * p).astype(jnp.int32)
        gate_ref[0, k:k + 1, :] = exps[k] / denom
    cnt_ref[0] = jnp.broadcast_to(cnt, (e, LANES)).astype(jnp.int32)


def _route(logt, tt, p, stage_rows):
    e, t = logt.shape
    nt = t // tt
    return pl.pallas_call(
        functools.partial(_route_kernel, p=p, stage_rows=stage_rows),
        grid=(nt,),
        in_specs=[pl.BlockSpec((e, tt), lambda i: (0, i))],
        out_specs=[pl.BlockSpec((1, TOP_K, tt), lambda i: (i, 0, 0)),
                   pl.BlockSpec((1, TOP_K, tt), lambda i: (i, 0, 0)),
                   pl.BlockSpec((1, e, LANES), lambda i: (i, 0, 0))],
        out_shape=[jax.ShapeDtypeStruct((nt, TOP_K, tt), jnp.int32),
                   jax.ShapeDtypeStruct((nt, TOP_K, tt), F32),
                   jax.ShapeDtypeStruct((nt, e, LANES), jnp.int32)],
        compiler_params=_cparams(1),
        name="route",
    )(logt)


def _plan(cnt, rb, nb, n_classes, p):
    nt, e = cnt.shape
    padded = ((cnt + CHUNK - 1) // CHUNK) * CHUNK
    loff = jnp.cumsum(padded, axis=1) - padded
    rows_e = jnp.sum(padded, axis=0)
    blocks_e = (rows_e + rb - 1) // rb
    bend = jnp.cumsum(blocks_e)
    pstart = (bend - blocks_e) * rb
    goff = pstart[None, :] + jnp.cumsum(padded, axis=0) - padded
    block_ids = jnp.arange(nb, dtype=bend.dtype)
    block_e = jnp.minimum(jnp.sum(bend[None, :] <= block_ids[:, None], axis=1), e - 1)
    chunks = padded // CHUNK
    classcnt = jnp.stack([jnp.sum((chunks >> k) & 1, axis=1) for k in range(n_classes)], axis=1)
    tail_start = pstart + rows_e
    tail_chunks = (blocks_e * rb - rows_e) // CHUNK
    i32 = lambda a: a.astype(jnp.int32)
    return (i32(chunks.reshape(-1)), i32(loff.reshape(-1) * p), i32(goff.reshape(-1) * p),
            i32(classcnt.reshape(-1)), i32(tail_start), i32(tail_chunks), i32(block_e), i32(bend[-1:]))


def _size_classes(tt):
    return (tt // CHUNK).bit_length()


def _start_group_dmas(chunks_ref, loff_ref, goff_ref, tile, n_experts, p, n_classes, make):
    unroll = 4
    assert n_experts % unroll == 0
    rows = CHUNK * p

    def per_experts(g, carry):
        for u in range(unroll):
            idx = tile * n_experts + g * unroll + u
            n = chunks_ref[idx]
            lo = loff_ref[idx]
            go = goff_ref[idx]
            for k in range(n_classes):
                @pl.when(n & (1 << k) != 0)
                def _():
                    done = (n & ~((2 << k) - 1)) * rows
                    make(pl.multiple_of(lo + done, rows), pl.multiple_of(go + done, rows),
                         rows << k).start()
        return carry

    lax.fori_loop(0, n_experts // unroll, per_experts, 0)


def _wait_group_dmas(classcnt_ref, tile, p, n_classes, make):
    for k in range(n_classes):
        def wait_one(j, c, k=k):
            make(0, 0, (CHUNK << k) * p).wait()
            return c

        lax.fori_loop(0, classcnt_ref[tile * n_classes + k], wait_one, 0)


def _dispatch_kernel(padded_ref, loff_ref, goff_ref, classcnt_ref, tstart_ref, tchunks_ref,
                     x_ref, slot_ref, xs_ref, stage, sem, *, n_experts):
    tile = pl.program_id(0)
    n_tiles = pl.num_programs(0)
    tt = slot_ref.shape[0] // TOP_K
    p = x_ref.shape[0] // tt
    buf_rows = stage.shape[0] // 2
    n_classes = _size_classes(tt)
    buf = tile % 2

    def make_for(which_buf):
        base = pl.multiple_of(which_buf * buf_rows, CHUNK * p)
        return lambda lrow, grow, nrows: pltpu.make_async_copy(
            stage.at[pl.ds(base + lrow, nrows), :], xs_ref.at[pl.ds(grow, nrows), :],
            sem.at[which_buf])

    @pl.when(tile < 2)
    def _():
        start = pl.multiple_of(buf * buf_rows, CHUNK * p)
        stage[pl.ds(start, buf_rows), :] = jnp.zeros((buf_rows, LANES), stage.dtype)

    @pl.when(tile >= 2)
    def _():
        _wait_group_dmas(classcnt_ref, tile - 2, p, n_classes, make_for(buf))

    @pl.when(tile == 0)
    def _():
        rows = CHUNK * p
        tail = lambda grow: pltpu.make_async_copy(
            stage.at[pl.ds(0, rows), :], xs_ref.at[pl.ds(grow, rows), :], sem.at[0])

        def per_expert(ex, total):
            n = tchunks_ref[ex]

            def per_chunk(j, c):
                tail(pl.multiple_of((tstart_ref[ex] + j * CHUNK) * p, rows)).start()
                return c

            lax.fori_loop(0, n, per_chunk, 0)
            return total + n

        total = lax.fori_loop(0, n_experts, per_expert, 0)

        def wait_tail(j, c):
            tail(0).wait()
            return c

        lax.fori_loop(0, total, wait_tail, 0)

    unroll = 8

    def per_tokens(g, carry):
        for u in range(unroll):
            t = g * unroll + u
            v = x_ref[pl.ds(pl.multiple_of(t * p, p), p), :]
            for k in range(TOP_K):
                s = slot_ref[k * tt + t]
                stage[pl.ds(pl.multiple_of(s, p), p), :] = v
        return carry

    lax.fori_loop(0, tt // unroll, per_tokens, 0)
    _start_group_dmas(padded_ref, loff_ref, goff_ref, tile, n_experts, p, n_classes,
                      make_for(buf))

    @pl.when(tile == n_tiles - 1)
    def _():
        @pl.when(tile >= 1)
        def _():
            _wait_group_dmas(classcnt_ref, tile - 1, p, n_classes, make_for(1 - buf))

        _wait_group_dmas(classcnt_ref, tile, p, n_classes, make_for(buf))


def _stage_rows(tt, n_experts):
    rows = TOP_K * tt + n_experts * (CHUNK - 1)
    return -(-rows // CHUNK) * CHUNK


def _dispatch(hlin, slots, tables, tails, *, n_rows, n_experts, d, tt):
    padded, loff, goff, classcnt = tables
    tstart, tchunks = tails
    nt = slots.shape[0] // (TOP_K * tt)
    p = d // (2 * LANES)
    return pl.pallas_call(
        functools.partial(_dispatch_kernel, n_experts=n_experts),
        grid_spec=pltpu.PrefetchScalarGridSpec(
            num_scalar_prefetch=6,
            grid=(nt,),
            in_specs=[pl.BlockSpec((tt * p, LANES), lambda i, *_: (i, 0)),
                      pl.BlockSpec((TOP_K * tt,), lambda i, *_: (i,),
                                   memory_space=pltpu.SMEM)],
            out_specs=pl.BlockSpec(memory_space=pl.ANY),
            scratch_shapes=[pltpu.VMEM((2 * _stage_rows(tt, n_experts) * p, LANES), jnp.uint32),
                            pltpu.SemaphoreType.DMA((2,))]),
        out_shape=jax.ShapeDtypeStruct((n_rows * p, LANES), jnp.uint32),
        compiler_params=_cparams(1),
        name="dispatch",
    )(padded, loff, goff, classcnt, tstart, tchunks, hlin, slots)


def _expert_kernel(be_ref, nu_ref, x_ref, win_ref, bin_ref, wout_ref, bout_ref, y_ref,
                   win_bf, wout_bf):
    b = pl.program_id(0)
    d = win_ref.shape[1]
    f = wout_ref.shape[1]
    p = d // (2 * LANES)
    rb = x_ref.shape[0] // p

    @pl.when(b < nu_ref[0])
    def _():
        prev = be_ref[jnp.maximum(b - 1, 0)]

        @pl.when(jnp.logical_or(b == 0, be_ref[b] != prev))
        def _():
            win_bf[...] = win_ref[0].astype(BF16)
            wout_bf[...] = wout_ref[0].astype(BF16)

        halves = [h for s in range(p) for h in _unpack_words(x_ref[pl.ds(s, rb, stride=p), :])]
        x = jnp.concatenate(halves, axis=-1)
        u = jnp.dot(x.astype(BF16), win_bf[...], preferred_element_type=F32) + bin_ref[0]
        x_glu = jnp.minimum(u[:, :f], SWIGLU_LIMIT)
        x_lin = jnp.clip(u[:, f:], -SWIGLU_LIMIT, SWIGLU_LIMIT)
        act = (x_lin + 1.0) * (x_glu * jax.nn.sigmoid(SWIGLU_ALPHA * x_glu))
        y = jnp.dot(act.astype(BF16), wout_bf[...], preferred_element_type=F32) + bout_ref[0]
        for s, w in enumerate(_pack_rows(y)):
            y_ref[pl.ds(s, rb, stride=p), :] = w


def _experts(xs, block_e, n_used, w_in, b_in, w_out, b_out, *, layer, rb, nb):
    _, e, d, f2 = w_in.shape
    f = f2 // 2
    p = d // (2 * LANES)
    last = lambda b, nu: jnp.maximum(jnp.minimum(b, nu[0] - 1), 0)
    blk = lambda b, be, nu: (last(b, nu), 0)
    wsel = lambda b, be, nu: (layer, be[last(b, nu)], 0, 0)
    return pl.pallas_call(
        _expert_kernel,
        grid_spec=pltpu.PrefetchScalarGridSpec(
            num_scalar_prefetch=2,
            grid=(nb,),
            in_specs=[pl.BlockSpec((rb * p, LANES), blk),
                      pl.BlockSpec((None, 1, d, f2), wsel),
                      pl.BlockSpec((None, 1, 1, f2), wsel),
                      pl.BlockSpec((None, 1, f, d), wsel),
                      pl.BlockSpec((None, 1, 1, d), wsel)],
            out_specs=pl.BlockSpec((rb * p, LANES), blk),
            scratch_shapes=[pltpu.VMEM((d, f2), BF16), pltpu.VMEM((f, d), BF16)]),
        out_shape=jax.ShapeDtypeStruct(xs.shape, xs.dtype),
        compiler_params=_cparams(1),
        name="experts",
    )(block_e, n_used, xs, w_in, b_in.reshape(-1, e, 1, f2), w_out, b_out.reshape(-1, e, 1, d))


def _combine_kernel(padded_ref, loff_ref, goff_ref, classcnt_ref, ys_ref, slot_ref, gate_ref,
                    xres_ref, mod_ref, lng_ref, lnb_ref, x_out_ref, stage, comb_even, comb_odd, sem,
                    *, n_experts, alpha):
    tile = pl.program_id(0)
    n_tiles = pl.num_programs(0)
    tt, d = xres_ref.shape
    p = d // (2 * LANES)
    buf_rows = stage.shape[0] // 2
    n_classes = _size_classes(tt)
    buf = tile % 2

    def make_for(which_buf):
        base = pl.multiple_of(which_buf * buf_rows, CHUNK * p)
        return lambda lrow, grow, nrows: pltpu.make_async_copy(
            ys_ref.at[pl.ds(grow, nrows), :], stage.at[pl.ds(base + lrow, nrows), :],
            sem.at[which_buf])

    def start_tile(which_tile, which_buf):
        _start_group_dmas(padded_ref, loff_ref, goff_ref, which_tile, n_experts, p, n_classes,
                          make_for(which_buf))

    @pl.when(tile == 0)
    def _():
        start_tile(tile, buf)

    @pl.when(tile + 1 < n_tiles)
    def _():
        start_tile(tile + 1, 1 - buf)

    _wait_group_dmas(classcnt_ref, tile, p, n_classes, make_for(buf))
    unroll = 4

    def per_tokens(g, carry):
        for u in range(unroll):
            t = g * unroll + u
            acc = None
            for k in range(TOP_K):
                s = slot_ref[k * tt + t]
                gate = gate_ref[k * tt + t]
                lo, hi = _unpack_words(stage[pl.ds(pl.multiple_of(s, p), p), :])
                lo, hi = gate * lo, gate * hi
                acc = (lo, hi) if acc is None else (acc[0] + lo, acc[1] + hi)
            row = pl.ds(pl.multiple_of(t * p, p), p)
            comb_even[row, :] = acc[0]
            comb_odd[row, :] = acc[1]
        return carry

    lax.fori_loop(0, tt // unroll, per_tokens, 0)
    moe = jnp.concatenate([(comb_odd if c % 2 else comb_even)[pl.ds(c // 2, tt, stride=p), :]
                           for c in range(2 * p)], axis=-1)
    g2 = mod_ref[5:6, :]
    x_out_ref[...] = _layer_norm(alpha * xres_ref[...] + (1.0 + g2) * moe,
                                 lng_ref[...], lnb_ref[...])


def _combine(ys, slots, gates, tables, xres, mod, lng, lnb, *, n_experts, seq, tt, alpha):
    padded, loff, goff, classcnt = tables
    t, d = xres.shape
    nt = t // tt
    p = d // (2 * LANES)
    tiles_per_seq = seq // tt
    smem = lambda: pl.BlockSpec((TOP_K * tt,), lambda i, *_: (i,), memory_space=pltpu.SMEM)
    return pl.pallas_call(
        functools.partial(_combine_kernel, n_experts=n_experts, alpha=alpha),
        grid_spec=pltpu.PrefetchScalarGridSpec(
            num_scalar_prefetch=4,
            grid=(nt,),
            in_specs=[pl.BlockSpec(memory_space=pl.ANY), smem(), smem(),
                      pl.BlockSpec((tt, d), lambda i, *_: (i, 0)),
                      pl.BlockSpec((None, 6, d), lambda i, *_: (i // tiles_per_seq, 0, 0)),
                      pl.BlockSpec((1, d), lambda i, *_: (0, 0)),
                      pl.BlockSpec((1, d), lambda i, *_: (0, 0))],
            out_specs=pl.BlockSpec((tt, d), lambda i, *_: (i, 0)),
            scratch_shapes=[pltpu.VMEM((2 * _stage_rows(tt, n_experts) * p, LANES), jnp.uint32),
                            pltpu.VMEM((tt * p, LANES), F32), pltpu.VMEM((tt * p, LANES), F32),
                            pltpu.SemaphoreType.DMA((2,))]),
        out_shape=jax.ShapeDtypeStruct((t, d), F32),
        compiler_params=_cparams(1),
        name="combine",
    )(padded, loff, goff, classcnt, ys, slots, gates, xres, mod, lng.reshape(1, d), lnb.reshape(1, d))


def _moe(hlin, logt, xres, mod, lng, lnb, w_in, b_in, w_out, b_out, *, layer, seq, tt, rb,
         alpha):
    e, t = logt.shape
    d = xres.shape[1]
    nt = t // tt
    p = d // (2 * LANES)
    slots, gates, cnt = _route(logt, tt, p, _stage_rows(tt, e))
    slots = slots.reshape(-1)
    gates = gates.reshape(-1)
    max_rows = TOP_K * t + nt * e * (CHUNK - 1)
    nb = -(-max_rows // rb) + e
    padded, loff, goff, classcnt, tstart, tchunks, block_e, n_used = _plan(
        cnt[:, :, 0], rb, nb, _size_classes(tt), p)
    tables = (padded, loff, goff, classcnt)
    xs = _dispatch(hlin, slots, tables, (tstart, tchunks), n_rows=nb * rb, n_experts=e, d=d,
                   tt=tt)
    ys = _experts(xs, block_e, n_used, w_in, b_in, w_out, b_out, layer=layer, rb=rb, nb=nb)
    return _combine(ys, slots, gates, tables, xres, mod, lng, lnb,
                    n_experts=e, seq=seq, tt=tt, alpha=alpha)


def _qkv_kernel(x_ref, mod_ref, qwt_ref, kw_ref, vwt_ref, qt_ref, k_ref, vt_ref):
    x = x_ref[...]
    sh1 = mod_ref[0:1, :]
    sc1 = mod_ref[1:2, :]
    h = (x * (1.0 + sc1) + sh1).astype(BF16)
    nt_dims = (((1,), (1,)), ((), ()))
    qt = lax.dot_general(qwt_ref[...], h, nt_dims, preferred_element_type=F32).astype(BF16)
    tq = qt_ref.shape[3]
    for j in range(qt_ref.shape[1]):
        qt_ref[0, j] = qt[:, j * tq:(j + 1) * tq]
    tk = vt_ref.shape[3]
    ngrp = tk // 8
    r = lax.broadcasted_iota(jnp.int32, (tk, tk), 0)
    s = lax.broadcasted_iota(jnp.int32, (tk, tk), 1)
    perm = jnp.where(s == (r & 7) * ngrp + (r >> 3), 1.0, 0.0).astype(BF16)
    xb = x.astype(BF16)
    xp = jnp.concatenate(
        [jnp.dot(perm, xb[j * tk:(j + 1) * tk, :], preferred_element_type=F32)
         for j in range(vt_ref.shape[1])], axis=0).astype(BF16)
    k_ref[...] = jnp.dot(xp, kw_ref[...], preferred_element_type=F32).astype(BF16)
    vt = lax.dot_general(vwt_ref[...], xp, nt_dims, preferred_element_type=F32).astype(BF16)
    for j in range(vt_ref.shape[1]):
        vt_ref[0, j] = vt[:, j * tk:(j + 1) * tk]


def _qkv(x2d, mod, qwt, kw, vwt, *, bsz, seq, tl, tk):
    t, d = x2d.shape
    nlt = seq // tl
    full = pl.BlockSpec((d, d), lambda b, l: (0, 0))
    return pl.pallas_call(
        _qkv_kernel,
        grid=(bsz, nlt),
        in_specs=[pl.BlockSpec((tl, d), lambda b, l: (b * nlt + l, 0)),
                  pl.BlockSpec((None, 6, d), lambda b, l: (b, 0, 0)),
                  full, full, full],
        out_specs=[pl.BlockSpec((1, tl // tk, d, tk), lambda b, l: (b, l, 0, 0)),
                   pl.BlockSpec((tl, d), lambda b, l: (b * nlt + l, 0)),
                   pl.BlockSpec((1, tl // tk, d, tk), lambda b, l: (b, l, 0, 0))],
        out_shape=[jax.ShapeDtypeStruct((bsz, seq // tk, d, tk), BF16),
                   jax.ShapeDtypeStruct((t, d), BF16),
                   jax.ShapeDtypeStruct((bsz, seq // tk, d, tk), BF16)],
        compiler_params=_cparams(2),
        name="qkv",
    )(x2d, mod, qwt, kw, vwt)


def _stick_block(z_ref, a_ref, carry, masked):
    tk, tq = z_ref.shape
    ngrp = tk // 8
    sub = lax.broadcasted_iota(jnp.int32, (8, tq), 0)
    lane = lax.broadcasted_iota(jnp.int32, (8, tq), 1)
    causal = lambda g: sub * ngrp + g < lane
    run = None
    for g in reversed(range(ngrp)):
        z = z_ref[g * 8:(g + 1) * 8, :]
        rest = 1.0 / (1.0 + jnp.exp2(z))
        beta = 1.0 - rest
        if masked:
            rest = jnp.where(causal(g), rest, 1.0)
            beta = jnp.where(causal(g), beta, 0.0)
        z_ref[g * 8:(g + 1) * 8, :] = beta if run is None else beta * run
        run = rest if run is None else run * rest
    s = run
    for k in (1, 2, 4):
        s = s * jnp.where(sub < 8 - k, pltpu.roll(s, 8 - k, axis=0), 1.0)
    later = jnp.where(sub < 7, pltpu.roll(s, 7, axis=0), 1.0) * carry
    for g in range(0, ngrp, 2):
        a_ref[g * 8:(g + 2) * 8, :] = jnp.concatenate(
            [z_ref[gg * 8:(gg + 1) * 8, :] * later for gg in (g, g + 1)], axis=0).astype(BF16)
    return carry * jnp.broadcast_to(s[0:1, :], (8, tq))


def _attn_kernel(qt_ref, k_ref, vt_ref, o_ref, z0, z1, z2, a0, a1, acc_buf, qm_buf, carry_buf,
                 *, head_dim):
    nq, _, tq = qt_ref.shape[1:]
    tk = vt_ref.shape[3]
    hp = qt_ref.shape[2] // head_dim
    row = lax.broadcasted_iota(jnp.int32, (hp * head_dim, tq), 0)

    def set_queries(qi):
        qf = qt_ref[0, qi].astype(F32)
        for h in range(hp):
            in_head = jnp.logical_and(row >= h * head_dim, row < (h + 1) * head_dim)
            qm_buf[h] = jnp.where(in_head, qf, 0.0).astype(BF16)

    def scores(kb, z_out):
        kblk = k_ref[pl.ds(pl.multiple_of(kb * tk, tk), tk), :]
        for h in range(hp):
            z_out[h] = jnp.dot(kblk, qm_buf[h], preferred_element_type=F32)

    def weighted_values(kb, a_in):
        return [jnp.dot(vt_ref[0, kb, h * head_dim:(h + 1) * head_dim, :], a_in[h],
                        preferred_element_type=F32) for h in range(hp)]

    def diagonal(z_in, a_out):
        one = jnp.ones((8, tq), F32)
        for h in range(hp):
            carry_buf[h] = _stick_block(z_in.at[h], a_out.at[h], one, True)

    def open_next(qi):
        nxt = jnp.minimum(qi + 1, nq - 1)
        set_queries(nxt)
        scores(nxt, z2)

    def step(kb, z_in, z_out, a_in, a_out, carries, last_of=None):
        pv = weighted_values(kb + 1, a_in)
        if last_of is None:
            scores(jnp.maximum(kb - 1, 0), z_out)
        else:
            open_next(last_of)
        carries = tuple(_stick_block(z_in.at[h], a_out.at[h], carries[h], False)
                        for h in range(hp))
        for h in range(hp):
            acc_buf[h] += pv[h]
        return carries

    def finish(qi, a_last):
        nxt = jnp.minimum(qi + 1, nq - 1)
        pv = weighted_values(0, a_last)
        scores(jnp.maximum(nxt - 1, 0), z1)
        out = jnp.concatenate([(acc_buf[h] + pv[h]).T for h in range(hp)], axis=-1)
        o_ref[pl.ds(pl.multiple_of(qi * tq, tq), tq), :] = out.astype(o_ref.dtype)
        diagonal(z2, a0)
        acc_buf[...] = jnp.zeros(acc_buf.shape, F32)

    set_queries(0)
    scores(0, z2)
    diagonal(z2, a0)
    acc_buf[...] = jnp.zeros(acc_buf.shape, F32)
    open_next(0)
    finish(0, a0)

    def query_tile(qi, c):
        def pair(i, carries):
            kb = qi - 1 - 2 * i
            carries = step(kb, z1, z0, a0, a1, carries)
            return step(kb - 1, z0, z1, a1, a0, carries)

        carries = lax.fori_loop(0, (qi - 1) // 2, pair,
                                tuple(carry_buf[h] for h in range(hp)))

        @pl.when(qi % 2 == 1)
        def _():
            step(0, z1, z0, a0, a1, carries, last_of=qi)
            finish(qi, a1)

        @pl.when(qi % 2 == 0)
        def _():
            mid = step(1, z1, z0, a0, a1, carries)
            step(0, z0, z1, a1, a0, mid, last_of=qi)
            finish(qi, a0)

        return c

    lax.fori_loop(1, nq, query_tile, 0)


def _attention(qt, k2d, vt, *, bsz, seq):
    _, nq, d, tq = qt.shape
    head_dim = d // N_HEADS
    tk = vt.shape[3]
    assert tk == tq, "the diagonal mask assumes square blocks"
    ncol = d // LANES
    hp = LANES // head_dim
    return pl.pallas_call(
        functools.partial(_attn_kernel, head_dim=head_dim),
        grid=(bsz, ncol),
        in_specs=[pl.BlockSpec((1, nq, LANES, tq), lambda b, c: (b, 0, c, 0)),
                  pl.BlockSpec((seq, LANES), lambda b, c: (b, c)),
                  pl.BlockSpec((1, seq // tk, LANES, tk), lambda b, c: (b, 0, c, 0))],
        out_specs=pl.BlockSpec((seq, LANES), lambda b, c: (b, c)),
        out_shape=jax.ShapeDtypeStruct((bsz * seq, d), BF16),
        scratch_shapes=[pltpu.VMEM((hp, tk, tq), F32), pltpu.VMEM((hp, tk, tq), F32),
                        pltpu.VMEM((hp, tk, tq), F32),
                        pltpu.VMEM((hp, tk, tq), BF16), pltpu.VMEM((hp, tk, tq), BF16),
                        pltpu.VMEM((hp, head_dim, tq), F32),
                        pltpu.VMEM((hp, hp * head_dim, tq), BF16),
                        pltpu.VMEM((hp, 8, tq), F32)],
        compiler_params=_cparams(2),
        name="attention",
    )(qt, k2d, vt)


def _oproj_kernel(o_ref, xres_ref, mod_ref, ow_ref, lng_ref, lnb_ref, rwt_ref, rb_ref,
                  x_out_ref, hlin_ref, logt_ref, *, alpha):
    y = jnp.dot(o_ref[...], ow_ref[...], preferred_element_type=F32)
    _post_norm_route(xres_ref[...], y, mod_ref, lng_ref, lnb_ref, rwt_ref, rb_ref,
                     x_out_ref, hlin_ref, logt_ref, alpha)


def _oproj(o2d, xres, mod, ow, lng, lnb, rwt, rb, *, seq, tl, alpha):
    t, d = xres.shape
    e = rwt.shape[0]
    p = d // (2 * LANES)
    nlt = seq // tl
    full = lambda shape: pl.BlockSpec(shape, lambda i: (0,) * len(shape))
    return pl.pallas_call(
        functools.partial(_oproj_kernel, alpha=alpha),
        grid=(t // tl,),
        in_specs=[pl.BlockSpec((tl, d), lambda i: (i, 0)),
                  pl.BlockSpec((tl, d), lambda i: (i, 0)),
                  pl.BlockSpec((None, 6, d), lambda i: (i // nlt, 0, 0)),
                  full((d, d)), full((1, d)), full((1, d)), full((e, d)), full((e, 1))],
        out_specs=[pl.BlockSpec((tl, d), lambda i: (i, 0)),
                   pl.BlockSpec((tl * p, LANES), lambda i: (i, 0)),
                   pl.BlockSpec((e, tl), lambda i: (0, i))],
        out_shape=[jax.ShapeDtypeStruct((t, d), F32),
                   jax.ShapeDtypeStruct((t * p, LANES), jnp.uint32),
                   jax.ShapeDtypeStruct((e, t), F32)],
        compiler_params=_cparams(1),
        name="oproj",
    )(o2d, xres, mod, ow, lng.reshape(1, d), lnb.reshape(1, d), rwt, rb.reshape(e, 1))


def kernel(x, c, ada_w, ada_b, ln_g, ln_b, cv_w1, cv_b1, cv_dw, cv_db, cv_ln_g, cv_ln_b,
           cv_w2, cv_b2, kv_w, q_w, o_w, router_w, router_b, moe_w_in, moe_b_in,
           moe_w_out, moe_b_out):
    bsz, seq, d = x.shape
    depth = ada_w.shape[0]
    n_a = cv_w1.shape[0]
    assert depth == 2 and n_a == 1 and q_w.shape[0] == 1, "one conv layer then one attention layer"
    alpha = (2.0 * depth) ** 0.25
    head_dim = d // N_HEADS
    tl = min(512, seq)
    tt = min(512, seq)
    rb = 512
    tq = min(256, seq)

    mod = _ada(c, ada_w, ada_b).reshape(depth, bsz, 6, d)
    x0 = x.reshape(bsz * seq, d)
    rwt = jnp.swapaxes(router_w, 1, 2)

    x1, hlin, logt = _conv_mixer(
        x0, mod[0], cv_w1[0].astype(BF16), cv_b1[0], cv_dw[0], cv_db[0], cv_ln_g[0], cv_ln_b[0],
        cv_w2[0].astype(BF16), cv_b2[0], ln_g[0, 0], ln_b[0, 0], rwt[0], router_b[0],
        bsz=bsz, seq=seq, tl=tl, alpha=alpha)
    x2 = _moe(hlin, logt, x1, mod[0], ln_g[0, 1], ln_b[0, 1], moe_w_in, moe_b_in,
              moe_w_out, moe_b_out, layer=0, seq=seq, tt=tt, rb=rb, alpha=alpha)

    scale = head_dim ** -0.5
    qwt = (q_w[0] * (scale * LOG2E)).T.astype(BF16)
    kw = kv_w[:, :d].astype(BF16)
    vwt = kv_w[:, d:].T.astype(BF16)
    qt, k2d, vt = _qkv(x2, mod[1], qwt, kw, vwt, bsz=bsz, seq=seq, tl=tl, tk=tq)
    o2d = _attention(qt, k2d, vt, bsz=bsz, seq=seq)
    x3, hlin, logt = _oproj(o2d, x2, mod[1], o_w[0].astype(BF16), ln_g[1, 0], ln_b[1, 0],
                            rwt[1], router_b[1], seq=seq, tl=tl, alpha=alpha)
    x4 = _moe(hlin, logt, x3, mod[1], ln_g[1, 1], ln_b[1, 1], moe_w_in, moe_b_in,
              moe_w_out, moe_b_out, layer=1, seq=seq, tt=tt, rb=rb, alpha=alpha)
    return x4.reshape(bsz, seq, d)
```

```python
import functools
import math

import jax
import jax.numpy as jnp
from jax import lax
from jax.experimental import pallas as pl
from jax.experimental.pallas import tpu as pltpu

F32 = jnp.float32
BF16 = jnp.bfloat16
HIGHEST = lax.Precision.HIGHEST

LANES = 128
N_HEADS = 16
TOP_K = 4
CONV_HALO = 32
SWIGLU_LIMIT = 7.0
SWIGLU_ALPHA = 1.702
LN_EPS = 1e-5
LOG2E = 1.4426950408889634
CHUNK = 8
VMEM_LIMIT = 56 * 1024 * 1024


def _cparams(n_axes):
    return pltpu.CompilerParams(
        dimension_semantics=("arbitrary",) * n_axes, vmem_limit_bytes=VMEM_LIMIT)


def _layer_norm(x, g, b):
    mu = jnp.mean(x, axis=-1, keepdims=True)
    xc = x - mu
    var = jnp.mean(xc * xc, axis=-1, keepdims=True)
    return xc * lax.rsqrt(var + LN_EPS) * g + b


def _silu(x):
    return x * jax.nn.sigmoid(x)


def _bf16_bits(a):
    return lax.bitcast_convert_type(a.astype(BF16).astype(F32), jnp.uint32)


def _pack_rows(x):
    pieces = []
    for s in range(x.shape[1] // (2 * LANES)):
        lo = _bf16_bits(x[:, 2 * s * LANES:(2 * s + 1) * LANES])
        hi = _bf16_bits(x[:, (2 * s + 1) * LANES:(2 * s + 2) * LANES])
        pieces.append((hi & jnp.uint32(0xFFFF0000)) | (lo >> 16))
    return pieces


def _unpack_words(w):
    return (lax.bitcast_convert_type(w << 16, F32),
            lax.bitcast_convert_type(w & jnp.uint32(0xFFFF0000), F32))


def _ada_kernel(c_ref, w_ref, b_ref, o_ref):
    o_ref[0] = jnp.dot(_silu(c_ref[...]), w_ref[0], precision=HIGHEST,
                       preferred_element_type=F32) + b_ref[0]


def _ada(c, ada_w, ada_b):
    depth, d, n = ada_w.shape
    bsz = c.shape[0]
    tn = d
    return pl.pallas_call(
        _ada_kernel,
        grid=(depth, n // tn),
        in_specs=[pl.BlockSpec((bsz, d), lambda l, j: (0, 0)),
                  pl.BlockSpec((1, d, tn), lambda l, j: (l, 0, j)),
                  pl.BlockSpec((1, 1, tn), lambda l, j: (l, 0, j))],
        out_specs=pl.BlockSpec((1, bsz, tn), lambda l, j: (l, 0, j)),
        out_shape=jax.ShapeDtypeStruct((depth, bsz, n), F32),
        compiler_params=_cparams(2),
        name="ada",
    )(c, ada_w, ada_b.reshape(depth, 1, n))


def _post_norm_route(xres, y, mod_ref, lng_ref, lnb_ref, rwt_ref, rb_ref,
                     x_out_ref, hlin_ref, logt_ref, alpha):
    tl, d = xres.shape
    g1 = mod_ref[2:3, :]
    sh2 = mod_ref[3:4, :]
    sc2 = mod_ref[4:5, :]
    x1 = _layer_norm(alpha * xres + (1.0 + g1) * y, lng_ref[...], lnb_ref[...])
    x_out_ref[...] = x1
    h2 = x1 * (1.0 + sc2) + sh2
    pieces = _pack_rows(h2)
    for s, w in enumerate(pieces):
        hlin_ref[pl.ds(s, tl, stride=len(pieces)), :] = w
    logt_ref[...] = lax.dot_general(
        rwt_ref[...], h2, (((1,), (1,)), ((), ())), precision=HIGHEST,
        preferred_element_type=F32) + rb_ref[...]


def _conv_kernel(x_ref, mod_ref, w1_ref, b1_ref, dw_ref, db_ref, clg_ref, clb_ref,
                 w2_ref, b2_ref, lng_ref, lnb_ref, rwt_ref, rb_ref,
                 x_out_ref, hlin_ref, logt_ref,
                 ubuf, sbuf, dwb, cbuf, *, alpha, width):
    tl, d = x_ref.shape
    first_tile = pl.program_id(1) == 0

    @pl.when(jnp.logical_and(pl.program_id(0) == 0, first_tile))
    def _():
        for j in range(width):
            dwb[j] = jnp.broadcast_to(dw_ref[j:j + 1, :], (8, d))

    x = x_ref[...]
    sh1 = mod_ref[0:1, :]
    sc1 = mod_ref[1:2, :]
    h = (x * (1.0 + sc1) + sh1).astype(BF16)
    u = jnp.dot(h, w1_ref[...], preferred_element_type=F32) + b1_ref[...]
    glu = u[:, :d] * jax.nn.sigmoid(u[:, d:])

    @pl.when(first_tile)
    def _():
        ubuf[0:CONV_HALO, :] = jnp.zeros((CONV_HALO, d), F32)

    ubuf[CONV_HALO:CONV_HALO + tl, :] = glu
    for r in range(1, 8):
        sbuf[r - 1] = ubuf[pl.ds(r, tl + CONV_HALO - 8), :]

    off0 = CONV_HALO - (width - 1)
    cw = 512
    rows = 32

    def body(i, carry):
        base = pl.multiple_of(i * rows, rows)
        for c in range(d // cw):
            cs = slice(c * cw, (c + 1) * cw)
            accs = [jnp.broadcast_to(db_ref[:, cs], (8, cw)) for _ in range(rows // 8)]
            for j in range(width):
                q, r = divmod(off0 + j, 8)
                w = dwb[j, :, cs]
                for a in range(rows // 8):
                    start = base + 8 * (q + a)
                    if r == 0:
                        v = ubuf[pl.ds(start, 8), cs]
                    else:
                        v = sbuf[r - 1, pl.ds(start, 8), cs]
                    accs[a] = accs[a] + w * v
            for a in range(rows // 8):
                cbuf[pl.ds(base + 8 * a, 8), cs] = accs[a]
        return carry

    lax.fori_loop(0, tl // rows, body, 0)
    ubuf[0:CONV_HALO, :] = ubuf[tl:tl + CONV_HALO, :]

    v = _silu(_layer_norm(cbuf[...], clg_ref[...], clb_ref[...]))
    y = jnp.dot(v.astype(BF16), w2_ref[...], preferred_element_type=F32) + b2_ref[...]
    _post_norm_route(x, y, mod_ref, lng_ref, lnb_ref, rwt_ref, rb_ref,
                     x_out_ref, hlin_ref, logt_ref, alpha)


def _conv_mixer(x2d, mod, w1, b1, dw, db, clg, clb, w2, b2, lng, lnb, rwt, rb,
                *, bsz, seq, tl, alpha):
    t, d = x2d.shape
    e = rwt.shape[0]
    width = dw.shape[0]
    p = d // (2 * LANES)
    nlt = seq // tl
    row = lambda a: a.reshape(1, -1)
    full = lambda shape: pl.BlockSpec(shape, lambda b, l: (0,) * len(shape))
    return pl.pallas_call(
        functools.partial(_conv_kernel, alpha=alpha, width=width),
        grid=(bsz, nlt),
        in_specs=[pl.BlockSpec((tl, d), lambda b, l: (b * nlt + l, 0)),
                  pl.BlockSpec((None, 6, d), lambda b, l: (b, 0, 0)),
                  full((d, 2 * d)), full((1, 2 * d)), full((width, d)), full((1, d)),
                  full((1, d)), full((1, d)), full((d, d)), full((1, d)),
                  full((1, d)), full((1, d)), full((e, d)), full((e, 1))],
        out_specs=[pl.BlockSpec((tl, d), lambda b, l: (b * nlt + l, 0)),
                   pl.BlockSpec((tl * p, LANES), lambda b, l: (b * nlt + l, 0)),
                   pl.BlockSpec((e, tl), lambda b, l: (0, b * nlt + l))],
        out_shape=[jax.ShapeDtypeStruct((t, d), F32),
                   jax.ShapeDtypeStruct((t * p, LANES), jnp.uint32),
                   jax.ShapeDtypeStruct((e, t), F32)],
        scratch_shapes=[pltpu.VMEM((tl + CONV_HALO, d), F32),
                        pltpu.VMEM((7, tl + CONV_HALO - 8, d), F32),
                        pltpu.VMEM((width, 8, d), F32),
                        pltpu.VMEM((tl, d), F32)],
        compiler_params=_cparams(2),
        name="conv_mixer",
    )(x2d, mod, w1, row(b1), dw, row(db), row(clg), row(clb), w2, row(b2),
      row(lng), row(lnb), rwt, rb.reshape(e, 1))


def _route_kernel(logt_ref, slot_ref, gate_ref, cnt_ref, *, p, stage_rows):
    e, tt = logt_ref.shape
    l = logt_ref[...]
    iota_e = lax.broadcasted_iota(jnp.int32, (e, tt), 0).astype(F32)
    sels, vals = [], []
    for _ in range(TOP_K):
        m = jnp.max(l, axis=0, keepdims=True)
        idx = jnp.min(jnp.where(l == m, iota_e, float(e)), axis=0, keepdims=True)
        sels.append(idx)
        vals.append(m)
        l = jnp.where(iota_e == idx, -jnp.inf, l)
    exps = [jnp.exp(v - vals[0]) for v in vals]
    denom = exps[0] + exps[1] + exps[2] + exps[3]
    onehots = [iota_e == s for s in sels]
    member = jnp.zeros((e, tt), F32)
    for oh in onehots:
        member = member + jnp.where(oh, 1.0, 0.0)
    tri = jnp.where(lax.broadcasted_iota(jnp.int32, (tt, tt), 0)
                    < lax.broadcasted_iota(jnp.int32, (tt, tt), 1), 1.0, 0.0).astype(BF16)
    rank = jnp.dot(member.astype(BF16), tri, preferred_element_type=F32)
    cnt = jnp.sum(member, axis=1, keepdims=True)
    padded = jnp.floor((cnt + (CHUNK - 1)) * (1.0 / CHUNK)) * CHUNK
    low = jnp.where(lax.broadcasted_iota(jnp.int32, (e, e), 1)
                    < lax.broadcasted_iota(jnp.int32, (e, e), 0), 1.0, 0.0)
    loff = jnp.dot(low, jnp.broadcast_to(padded, (e, tt)), precision=HIGHEST,
                   preferred_element_type=F32)
    pos = loff + rank + (pl.program_id(0) % 2 * stage_rows).astype(F32)
    for k in range(TOP_K):
        slot = jnp.sum(jnp.where(onehots[k], pos, 0.0), axis=0, keepdims=True)
        slot_ref[0, k:k + 1, :] = (slot * p).astype(jnp.int32)
        gate_ref[0, k:k + 1, :] = exps[k] / denom
    cnt_ref[0] = jnp.broadcast_to(cnt, (e, LANES)).astype(jnp.int32)


def _route(logt, tt, p, stage_rows):
    e, t = logt.shape
    nt = t // tt
    return pl.pallas_call(
        functools.partial(_route_kernel, p=p, stage_rows=stage_rows),
        grid=(nt,),
        in_specs=[pl.BlockSpec((e, tt), lambda i: (0, i))],
        out_specs=[pl.BlockSpec((1, TOP_K, tt), lambda i: (i, 0, 0)),
                   pl.BlockSpec((1, TOP_K, tt), lambda i: (i, 0, 0)),
                   pl.BlockSpec((1, e, LANES), lambda i: (i, 0, 0))],
        out_shape=[jax.ShapeDtypeStruct((nt, TOP_K, tt), jnp.int32),
                   jax.ShapeDtypeStruct((nt, TOP_K, tt), F32),
                   jax.ShapeDtypeStruct((nt, e, LANES), jnp.int32)],
        compiler_params=_cparams(1),
        name="route",
    )(logt)


def _plan(cnt, rb, nb, n_classes, p):
    nt, e = cnt.shape
    padded = ((cnt + CHUNK - 1) // CHUNK) * CHUNK
    loff = jnp.cumsum(padded, axis=1) - padded
    rows_e = jnp.sum(padded, axis=0)
    blocks_e = (rows_e + rb - 1) // rb
    bend = jnp.cumsum(blocks_e)
    pstart = (bend - blocks_e) * rb
    goff = pstart[None, :] + jnp.cumsum(padded, axis=0) - padded
    block_ids = jnp.arange(nb, dtype=bend.dtype)
    block_e = jnp.minimum(jnp.sum(bend[None, :] <= block_ids[:, None], axis=1), e - 1)
    chunks = padded // CHUNK
    classcnt = jnp.stack([jnp.sum((chunks >> k) & 1, axis=1) for k in range(n_classes)], axis=1)
    tail_start = pstart + rows_e
    tail_chunks = (blocks_e * rb - rows_e) // CHUNK
    i32 = lambda a: a.astype(jnp.int32)
    return (i32(chunks.reshape(-1)), i32(loff.reshape(-1) * p), i32(goff.reshape(-1) * p),
            i32(classcnt.reshape(-1)), i32(tail_start), i32(tail_chunks), i32(block_e), i32(bend[-1:]))


def _size_classes(tt):
    return (tt // CHUNK).bit_length()


def _start_group_dmas(chunks_ref, loff_ref, goff_ref, tile, n_experts, p, n_classes, make):
    unroll = 4
    assert n_experts % unroll == 0
    rows = CHUNK * p

    def per_experts(g, carry):
        for u in range(unroll):
            idx = tile * n_experts + g * unroll + u
            n = chunks_ref[idx]
            lo = loff_ref[idx]
            go = goff_ref[idx]
            for k in range(n_classes):
                @pl.when(n & (1 << k) != 0)
                def _():
                    done = (n & ~((2 << k) - 1)) * rows
                    make(pl.multiple_of(lo + done, rows), pl.multiple_of(go + done, rows),
                         rows << k).start()
        return carry

    lax.fori_loop(0, n_experts // unroll, per_experts, 0)


def _wait_group_dmas(classcnt_ref, tile, p, n_classes, make):
    for k in range(n_classes):
        def wait_one(j, c, k=k):
            make(0, 0, (CHUNK << k) * p).wait()
            return c

        lax.fori_loop(0, classcnt_ref[tile * n_classes + k], wait_one, 0)


def _dispatch_kernel(padded_ref, loff_ref, goff_ref, classcnt_ref, tstart_ref, tchunks_ref,
                     x_ref, slot_ref, xs_ref, stage, sem, *, n_experts):
    tile = pl.program_id(0)
    n_tiles = pl.num_programs(0)
    tt = slot_ref.shape[0] // TOP_K
    p = x_ref.shape[0] // tt
    buf_rows = stage.shape[0] // 2
    n_classes = _size_classes(tt)
    buf = tile % 2

    def make_for(which_buf):
        base = pl.multiple_of(which_buf * buf_rows, CHUNK * p)
        return lambda lrow, grow, nrows: pltpu.make_async_copy(
            stage.at[pl.ds(base + lrow, nrows), :], xs_ref.at[pl.ds(grow, nrows), :],
            sem.at[which_buf])

    @pl.when(tile < 2)
    def _():
        start = pl.multiple_of(buf * buf_rows, CHUNK * p)
        stage[pl.ds(start, buf_rows), :] = jnp.zeros((buf_rows, LANES), stage.dtype)

    @pl.when(tile >= 2)
    def _():
        _wait_group_dmas(classcnt_ref, tile - 2, p, n_classes, make_for(buf))

    @pl.when(tile == 0)
    def _():
        rows = CHUNK * p
        tail = lambda grow: pltpu.make_async_copy(
            stage.at[pl.ds(0, rows), :], xs_ref.at[pl.ds(grow, rows), :], sem.at[0])

        def per_expert(ex, total):
            n = tchunks_ref[ex]

            def per_chunk(j, c):
                tail(pl.multiple_of((tstart_ref[ex] + j * CHUNK) * p, rows)).start()
                return c

            lax.fori_loop(0, n, per_chunk, 0)
            return total + n

        total = lax.fori_loop(0, n_experts, per_expert, 0)

        def wait_tail(j, c):
            tail(0).wait()
            return c

        lax.fori_loop(0, total, wait_tail, 0)

    unroll = 8

    def per_tokens(g, carry):
        for u in range(unroll):
            t = g * unroll + u
            v = x_ref[pl.ds(pl.multiple_of(t * p, p), p), :]
            for k in range(TOP_K):
                s = slot_ref[k * tt + t]
                stage[pl.ds(pl.multiple_of(s, p), p), :] = v
        return carry

    lax.fori_loop(0, tt // unroll, per_tokens, 0)
    _start_group_dmas(padded_ref, loff_ref, goff_ref, tile, n_experts, p, n_classes,
                      make_for(buf))

    @pl.when(tile == n_tiles - 1)
    def _():
        @pl.when(tile >= 1)
        def _():
            _wait_group_dmas(classcnt_ref, tile - 1, p, n_classes, make_for(1 - buf))

        _wait_group_dmas(classcnt_ref, tile, p, n_classes, make_for(buf))


def _stage_rows(tt, n_experts):
    rows = TOP_K * tt + n_experts * (CHUNK - 1)
    return -(-rows // CHUNK) * CHUNK


def _dispatch(hlin, slots, tables, tails, *, n_rows, n_experts, d, tt):
    padded, loff, goff, classcnt = tables
    tstart, tchunks = tails
    nt = slots.shape[0] // (TOP_K * tt)
    p = d // (2 * LANES)
    return pl.pallas_call(
        functools.partial(_dispatch_kernel, n_experts=n_experts),
        grid_spec=pltpu.PrefetchScalarGridSpec(
            num_scalar_prefetch=6,
            grid=(nt,),
            in_specs=[pl.BlockSpec((tt * p, LANES), lambda i, *_: (i, 0)),
                      pl.BlockSpec((TOP_K * tt,), lambda i, *_: (i,),
                                   memory_space=pltpu.SMEM)],
            out_specs=pl.BlockSpec(memory_space=pl.ANY),
            scratch_shapes=[pltpu.VMEM((2 * _stage_rows(tt, n_experts) * p, LANES), jnp.uint32),
                            pltpu.SemaphoreType.DMA((2,))]),
        out_shape=jax.ShapeDtypeStruct((n_rows * p, LANES), jnp.uint32),
        compiler_params=_cparams(1),
        name="dispatch",
    )(padded, loff, goff, classcnt, tstart, tchunks, hlin, slots)


def _expert_kernel(be_ref, nu_ref, x_ref, win_ref, bin_ref, wout_ref, bout_ref, y_ref,
                   win_bf, wout_bf):
    b = pl.program_id(0)
    d = win_ref.shape[1]
    f = wout_ref.shape[1]
    p = d // (2 * LANES)
    rb = x_ref.shape[0] // p

    @pl.when(b < nu_ref[0])
    def _():
        prev = be_ref[jnp.maximum(b - 1, 0)]

        @pl.when(jnp.logical_or(b == 0, be_ref[b] != prev))
        def _():
            win_bf[...] = win_ref[0].astype(BF16)
            wout_bf[...] = wout_ref[0].astype(BF16)

        halves = [h for s in range(p) for h in _unpack_words(x_ref[pl.ds(s, rb, stride=p), :])]
        x = jnp.concatenate(halves, axis=-1)
        u = jnp.dot(x.astype(BF16), win_bf[...], preferred_element_type=F32) + bin_ref[0]
        x_glu = jnp.minimum(u[:, :f], SWIGLU_LIMIT)
        x_lin = jnp.clip(u[:, f:], -SWIGLU_LIMIT, SWIGLU_LIMIT)
        act = (x_lin + 1.0) * (x_glu * jax.nn.sigmoid(SWIGLU_ALPHA * x_glu))
        y = jnp.dot(act.astype(BF16), wout_bf[...], preferred_element_type=F32) + bout_ref[0]
        for s, w in enumerate(_pack_rows(y)):
            y_ref[pl.ds(s, rb, stride=p), :] = w


def _experts(xs, block_e, n_used, w_in, b_in, w_out, b_out, *, layer, rb, nb):
    _, e, d, f2 = w_in.shape
    f = f2 // 2
    p = d // (2 * LANES)
    last = lambda b, nu: jnp.maximum(jnp.minimum(b, nu[0] - 1), 0)
    blk = lambda b, be, nu: (last(b, nu), 0)
    wsel = lambda b, be, nu: (layer, be[last(b, nu)], 0, 0)
    return pl.pallas_call(
        _expert_kernel,
        grid_spec=pltpu.PrefetchScalarGridSpec(
            num_scalar_prefetch=2,
            grid=(nb,),
            in_specs=[pl.BlockSpec((rb * p, LANES), blk),
                      pl.BlockSpec((None, 1, d, f2), wsel),
                      pl.BlockSpec((None, 1, 1, f2), wsel),
                      pl.BlockSpec((None, 1, f, d), wsel),
                      pl.BlockSpec((None, 1, 1, d), wsel)],
            out_specs=pl.BlockSpec((rb * p, LANES), blk),
            scratch_shapes=[pltpu.VMEM((d, f2), BF16), pltpu.VMEM((f, d), BF16)]),
        out_shape=jax.ShapeDtypeStruct(xs.shape, xs.dtype),
        compiler_params=_cparams(1),
        name="experts",
    )(block_e, n_used, xs, w_in, b_in.reshape(-1, e, 1, f2), w_out, b_out.reshape(-1, e, 1, d))


def _combine_kernel(padded_ref, loff_ref, goff_ref, classcnt_ref, ys_ref, slot_ref, gate_ref,
                    xres_ref, mod_ref, lng_ref, lnb_ref, x_out_ref, stage, comb_even, comb_odd, sem,
                    *, n_experts, alpha):
    tile = pl.program_id(0)
    n_tiles = pl.num_programs(0)
    tt, d = xres_ref.shape
    p = d // (2 * LANES)
    buf_rows = stage.shape[0] // 2
    n_classes = _size_classes(tt)
    buf = tile % 2

    def make_for(which_buf):
        base = pl.multiple_of(which_buf * buf_rows, CHUNK * p)
        return lambda lrow, grow, nrows: pltpu.make_async_copy(
            ys_ref.at[pl.ds(grow, nrows), :], stage.at[pl.ds(base + lrow, nrows), :],
            sem.at[which_buf])

    def start_tile(which_tile, which_buf):
        _start_group_dmas(padded_ref, loff_ref, goff_ref, which_tile, n_experts, p, n_classes,
                          make_for(which_buf))

    @pl.when(tile == 0)
    def _():
        start_tile(tile, buf)

    @pl.when(tile + 1 < n_tiles)
    def _():
        start_tile(tile + 1, 1 - buf)

    _wait_group_dmas(classcnt_ref, tile, p, n_classes, make_for(buf))
    unroll = 4

    def per_tokens(g, carry):
        for u in range(unroll):
            t = g * unroll + u
            acc = None
            for k in range(TOP_K):
                s = slot_ref[k * tt + t]
                gate = gate_ref[k * tt + t]
                lo, hi = _unpack_words(stage[pl.ds(pl.multiple_of(s, p), p), :])
                lo, hi = gate * lo, gate * hi
                acc = (lo, hi) if acc is None else (acc[0] + lo, acc[1] + hi)
            row = pl.ds(pl.multiple_of(t * p, p), p)
            comb_even[row, :] = acc[0]
            comb_odd[row, :] = acc[1]
        return carry

    lax.fori_loop(0, tt // unroll, per_tokens, 0)
    moe = jnp.concatenate([(comb_odd if c % 2 else comb_even)[pl.ds(c // 2, tt, stride=p), :]
                           for c in range(2 * p)], axis=-1)
    g2 = mod_ref[5:6, :]
    x_out_ref[...] = _layer_norm(alpha * xres_ref[...] + (1.0 + g2) * moe,
                                 lng_ref[...], lnb_ref[...])


def _combine(ys, slots, gates, tables, xres, mod, lng, lnb, *, n_experts, seq, tt, alpha):
    padded, loff, goff, classcnt = tables
    t, d = xres.shape
    nt = t // tt
    p = d // (2 * LANES)
    tiles_per_seq = seq // tt
    smem = lambda: pl.BlockSpec((TOP_K * tt,), lambda i, *_: (i,), memory_space=pltpu.SMEM)
    return pl.pallas_call(
        functools.partial(_combine_kernel, n_experts=n_experts, alpha=alpha),
        grid_spec=pltpu.PrefetchScalarGridSpec(
            num_scalar_prefetch=4,
            grid=(nt,),
            in_specs=[pl.BlockSpec(memory_space=pl.ANY), smem(), smem(),
                      pl.BlockSpec((tt, d), lambda i, *_: (i, 0)),
                      pl.BlockSpec((None, 6, d), lambda i, *_: (i // tiles_per_seq, 0, 0)),
                      pl.BlockSpec((1, d), lambda i, *_: (0, 0)),
                      pl.BlockSpec((1, d), lambda i, *_: (0, 0))],
            out_specs=pl.BlockSpec((tt, d), lambda i, *_: (i, 0)),
            scratch_shapes=[pltpu.VMEM((2 * _stage_rows(tt, n_experts) * p, LANES), jnp.uint32),
                            pltpu.VMEM((tt * p, LANES), F32), pltpu.VMEM((tt * p, LANES), F32),
                            pltpu.SemaphoreType.DMA((2,))]),
        out_shape=jax.ShapeDtypeStruct((t, d), F32),
        compiler_params=_cparams(1),
        name="combine",
    )(padded, loff, goff, classcnt, ys, slots, gates, xres, mod, lng.reshape(1, d), lnb.reshape(1, d))


def _moe(hlin, logt, xres, mod, lng, lnb, w_in, b_in, w_out, b_out, *, layer, seq, tt, rb,
         alpha):
    e, t = logt.shape
    d = xres.shape[1]
    nt = t // tt
    p = d // (2 * LANES)
    slots, gates, cnt = _route(logt, tt, p, _stage_rows(tt, e))
    slots = slots.reshape(-1)
    gates = gates.reshape(-1)
    max_rows = TOP_K * t + nt * e * (CHUNK - 1)
    nb = -(-max_rows // rb) + e
    padded, loff, goff, classcnt, tstart, tchunks, block_e, n_used = _plan(
        cnt[:, :, 0], rb, nb, _size_classes(tt), p)
    tables = (padded, loff, goff, classcnt)
    xs = _dispatch(hlin, slots, tables, (tstart, tchunks), n_rows=nb * rb, n_experts=e, d=d,
                   tt=tt)
    ys = _experts(xs, block_e, n_used, w_in, b_in, w_out, b_out, layer=layer, rb=rb, nb=nb)
    return _combine(ys, slots, gates, tables, xres, mod, lng, lnb,
                    n_experts=e, seq=seq, tt=tt, alpha=alpha)


def _qkv_kernel(x_ref, mod_ref, qwt_ref, kw_ref, vwt_ref, qt_ref, k_ref, vt_ref):
    x = x_ref[...]
    sh1 = mod_ref[0:1, :]
    sc1 = mod_ref[1:2, :]
    h = (x * (1.0 + sc1) + sh1).astype(BF16)
    nt_dims = (((1,), (1,)), ((), ()))
    qt = lax.dot_general(qwt_ref[...], h, nt_dims, preferred_element_type=F32).astype(BF16)
    tq = qt_ref.shape[3]
    for j in range(qt_ref.shape[1]):
        qt_ref[0, j] = qt[:, j * tq:(j + 1) * tq]
    tk = vt_ref.shape[3]
    ngrp = tk // 8
    r = lax.broadcasted_iota(jnp.int32, (tk, tk), 0)
    s = lax.broadcasted_iota(jnp.int32, (tk, tk), 1)
    perm = jnp.where(s == (r & 7) * ngrp + (r >> 3), 1.0, 0.0).astype(BF16)
    xb = x.astype(BF16)
    xp = jnp.concatenate(
        [jnp.dot(perm, xb[j * tk:(j + 1) * tk, :], preferred_element_type=F32)
         for j in range(vt_ref.shape[1])], axis=0).astype(BF16)
    k_ref[...] = jnp.dot(xp, kw_ref[...], preferred_element_type=F32).astype(BF16)
    vt = lax.dot_general(vwt_ref[...], xp, nt_dims, preferred_element_type=F32).astype(BF16)
    for j in range(vt_ref.shape[1]):
        vt_ref[0, j] = vt[:, j * tk:(j + 1) * tk]


def _qkv(x2d, mod, qwt, kw, vwt, *, bsz, seq, tl, tk):
    t, d = x2d.shape
    nlt = seq // tl
    full = pl.BlockSpec((d, d), lambda b, l: (0, 0))
    return pl.pallas_call(
        _qkv_kernel,
        grid=(bsz, nlt),
        in_specs=[pl.BlockSpec((tl, d), lambda b, l: (b * nlt + l, 0)),
                  pl.BlockSpec((None, 6, d), lambda b, l: (b, 0, 0)),
                  full, full, full],
        out_specs=[pl.BlockSpec((1, tl // tk, d, tk), lambda b, l: (b, l, 0, 0)),
                   pl.BlockSpec((tl, d), lambda b, l: (b * nlt + l, 0)),
                   pl.BlockSpec((1, tl // tk, d, tk), lambda b, l: (b, l, 0, 0))],
        out_shape=[jax.ShapeDtypeStruct((bsz, seq // tk, d, tk), BF16),
                   jax.ShapeDtypeStruct((t, d), BF16),
                   jax.ShapeDtypeStruct((bsz, seq // tk, d, tk), BF16)],
        compiler_params=_cparams(2),
        name="qkv",
    )(x2d, mod, qwt, kw, vwt)


def _stick_block(z_ref, a_ref, carry, masked):
    tk, tq = z_ref.shape
    ngrp = tk // 8
    sub = lax.broadcasted_iota(jnp.int32, (8, tq), 0)
    lane = lax.broadcasted_iota(jnp.int32, (8, tq), 1)
    causal = lambda g: sub * ngrp + g < lane
    run = None
    for g in reversed(range(ngrp)):
        z = z_ref[g * 8:(g + 1) * 8, :]
        rest = 1.0 / (1.0 + jnp.exp2(z))
        beta = 1.0 - rest
        if masked:
            rest = jnp.where(causal(g), rest, 1.0)
            beta = jnp.where(causal(g), beta, 0.0)
        z_ref[g * 8:(g + 1) * 8, :] = beta if run is None else beta * run
        run = rest if run is None else run * rest
    s = run
    for k in (1, 2, 4):
        s = s * jnp.where(sub < 8 - k, pltpu.roll(s, 8 - k, axis=0), 1.0)
    later = jnp.where(sub < 7, pltpu.roll(s, 7, axis=0), 1.0) * carry
    for g in range(0, ngrp, 2):
        a_ref[g * 8:(g + 2) * 8, :] = jnp.concatenate(
            [z_ref[gg * 8:(gg + 1) * 8, :] * later for gg in (g, g + 1)], axis=0).astype(BF16)
    return carry * jnp.broadcast_to(s[0:1, :], (8, tq))


def _attn_kernel(qt_ref, k_ref, vt_ref, o_ref, z0, z1, z2, a0, a1, acc_buf, qm_buf, carry_buf,
                 *, head_dim):
    nq, _, tq = qt_ref.shape[1:]
    tk = vt_ref.shape[3]
    hp = qt_ref.shape[2] // head_dim
    row = lax.broadcasted_iota(jnp.int32, (hp * head_dim, tq), 0)

    def set_queries(qi):
        qf = qt_ref[0, qi].astype(F32)
        for h in range(hp):
            in_head = jnp.logical_and(row >= h * head_dim, row < (h + 1) * head_dim)
            qm_buf[h] = jnp.where(in_head, qf, 0.0).astype(BF16)

    def scores(kb, z_out):
        kblk = k_ref[pl.ds(pl.multiple_of(kb * tk, tk), tk), :]
        for h in range(hp):
            z_out[h] = jnp.dot(kblk, qm_buf[h], preferred_element_type=F32)

    def weighted_values(kb, a_in):
        return [jnp.dot(vt_ref[0, kb, h * head_dim:(h + 1) * head_dim, :], a_in[h],
                        preferred_element_type=F32) for h in range(hp)]

    def diagonal(z_in, a_out):
        one = jnp.ones((8, tq), F32)
        for h in range(hp):
            carry_buf[h] = _stick_block(z_in.at[h], a_out.at[h], one, True)

    def open_next(qi):
        nxt = jnp.minimum(qi + 1, nq - 1)
        set_queries(nxt)
        scores(nxt, z2)

    def step(kb, z_in, z_out, a_in, a_out, carries, last_of=None):
        pv = weighted_values(kb + 1, a_in)
        if last_of is None:
            scores(jnp.maximum(kb - 1, 0), z_out)
        else:
            open_next(last_of)
        carries = tuple(_stick_block(z_in.at[h], a_out.at[h], carries[h], False)
                        for h in range(hp))
        for h in range(hp):
            acc_buf[h] += pv[h]
        return carries

    def finish(qi, a_last):
        nxt = jnp.minimum(qi + 1, nq - 1)
        pv = weighted_values(0, a_last)
        scores(jnp.maximum(nxt - 1, 0), z1)
        out = jnp.concatenate([(acc_buf[h] + pv[h]).T for h in range(hp)], axis=-1)
        o_ref[pl.ds(pl.multiple_of(qi * tq, tq), tq), :] = out.astype(o_ref.dtype)
        diagonal(z2, a0)
        acc_buf[...] = jnp.zeros(acc_buf.shape, F32)

    set_queries(0)
    scores(0, z2)
    diagonal(z2, a0)
    acc_buf[...] = jnp.zeros(acc_buf.shape, F32)
    open_next(0)
    finish(0, a0)

    def query_tile(qi, c):
        def pair(i, carries):
            kb = qi - 1 - 2 * i
            carries = step(kb, z1, z0, a0, a1, carries)
            return step(kb - 1, z0, z1, a1, a0, carries)

        carries = lax.fori_loop(0, (qi - 1) // 2, pair,
                                tuple(carry_buf[h] for h in range(hp)))

        @pl.when(qi % 2 == 1)
        def _():
            step(0, z1, z0, a0, a1, carries, last_of=qi)
            finish(qi, a1)

        @pl.when(qi % 2 == 0)
        def _():
            mid = step(1, z1, z0, a0, a1, carries)
            step(0, z0, z1, a1, a0, mid, last_of=qi)
            finish(qi, a0)

        return c

    lax.fori_loop(1, nq, query_tile, 0)


def _attention(qt, k2d, vt, *, bsz, seq):
    _, nq, d, tq = qt.shape
    head_dim = d // N_HEADS
    tk = vt.shape[3]
    assert tk == tq, "the diagonal mask assumes square blocks"
    ncol = d // LANES
    hp = LANES // head_dim
    return pl.pallas_call(
        functools.partial(_attn_kernel, head_dim=head_dim),
        grid=(bsz, ncol),
        in_specs=[pl.BlockSpec((1, nq, LANES, tq), lambda b, c: (b, 0, c, 0)),
                  pl.BlockSpec((seq, LANES), lambda b, c: (b, c)),
                  pl.BlockSpec((1, seq // tk, LANES, tk), lambda b, c: (b, 0, c, 0))],
        out_specs=pl.BlockSpec((seq, LANES), lambda b, c: (b, c)),
        out_shape=jax.ShapeDtypeStruct((bsz * seq, d), BF16),
        scratch_shapes=[pltpu.VMEM((hp, tk, tq), F32), pltpu.VMEM((hp, tk, tq), F32),
                        pltpu.VMEM((hp, tk, tq), F32),
                        pltpu.VMEM((hp, tk, tq), BF16), pltpu.VMEM((hp, tk, tq), BF16),
                        pltpu.VMEM((hp, head_dim, tq), F32),
                        pltpu.VMEM((hp, hp * head_dim, tq), BF16),
                        pltpu.VMEM((hp, 8, tq), F32)],
        compiler_params=_cparams(2),
        name="attention",
    )(qt, k2d, vt)


def _oproj_kernel(o_ref, xres_ref, mod_ref, ow_ref, lng_ref, lnb_ref, rwt_ref, rb_ref,
                  x_out_ref, hlin_ref, logt_ref, *, alpha):
    y = jnp.dot(o_ref[...], ow_ref[...], preferred_element_type=F32)
    _post_norm_route(xres_ref[...], y, mod_ref, lng_ref, lnb_ref, rwt_ref, rb_ref,
                     x_out_ref, hlin_ref, logt_ref, alpha)


def _oproj(o2d, xres, mod, ow, lng, lnb, rwt, rb, *, seq, tl, alpha):
    t, d = xres.shape
    e = rwt.shape[0]
    p = d // (2 * LANES)
    nlt = seq // tl
    full = lambda shape: pl.BlockSpec(shape, lambda i: (0,) * len(shape))
    return pl.pallas_call(
        functools.partial(_oproj_kernel, alpha=alpha),
        grid=(t // tl,),
        in_specs=[pl.BlockSpec((tl, d), lambda i: (i, 0)),
                  pl.BlockSpec((tl, d), lambda i: (i, 0)),
                  pl.BlockSpec((None, 6, d), lambda i: (i // nlt, 0, 0)),
                  full((d, d)), full((1, d)), full((1, d)), full((e, d)), full((e, 1))],
        out_specs=[pl.BlockSpec((tl, d), lambda i: (i, 0)),
                   pl.BlockSpec((tl * p, LANES), lambda i: (i, 0)),
                   pl.BlockSpec((e, tl), lambda i: (0, i))],
        out_shape=[jax.ShapeDtypeStruct((t, d), F32),
                   jax.ShapeDtypeStruct((t * p, LANES), jnp.uint32),
                   jax.ShapeDtypeStruct((e, t), F32)],
        compiler_params=_cparams(1),
        name="oproj",
    )(o2d, xres, mod, ow, lng.reshape(1, d), lnb.reshape(1, d), rwt, rb.reshape(e, 1))


def kernel(x, c, ada_w, ada_b, ln_g, ln_b, cv_w1, cv_b1, cv_dw, cv_db, cv_ln_g, cv_ln_b,
           cv_w2, cv_b2, kv_w, q_w, o_w, router_w, router_b, moe_w_in, moe_b_in,
           moe_w_out, moe_b_out):
    bsz, seq, d = x.shape
    depth = ada_w.shape[0]
    n_a = cv_w1.shape[0]
    assert depth == 2 and n_a == 1 and q_w.shape[0] == 1, "one conv layer then one attention layer"
    alpha = (2.0 * depth) ** 0.25
    head_dim = d // N_HEADS
    tl = min(512, seq)
    tt = min(1024, seq)
    rb = 512
    tq = min(256, seq)

    mod = _ada(c, ada_w, ada_b).reshape(depth, bsz, 6, d)
    x0 = x.reshape(bsz * seq, d)
    rwt = jnp.swapaxes(router_w, 1, 2)

    x1, hlin, logt = _conv_mixer(
        x0, mod[0], cv_w1[0].astype(BF16), cv_b1[0], cv_dw[0], cv_db[0], cv_ln_g[0], cv_ln_b[0],
        cv_w2[0].astype(BF16), cv_b2[0], ln_g[0, 0], ln_b[0, 0], rwt[0], router_b[0],
        bsz=bsz, seq=seq, tl=tl, alpha=alpha)
    x2 = _moe(hlin, logt, x1, mod[0], ln_g[0, 1], ln_b[0, 1], moe_w_in, moe_b_in,
              moe_w_out, moe_b_out, layer=0, seq=seq, tt=tt, rb=rb, alpha=alpha)

    scale = head_dim ** -0.5
    qwt = (q_w[0] * (scale * LOG2E)).T.astype(BF16)
    kw = kv_w[:, :d].astype(BF16)
    vwt = kv_w[:, d:].T.astype(BF16)
    qt, k2d, vt = _qkv(x2, mod[1], qwt, kw, vwt, bsz=bsz, seq=seq, tl=tl, tk=tq)
    o2d = _attention(qt, k2d, vt, bsz=bsz, seq=seq)
    x3, hlin, logt = _oproj(o2d, x2, mod[1], o_w[0].astype(BF16), ln_g[1, 0], ln_b[1, 0],
                            rwt[1], router_b[1], seq=seq, tl=tl, alpha=alpha)
    x4 = _moe(hlin, logt, x3, mod[1], ln_g[1, 1], ln_b[1, 1], moe_w_in, moe_b_in,
              moe_w_out, moe_b_out, layer=1, seq=seq, tt=tt, rb=rb, alpha=alpha)
    return x4.reshape(bsz, seq, d)
```

```python
import functools

import jax
import jax.numpy as jnp
from jax import lax
from jax.experimental import pallas as pl
from jax.experimental.pallas import tpu as pltpu

F32 = jnp.float32
BF16 = jnp.bfloat16
HIGHEST = lax.Precision.HIGHEST

LANES = 128
N_HEADS = 16
TOP_K = 4
CONV_HALO = 32
SWIGLU_LIMIT = 7.0
SWIGLU_ALPHA = 1.702
LN_EPS = 1e-5
LOG2E = 1.4426950408889634
CHUNK = 8
VMEM_LIMIT = 56 * 1024 * 1024


def _cparams(n_axes):
    return pltpu.CompilerParams(
        dimension_semantics=("arbitrary",) * n_axes, vmem_limit_bytes=VMEM_LIMIT)


def _layer_norm(x, g, b):
    mu = jnp.mean(x, axis=-1, keepdims=True)
    xc = x - mu
    var = jnp.mean(xc * xc, axis=-1, keepdims=True)
    return xc * lax.rsqrt(var + LN_EPS) * g + b


def _silu(x):
    return x * jax.nn.sigmoid(x)


def _bf16_bits(a):
    return lax.bitcast_convert_type(a.astype(BF16).astype(F32), jnp.uint32)


def _pack_rows(x):
    pieces = []
    for s in range(x.shape[1] // (2 * LANES)):
        lo = _bf16_bits(x[:, 2 * s * LANES:(2 * s + 1) * LANES])
        hi = _bf16_bits(x[:, (2 * s + 1) * LANES:(2 * s + 2) * LANES])
        pieces.append((hi & jnp.uint32(0xFFFF0000)) | (lo >> 16))
    return pieces


def _unpack_words(w):
    return (lax.bitcast_convert_type(w << 16, F32),
            lax.bitcast_convert_type(w & jnp.uint32(0xFFFF0000), F32))


def _ada_kernel(c_ref, w_ref, b_ref, o_ref):
    o_ref[0] = jnp.dot(_silu(c_ref[...]), w_ref[0], precision=HIGHEST,
                       preferred_element_type=F32) + b_ref[0]


def _ada(c, ada_w, ada_b):
    depth, d, n = ada_w.shape
    bsz = c.shape[0]
    tn = d
    return pl.pallas_call(
        _ada_kernel,
        grid=(depth, n // tn),
        in_specs=[pl.BlockSpec((bsz, d), lambda l, j: (0, 0)),
                  pl.BlockSpec((1, d, tn), lambda l, j: (l, 0, j)),
                  pl.BlockSpec((1, 1, tn), lambda l, j: (l, 0, j))],
        out_specs=pl.BlockSpec((1, bsz, tn), lambda l, j: (l, 0, j)),
        out_shape=jax.ShapeDtypeStruct((depth, bsz, n), F32),
        compiler_params=_cparams(2),
        name="ada",
    )(c, ada_w, ada_b.reshape(depth, 1, n))


def _post_norm_route(xres, y, mod_ref, lng_ref, lnb_ref, rwt_ref, rb_ref,
                     x_out_ref, hlin_ref, logt_ref, alpha):
    tl, d = xres.shape
    g1 = mod_ref[2:3, :]
    sh2 = mod_ref[3:4, :]
    sc2 = mod_ref[4:5, :]
    x1 = _layer_norm(alpha * xres + (1.0 + g1) * y, lng_ref[...], lnb_ref[...])
    x_out_ref[...] = x1
    h2 = x1 * (1.0 + sc2) + sh2
    pieces = _pack_rows(h2)
    for s, w in enumerate(pieces):
        hlin_ref[pl.ds(s, tl, stride=len(pieces)), :] = w
    logt_ref[...] = lax.dot_general(
        rwt_ref[...], h2, (((1,), (1,)), ((), ())), precision=HIGHEST,
        preferred_element_type=F32) + rb_ref[...]


def _conv_kernel(x_ref, mod_ref, w1_ref, b1_ref, dw_ref, db_ref, clg_ref, clb_ref,
                 w2_ref, b2_ref, lng_ref, lnb_ref, rwt_ref, rb_ref,
                 x_out_ref, hlin_ref, logt_ref,
                 ubuf, sbuf, dwb, cbuf, *, alpha, width):
    tl, d = x_ref.shape
    first_tile = pl.program_id(1) == 0

    @pl.when(jnp.logical_and(pl.program_id(0) == 0, first_tile))
    def _():
        for j in range(width):
            dwb[j] = jnp.broadcast_to(dw_ref[j:j + 1, :], (8, d))

    x = x_ref[...]
    sh1 = mod_ref[0:1, :]
    sc1 = mod_ref[1:2, :]
    h = (x * (1.0 + sc1) + sh1).astype(BF16)
    u = jnp.dot(h, w1_ref[...], preferred_element_type=F32) + b1_ref[...]
    glu = u[:, :d] * jax.nn.sigmoid(u[:, d:])

    @pl.when(first_tile)
    def _():
        ubuf[0:CONV_HALO, :] = jnp.zeros((CONV_HALO, d), F32)

    ubuf[CONV_HALO:CONV_HALO + tl, :] = glu
    for r in range(1, 8):
        sbuf[r - 1] = ubuf[pl.ds(r, tl + CONV_HALO - 8), :]

    off0 = CONV_HALO - (width - 1)
    cw = 512
    rows = 32

    def body(i, carry):
        base = pl.multiple_of(i * rows, rows)
        for c in range(d // cw):
            cs = slice(c * cw, (c + 1) * cw)
            accs = [jnp.broadcast_to(db_ref[:, cs], (8, cw)) for _ in range(rows // 8)]
            for j in range(width):
                q, r = divmod(off0 + j, 8)
                w = dwb[j, :, cs]
                for a in range(rows // 8):
                    start = base + 8 * (q + a)
                    if r == 0:
                        v = ubuf[pl.ds(start, 8), cs]
                    else:
                        v = sbuf[r - 1, pl.ds(start, 8), cs]
                    accs[a] = accs[a] + w * v
            for a in range(rows // 8):
                cbuf[pl.ds(base + 8 * a, 8), cs] = accs[a]
        return carry

    lax.fori_loop(0, tl // rows, body, 0)
    ubuf[0:CONV_HALO, :] = ubuf[tl:tl + CONV_HALO, :]

    v = _silu(_layer_norm(cbuf[...], clg_ref[...], clb_ref[...]))
    y = jnp.dot(v.astype(BF16), w2_ref[...], preferred_element_type=F32) + b2_ref[...]
    _post_norm_route(x, y, mod_ref, lng_ref, lnb_ref, rwt_ref, rb_ref,
                     x_out_ref, hlin_ref, logt_ref, alpha)


def _conv_mixer(x2d, mod, w1, b1, dw, db, clg, clb, w2, b2, lng, lnb, rwt, rb,
                *, bsz, seq, tl, alpha):
    t, d = x2d.shape
    e = rwt.shape[0]
    width = dw.shape[0]
    p = d // (2 * LANES)
    nlt = seq // tl
    row = lambda a: a.reshape(1, -1)
    full = lambda shape: pl.BlockSpec(shape, lambda b, l: (0,) * len(shape))
    return pl.pallas_call(
        functools.partial(_conv_kernel, alpha=alpha, width=width),
        grid=(bsz, nlt),
        in_specs=[pl.BlockSpec((tl, d), lambda b, l: (b * nlt + l, 0)),
                  pl.BlockSpec((None, 6, d), lambda b, l: (b, 0, 0)),
                  full((d, 2 * d)), full((1, 2 * d)), full((width, d)), full((1, d)),
                  full((1, d)), full((1, d)), full((d, d)), full((1, d)),
                  full((1, d)), full((1, d)), full((e, d)), full((e, 1))],
        out_specs=[pl.BlockSpec((tl, d), lambda b, l: (b * nlt + l, 0)),
                   pl.BlockSpec((tl * p, LANES), lambda b, l: (b * nlt + l, 0)),
                   pl.BlockSpec((e, tl), lambda b, l: (0, b * nlt + l))],
        out_shape=[jax.ShapeDtypeStruct((t, d), F32),
                   jax.ShapeDtypeStruct((t * p, LANES), jnp.uint32),
                   jax.ShapeDtypeStruct((e, t), F32)],
        scratch_shapes=[pltpu.VMEM((tl + CONV_HALO, d), F32),
                        pltpu.VMEM((7, tl + CONV_HALO - 8, d), F32),
                        pltpu.VMEM((width, 8, d), F32),
                        pltpu.VMEM((tl, d), F32)],
        compiler_params=_cparams(2),
        name="conv_mixer",
    )(x2d, mod, w1, row(b1), dw, row(db), row(clg), row(clb), w2, row(b2),
      row(lng), row(lnb), rwt, rb.reshape(e, 1))


def _route_kernel(logt_ref, slot_ref, gate_ref, cnt_ref, *, p, stage_rows):
    e, tt = logt_ref.shape
    l = logt_ref[...]
    iota_e = lax.broadcasted_iota(jnp.int32, (e, tt), 0).astype(F32)
    sels, vals = [], []
    for _ in range(TOP_K):
        m = jnp.max(l, axis=0, keepdims=True)
        idx = jnp.min(jnp.where(l == m, iota_e, float(e)), axis=0, keepdims=True)
        sels.append(idx)
        vals.append(m)
        l = jnp.where(iota_e == idx, -jnp.inf, l)
    exps = [jnp.exp(v - vals[0]) for v in vals]
    denom = exps[0] + exps[1] + exps[2] + exps[3]
    onehots = [iota_e == s for s in sels]
    member = jnp.zeros((e, tt), F32)
    for oh in onehots:
        member = member + jnp.where(oh, 1.0, 0.0)
    tri = jnp.where(lax.broadcasted_iota(jnp.int32, (tt, tt), 0)
                    < lax.broadcasted_iota(jnp.int32, (tt, tt), 1), 1.0, 0.0).astype(BF16)
    rank = jnp.dot(member.astype(BF16), tri, preferred_element_type=F32)
    cnt = jnp.sum(member, axis=1, keepdims=True)
    padded = jnp.floor((cnt + (CHUNK - 1)) * (1.0 / CHUNK)) * CHUNK
    low = jnp.where(lax.broadcasted_iota(jnp.int32, (e, e), 1)
                    < lax.broadcasted_iota(jnp.int32, (e, e), 0), 1.0, 0.0)
    loff = jnp.dot(low, jnp.broadcast_to(padded, (e, tt)), precision=HIGHEST,
                   preferred_element_type=F32)
    pos = loff + rank + (pl.program_id(0) % 2 * stage_rows).astype(F32)
    for k in range(TOP_K):
        slot = jnp.sum(jnp.where(onehots[k], pos, 0.0), axis=0, keepdims=True)
        slot_ref[0, k:k + 1, :] = (slot * p).astype(jnp.int32)
        gate_ref[0, k:k + 1, :] = exps[k] / denom
    cnt_ref[0] = jnp.broadcast_to(cnt, (e, LANES)).astype(jnp.int32)


def _route(logt, tt, p, stage_rows):
    e, t = logt.shape
    nt = t // tt
    return pl.pallas_call(
        functools.partial(_route_kernel, p=p, stage_rows=stage_rows),
        grid=(nt,),
        in_specs=[pl.BlockSpec((e, tt), lambda i: (0, i))],
        out_specs=[pl.BlockSpec((1, TOP_K, tt), lambda i: (i, 0, 0)),
                   pl.BlockSpec((1, TOP_K, tt), lambda i: (i, 0, 0)),
                   pl.BlockSpec((1, e, LANES), lambda i: (i, 0, 0))],
        out_shape=[jax.ShapeDtypeStruct((nt, TOP_K, tt), jnp.int32),
                   jax.ShapeDtypeStruct((nt, TOP_K, tt), F32),
                   jax.ShapeDtypeStruct((nt, e, LANES), jnp.int32)],
        compiler_params=_cparams(1),
        name="route",
    )(logt)


def _plan(cnt, rb, nb, n_classes, p):
    nt, e = cnt.shape
    padded = ((cnt + CHUNK - 1) // CHUNK) * CHUNK
    loff = jnp.cumsum(padded, axis=1) - padded
    rows_e = jnp.sum(padded, axis=0)
    blocks_e = (rows_e + rb - 1) // rb
    bend = jnp.cumsum(blocks_e)
    pstart = (bend - blocks_e) * rb
    goff = pstart[None, :] + jnp.cumsum(padded, axis=0) - padded
    block_ids = jnp.arange(nb, dtype=bend.dtype)
    block_e = jnp.minimum(jnp.sum(bend[None, :] <= block_ids[:, None], axis=1), e - 1)
    chunks = padded // CHUNK
    classcnt = jnp.stack([jnp.sum((chunks >> k) & 1, axis=1) for k in range(n_classes)], axis=1)
    tail_start = pstart + rows_e
    tail_chunks = (blocks_e * rb - rows_e) // CHUNK
    i32 = lambda a: a.astype(jnp.int32)
    return (i32(chunks.reshape(-1)), i32(loff.reshape(-1) * p), i32(goff.reshape(-1) * p),
            i32(classcnt.reshape(-1)), i32(tail_start), i32(tail_chunks), i32(block_e),
            i32(bend[-1:]))


def _size_classes(tt):
    return (tt // CHUNK).bit_length()


def _start_group_dmas(chunks_ref, loff_ref, goff_ref, tile, n_experts, p, n_classes, make):
    unroll = 4
    assert n_experts % unroll == 0
    rows = CHUNK * p

    def per_experts(g, carry):
        for u in range(unroll):
            idx = tile * n_experts + g * unroll + u
            n = chunks_ref[idx]
            lo = loff_ref[idx]
            go = goff_ref[idx]
            for k in range(n_classes):
                @pl.when(n & (1 << k) != 0)
                def _():
                    done = (n & ~((2 << k) - 1)) * rows
                    make(pl.multiple_of(lo + done, rows), pl.multiple_of(go + done, rows),
                         rows << k).start()
        return carry

    lax.fori_loop(0, n_experts // unroll, per_experts, 0)


def _wait_group_dmas(classcnt_ref, tile, p, n_classes, make):
    for k in range(n_classes):
        def wait_one(j, c, k=k):
            make(0, 0, (CHUNK << k) * p).wait()
            return c

        lax.fori_loop(0, classcnt_ref[tile * n_classes + k], wait_one, 0)


def _dispatch_kernel(chunks_ref, loff_ref, goff_ref, classcnt_ref, tstart_ref, tchunks_ref,
                     x_ref, slot_ref, xs_ref, stage, sem, *, n_experts):
    tile = pl.program_id(0)
    n_tiles = pl.num_programs(0)
    tt = slot_ref.shape[0] // TOP_K
    p = x_ref.shape[0] // tt
    buf_rows = stage.shape[0] // 2
    n_classes = _size_classes(tt)
    buf = tile % 2

    def make_for(which_buf):
        base = pl.multiple_of(which_buf * buf_rows, CHUNK * p)
        return lambda lrow, grow, nrows: pltpu.make_async_copy(
            stage.at[pl.ds(base + lrow, nrows), :], xs_ref.at[pl.ds(grow, nrows), :],
            sem.at[which_buf])

    @pl.when(tile < 2)
    def _():
        start = pl.multiple_of(buf * buf_rows, CHUNK * p)
        stage[pl.ds(start, buf_rows), :] = jnp.zeros((buf_rows, LANES), stage.dtype)

    @pl.when(tile >= 2)
    def _():
        _wait_group_dmas(classcnt_ref, tile - 2, p, n_classes, make_for(buf))

    @pl.when(tile == 0)
    def _():
        rows = CHUNK * p
        tail = lambda grow: pltpu.make_async_copy(
            stage.at[pl.ds(0, rows), :], xs_ref.at[pl.ds(grow, rows), :], sem.at[0])

        def per_expert(ex, total):
            n = tchunks_ref[ex]

            def per_chunk(j, c):
                tail(pl.multiple_of((tstart_ref[ex] + j * CHUNK) * p, rows)).start()
                return c

            lax.fori_loop(0, n, per_chunk, 0)
            return total + n

        total = lax.fori_loop(0, n_experts, per_expert, 0)

        def wait_tail(j, c):
            tail(0).wait()
            return c

        lax.fori_loop(0, total, wait_tail, 0)

    unroll = 8

    def per_tokens(g, carry):
        for u in range(unroll):
            t = g * unroll + u
            v = x_ref[pl.ds(pl.multiple_of(t * p, p), p), :]
            for k in range(TOP_K):
                s = slot_ref[k * tt + t]
                stage[pl.ds(pl.multiple_of(s, p), p), :] = v
        return carry

    lax.fori_loop(0, tt // unroll, per_tokens, 0)
    _start_group_dmas(chunks_ref, loff_ref, goff_ref, tile, n_experts, p, n_classes,
                      make_for(buf))

    @pl.when(tile == n_tiles - 1)
    def _():
        @pl.when(tile >= 1)
        def _():
            _wait_group_dmas(classcnt_ref, tile - 1, p, n_classes, make_for(1 - buf))

        _wait_group_dmas(classcnt_ref, tile, p, n_classes, make_for(buf))


def _stage_rows(tt, n_experts):
    rows = TOP_K * tt + n_experts * (CHUNK - 1)
    return -(-rows // CHUNK) * CHUNK


def _dispatch(hlin, slots, tables, tails, *, n_rows, n_experts, d, tt):
    chunks, loff, goff, classcnt = tables
    tstart, tchunks = tails
    nt = slots.shape[0] // (TOP_K * tt)
    p = d // (2 * LANES)
    return pl.pallas_call(
        functools.partial(_dispatch_kernel, n_experts=n_experts),
        grid_spec=pltpu.PrefetchScalarGridSpec(
            num_scalar_prefetch=6,
            grid=(nt,),
            in_specs=[pl.BlockSpec((tt * p, LANES), lambda i, *_: (i, 0)),
                      pl.BlockSpec((TOP_K * tt,), lambda i, *_: (i,),
                                   memory_space=pltpu.SMEM)],
            out_specs=pl.BlockSpec(memory_space=pl.ANY),
            scratch_shapes=[pltpu.VMEM((2 * _stage_rows(tt, n_experts) * p, LANES), jnp.uint32),
                            pltpu.SemaphoreType.DMA((2,))]),
        out_shape=jax.ShapeDtypeStruct((n_rows * p, LANES), jnp.uint32),
        compiler_params=_cparams(1),
        name="dispatch",
    )(chunks, loff, goff, classcnt, tstart, tchunks, hlin, slots)


def _expert_kernel(be_ref, nu_ref, x_ref, win_ref, bin_ref, wout_ref, bout_ref, y_ref,
                   win_bf, wout_bf):
    b = pl.program_id(0)
    d = win_ref.shape[1]
    f = wout_ref.shape[1]
    p = d // (2 * LANES)
    rb = x_ref.shape[0] // p

    @pl.when(b < nu_ref[0])
    def _():
        prev = be_ref[jnp.maximum(b - 1, 0)]

        @pl.when(jnp.logical_or(b == 0, be_ref[b] != prev))
        def _():
            win_bf[...] = win_ref[0].astype(BF16)
            wout_bf[...] = wout_ref[0].astype(BF16)

        halves = [h for s in range(p) for h in _unpack_words(x_ref[pl.ds(s, rb, stride=p), :])]
        x = jnp.concatenate(halves, axis=-1)
        u = jnp.dot(x.astype(BF16), win_bf[...], preferred_element_type=F32) + bin_ref[0]
        x_glu = jnp.minimum(u[:, :f], SWIGLU_LIMIT)
        x_lin = jnp.clip(u[:, f:], -SWIGLU_LIMIT, SWIGLU_LIMIT)
        act = (x_lin + 1.0) * (x_glu * jax.nn.sigmoid(SWIGLU_ALPHA * x_glu))
        y = jnp.dot(act.astype(BF16), wout_bf[...], preferred_element_type=F32) + bout_ref[0]
        for s, w in enumerate(_pack_rows(y)):
            y_ref[pl.ds(s, rb, stride=p), :] = w


def _experts(xs, block_e, n_used, w_in, b_in, w_out, b_out, *, layer, rb, nb):
    _, e, d, f2 = w_in.shape
    f = f2 // 2
    p = d // (2 * LANES)
    last = lambda b, nu: jnp.maximum(jnp.minimum(b, nu[0] - 1), 0)
    blk = lambda b, be, nu: (last(b, nu), 0)
    wsel = lambda b, be, nu: (layer, be[last(b, nu)], 0, 0)
    return pl.pallas_call(
        _expert_kernel,
        grid_spec=pltpu.PrefetchScalarGridSpec(
            num_scalar_prefetch=2,
            grid=(nb,),
            in_specs=[pl.BlockSpec((rb * p, LANES), blk),
                      pl.BlockSpec((None, 1, d, f2), wsel),
                      pl.BlockSpec((None, 1, 1, f2), wsel),
                      pl.BlockSpec((None, 1, f, d), wsel),
                      pl.BlockSpec((None, 1, 1, d), wsel)],
            out_specs=pl.BlockSpec((rb * p, LANES), blk),
            scratch_shapes=[pltpu.VMEM((d, f2), BF16), pltpu.VMEM((f, d), BF16)]),
        out_shape=jax.ShapeDtypeStruct(xs.shape, xs.dtype),
        compiler_params=_cparams(1),
        name="experts",
    )(block_e, n_used, xs, w_in, b_in.reshape(-1, e, 1, f2), w_out, b_out.reshape(-1, e, 1, d))


def _combine_kernel(chunks_ref, loff_ref, goff_ref, classcnt_ref, ys_ref, slot_ref, gate_ref,
                    xres_ref, mod_ref, lng_ref, lnb_ref, x_out_ref, stage, comb_even, comb_odd, sem,
                    *, n_experts, alpha):
    tile = pl.program_id(0)
    n_tiles = pl.num_programs(0)
    tt, d = xres_ref.shape
    p = d // (2 * LANES)
    buf_rows = stage.shape[0] // 2
    n_classes = _size_classes(tt)
    buf = tile % 2

    def make_for(which_buf):
        base = pl.multiple_of(which_buf * buf_rows, CHUNK * p)
        return lambda lrow, grow, nrows: pltpu.make_async_copy(
            ys_ref.at[pl.ds(grow, nrows), :], stage.at[pl.ds(base + lrow, nrows), :],
            sem.at[which_buf])

    def start_tile(which_tile, which_buf):
        _start_group_dmas(chunks_ref, loff_ref, goff_ref, which_tile, n_experts, p, n_classes,
                          make_for(which_buf))

    @pl.when(tile == 0)
    def _():
        start_tile(tile, buf)

    @pl.when(tile + 1 < n_tiles)
    def _():
        start_tile(tile + 1, 1 - buf)

    _wait_group_dmas(classcnt_ref, tile, p, n_classes, make_for(buf))
    unroll = 8

    def per_tokens(g, carry):
        for u in range(unroll):
            t = g * unroll + u
            acc = None
            for k in range(TOP_K):
                s = slot_ref[k * tt + t]
                gate = gate_ref[k * tt + t]
                lo, hi = _unpack_words(stage[pl.ds(pl.multiple_of(s, p), p), :])
                lo, hi = gate * lo, gate * hi
                acc = (lo, hi) if acc is None else (acc[0] + lo, acc[1] + hi)
            row = pl.ds(pl.multiple_of(t * p, p), p)
            comb_even[row, :] = acc[0]
            comb_odd[row, :] = acc[1]
        return carry

    lax.fori_loop(0, tt // unroll, per_tokens, 0)
    moe = jnp.concatenate([(comb_odd if c % 2 else comb_even)[pl.ds(c // 2, tt, stride=p), :]
                           for c in range(2 * p)], axis=-1)
    g2 = mod_ref[5:6, :]
    x_out_ref[...] = _layer_norm(alpha * xres_ref[...] + (1.0 + g2) * moe,
                                 lng_ref[...], lnb_ref[...])


def _combine(ys, slots, gates, tables, xres, mod, lng, lnb, *, n_experts, seq, tt, alpha):
    chunks, loff, goff, classcnt = tables
    t, d = xres.shape
    nt = t // tt
    p = d // (2 * LANES)
    tiles_per_seq = seq // tt
    smem = lambda: pl.BlockSpec((TOP_K * tt,), lambda i, *_: (i,), memory_space=pltpu.SMEM)
    return pl.pallas_call(
        functools.partial(_combine_kernel, n_experts=n_experts, alpha=alpha),
        grid_spec=pltpu.PrefetchScalarGridSpec(
            num_scalar_prefetch=4,
            grid=(nt,),
            in_specs=[pl.BlockSpec(memory_space=pl.ANY), smem(), smem(),
                      pl.BlockSpec((tt, d), lambda i, *_: (i, 0)),
                      pl.BlockSpec((None, 6, d), lambda i, *_: (i // tiles_per_seq, 0, 0)),
                      pl.BlockSpec((1, d), lambda i, *_: (0, 0)),
                      pl.BlockSpec((1, d), lambda i, *_: (0, 0))],
            out_specs=pl.BlockSpec((tt, d), lambda i, *_: (i, 0)),
            scratch_shapes=[pltpu.VMEM((2 * _stage_rows(tt, n_experts) * p, LANES), jnp.uint32),
                            pltpu.VMEM((tt * p, LANES), F32), pltpu.VMEM((tt * p, LANES), F32),
                            pltpu.SemaphoreType.DMA((2,))]),
        out_shape=jax.ShapeDtypeStruct((t, d), F32),
        compiler_params=_cparams(1),
        name="combine",
    )(chunks, loff, goff, classcnt, ys, slots, gates, xres, mod, lng.reshape(1, d),
      lnb.reshape(1, d))


def _moe(hlin, logt, xres, mod, lng, lnb, w_in, b_in, w_out, b_out, *, layer, seq, tt, rb,
         alpha):
    e, t = logt.shape
    d = xres.shape[1]
    nt = t // tt
    p = d // (2 * LANES)
    slots, gates, cnt = _route(logt, tt, p, _stage_rows(tt, e))
    slots = slots.reshape(-1)
    gates = gates.reshape(-1)
    max_rows = TOP_K * t + nt * e * (CHUNK - 1)
    nb = -(-max_rows // rb) + e
    chunks, loff, goff, classcnt, tstart, tchunks, block_e, n_used = _plan(
        cnt[:, :, 0], rb, nb, _size_classes(tt), p)
    tables = (chunks, loff, goff, classcnt)
    xs = _dispatch(hlin, slots, tables, (tstart, tchunks), n_rows=nb * rb, n_experts=e, d=d,
                   tt=tt)
    ys = _experts(xs, block_e, n_used, w_in, b_in, w_out, b_out, layer=layer, rb=rb, nb=nb)
    return _combine(ys, slots, gates, tables, xres, mod, lng, lnb,
                    n_experts=e, seq=seq, tt=tt, alpha=alpha)


def _qkv_kernel(x_ref, mod_ref, qwt_ref, kw_ref, vwt_ref, qt_ref, k_ref, vt_ref):
    x = x_ref[...]
    sh1 = mod_ref[0:1, :]
    sc1 = mod_ref[1:2, :]
    h = (x * (1.0 + sc1) + sh1).astype(BF16)
    nt_dims = (((1,), (1,)), ((), ()))
    qt = lax.dot_general(qwt_ref[...], h, nt_dims, preferred_element_type=F32).astype(BF16)
    tq = qt_ref.shape[3]
    for j in range(qt_ref.shape[1]):
        qt_ref[0, j] = qt[:, j * tq:(j + 1) * tq]
    tk = vt_ref.shape[3]
    ngrp = tk // 8
    r = lax.broadcasted_iota(jnp.int32, (tk, tk), 0)
    s = lax.broadcasted_iota(jnp.int32, (tk, tk), 1)
    perm = jnp.where(s == (r & 7) * ngrp + (r >> 3), 1.0, 0.0).astype(BF16)
    xb = x.astype(BF16)
    xp = jnp.concatenate(
        [jnp.dot(perm, xb[j * tk:(j + 1) * tk, :], preferred_element_type=F32)
         for j in range(vt_ref.shape[1])], axis=0).astype(BF16)
    k_ref[...] = jnp.dot(xp, kw_ref[...], preferred_element_type=F32).astype(BF16)
    vt = lax.dot_general(vwt_ref[...], xp, nt_dims, preferred_element_type=F32).astype(BF16)
    for j in range(vt_ref.shape[1]):
        vt_ref[0, j] = vt[:, j * tk:(j + 1) * tk]


def _qkv(x2d, mod, qwt, kw, vwt, *, bsz, seq, tl, tk):
    t, d = x2d.shape
    nlt = seq // tl
    full = pl.BlockSpec((d, d), lambda b, l: (0, 0))
    return pl.pallas_call(
        _qkv_kernel,
        grid=(bsz, nlt),
        in_specs=[pl.BlockSpec((tl, d), lambda b, l: (b * nlt + l, 0)),
                  pl.BlockSpec((None, 6, d), lambda b, l: (b, 0, 0)),
                  full, full, full],
        out_specs=[pl.BlockSpec((1, tl // tk, d, tk), lambda b, l: (b, l, 0, 0)),
                   pl.BlockSpec((tl, d), lambda b, l: (b * nlt + l, 0)),
                   pl.BlockSpec((1, tl // tk, d, tk), lambda b, l: (b, l, 0, 0))],
        out_shape=[jax.ShapeDtypeStruct((bsz, seq // tk, d, tk), BF16),
                   jax.ShapeDtypeStruct((t, d), BF16),
                   jax.ShapeDtypeStruct((bsz, seq // tk, d, tk), BF16)],
        compiler_params=_cparams(2),
        name="qkv",
    )(x2d, mod, qwt, kw, vwt)


def _stick_block(z_ref, a_ref, carry, masked):
    tk, tq = z_ref.shape
    ngrp = tk // 8
    sub = lax.broadcasted_iota(jnp.int32, (8, tq), 0)
    lane = lax.broadcasted_iota(jnp.int32, (8, tq), 1)
    causal = lambda g: sub * ngrp + g < lane
    run = None
    for g in reversed(range(ngrp)):
        z = z_ref[g * 8:(g + 1) * 8, :]
        rest = 1.0 / (1.0 + jnp.exp2(z))
        beta = 1.0 - rest
        if masked:
            rest = jnp.where(causal(g), rest, 1.0)
            beta = jnp.where(causal(g), beta, 0.0)
        z_ref[g * 8:(g + 1) * 8, :] = beta if run is None else beta * run
        run = rest if run is None else run * rest
    s = run
    for k in (1, 2, 4):
        s = s * jnp.where(sub < 8 - k, pltpu.roll(s, 8 - k, axis=0), 1.0)
    later = jnp.where(sub < 7, pltpu.roll(s, 7, axis=0), 1.0) * carry
    for g in range(0, ngrp, 2):
        a_ref[g * 8:(g + 2) * 8, :] = jnp.concatenate(
            [z_ref[gg * 8:(gg + 1) * 8, :] * later for gg in (g, g + 1)], axis=0).astype(BF16)
    return carry * jnp.broadcast_to(s[0:1, :], (8, tq))


def _attn_kernel(qt_ref, k_ref, vt_ref, o_ref, z0, z1, z2, a0, a1, acc_buf, qm_buf, carry_buf,
                 *, head_dim):
    nq, _, tq = qt_ref.shape[1:]
    tk = vt_ref.shape[3]
    hp = qt_ref.shape[2] // head_dim
    row = lax.broadcasted_iota(jnp.int32, (hp * head_dim, tq), 0)

    def set_queries(qi):
        qf = qt_ref[0, qi].astype(F32)
        for h in range(hp):
            in_head = jnp.logical_and(row >= h * head_dim, row < (h + 1) * head_dim)
            qm_buf[h] = jnp.where(in_head, qf, 0.0).astype(BF16)

    def scores(kb, z_out):
        kblk = k_ref[pl.ds(pl.multiple_of(kb * tk, tk), tk), :]
        for h in range(hp):
            z_out[h] = jnp.dot(kblk, qm_buf[h], preferred_element_type=F32)

    def weighted_values(kb, a_in):
        return [jnp.dot(vt_ref[0, kb, h * head_dim:(h + 1) * head_dim, :], a_in[h],
                        preferred_element_type=F32) for h in range(hp)]

    def diagonal(z_in, a_out):
        one = jnp.ones((8, tq), F32)
        for h in range(hp):
            carry_buf[h] = _stick_block(z_in.at[h], a_out.at[h], one, True)

    def open_next(qi):
        nxt = jnp.minimum(qi + 1, nq - 1)
        set_queries(nxt)
        scores(nxt, z2)

    def step(kb, z_in, z_out, a_in, a_out, carries, last_of=None):
        pv = weighted_values(kb + 1, a_in)
        if last_of is None:
            scores(jnp.maximum(kb - 1, 0), z_out)
        else:
            open_next(last_of)
        carries = tuple(_stick_block(z_in.at[h], a_out.at[h], carries[h], False)
                        for h in range(hp))
        for h in range(hp):
            acc_buf[h] += pv[h]
        return carries

    def finish(qi, a_last):
        nxt = jnp.minimum(qi + 1, nq - 1)
        pv = weighted_values(0, a_last)
        scores(jnp.maximum(nxt - 1, 0), z1)
        out = jnp.concatenate([(acc_buf[h] + pv[h]).T for h in range(hp)], axis=-1)
        o_ref[pl.ds(pl.multiple_of(qi * tq, tq), tq), :] = out.astype(o_ref.dtype)
        diagonal(z2, a0)
        acc_buf[...] = jnp.zeros(acc_buf.shape, F32)

    set_queries(0)
    scores(0, z2)
    diagonal(z2, a0)
    acc_buf[...] = jnp.zeros(acc_buf.shape, F32)
    open_next(0)
    finish(0, a0)

    def query_tile(qi, c):
        def pair(i, carries):
            kb = qi - 1 - 2 * i
            carries = step(kb, z1, z0, a0, a1, carries)
            return step(kb - 1, z0, z1, a1, a0, carries)

        carries = lax.fori_loop(0, (qi - 1) // 2, pair,
                                tuple(carry_buf[h] for h in range(hp)))

        @pl.when(qi % 2 == 1)
        def _():
            step(0, z1, z0, a0, a1, carries, last_of=qi)
            finish(qi, a1)

        @pl.when(qi % 2 == 0)
        def _():
            mid = step(1, z1, z0, a0, a1, carries)
            step(0, z0, z1, a1, a0, mid, last_of=qi)
            finish(qi, a0)

        return c

    lax.fori_loop(1, nq, query_tile, 0)


def _attention(qt, k2d, vt, *, bsz, seq):
    _, nq, d, tq = qt.shape
    head_dim = d // N_HEADS
    tk = vt.shape[3]
    assert tk == tq, "the diagonal mask assumes square blocks"
    ncol = d // LANES
    hp = LANES // head_dim
    return pl.pallas_call(
        functools.partial(_attn_kernel, head_dim=head_dim),
        grid=(bsz, ncol),
        in_specs=[pl.BlockSpec((1, nq, LANES, tq), lambda b, c: (b, 0, c, 0)),
                  pl.BlockSpec((seq, LANES), lambda b, c: (b, c)),
                  pl.BlockSpec((1, seq // tk, LANES, tk), lambda b, c: (b, 0, c, 0))],
        out_specs=pl.BlockSpec((seq, LANES), lambda b, c: (b, c)),
        out_shape=jax.ShapeDtypeStruct((bsz * seq, d), BF16),
        scratch_shapes=[pltpu.VMEM((hp, tk, tq), F32), pltpu.VMEM((hp, tk, tq), F32),
                        pltpu.VMEM((hp, tk, tq), F32),
                        pltpu.VMEM((hp, tk, tq), BF16), pltpu.VMEM((hp, tk, tq), BF16),
                        pltpu.VMEM((hp, head_dim, tq), F32),
                        pltpu.VMEM((hp, hp * head_dim, tq), BF16),
                        pltpu.VMEM((hp, 8, tq), F32)],
        compiler_params=_cparams(2),
        name="attention",
    )(qt, k2d, vt)


def _oproj_kernel(o_ref, xres_ref, mod_ref, ow_ref, lng_ref, lnb_ref, rwt_ref, rb_ref,
                  x_out_ref, hlin_ref, logt_ref, *, alpha):
    y = jnp.dot(o_ref[...], ow_ref[...], preferred_element_type=F32)
    _post_norm_route(xres_ref[...], y, mod_ref, lng_ref, lnb_ref, rwt_ref, rb_ref,
                     x_out_ref, hlin_ref, logt_ref, alpha)


def _oproj(o2d, xres, mod, ow, lng, lnb, rwt, rb, *, seq, tl, alpha):
    t, d = xres.shape
    e = rwt.shape[0]
    p = d // (2 * LANES)
    nlt = seq // tl
    full = lambda shape: pl.BlockSpec(shape, lambda i: (0,) * len(shape))
    return pl.pallas_call(
        functools.partial(_oproj_kernel, alpha=alpha),
        grid=(t // tl,),
        in_specs=[pl.BlockSpec((tl, d), lambda i: (i, 0)),
                  pl.BlockSpec((tl, d), lambda i: (i, 0)),
                  pl.BlockSpec((None, 6, d), lambda i: (i // nlt, 0, 0)),
                  full((d, d)), full((1, d)), full((1, d)), full((e, d)), full((e, 1))],
        out_specs=[pl.BlockSpec((tl, d), lambda i: (i, 0)),
                   pl.BlockSpec((tl * p, LANES), lambda i: (i, 0)),
                   pl.BlockSpec((e, tl), lambda i: (0, i))],
        out_shape=[jax.ShapeDtypeStruct((t, d), F32),
                   jax.ShapeDtypeStruct((t * p, LANES), jnp.uint32),
                   jax.ShapeDtypeStruct((e, t), F32)],
        compiler_params=_cparams(1),
        name="oproj",
    )(o2d, xres, mod, ow, lng.reshape(1, d), lnb.reshape(1, d), rwt, rb.reshape(e, 1))


def kernel(x, c, ada_w, ada_b, ln_g, ln_b, cv_w1, cv_b1, cv_dw, cv_db, cv_ln_g, cv_ln_b,
           cv_w2, cv_b2, kv_w, q_w, o_w, router_w, router_b, moe_w_in, moe_b_in,
           moe_w_out, moe_b_out):
    bsz, seq, d = x.shape
    depth = ada_w.shape[0]
    n_a = cv_w1.shape[0]
    assert depth == 2 and n_a == 1 and q_w.shape[0] == 1, "one conv layer then one attention layer"
    alpha = (2.0 * depth) ** 0.25
    head_dim = d // N_HEADS
    tl = min(512, seq)
    tt = min(1024, seq)
    rb = 512
    tq = min(256, seq)

    mod = _ada(c, ada_w, ada_b).reshape(depth, bsz, 6, d)
    x0 = x.reshape(bsz * seq, d)
    rwt = jnp.swapaxes(router_w, 1, 2)

    x1, hlin, logt = _conv_mixer(
        x0, mod[0], cv_w1[0].astype(BF16), cv_b1[0], cv_dw[0], cv_db[0], cv_ln_g[0], cv_ln_b[0],
        cv_w2[0].astype(BF16), cv_b2[0], ln_g[0, 0], ln_b[0, 0], rwt[0], router_b[0],
        bsz=bsz, seq=seq, tl=tl, alpha=alpha)
    x2 = _moe(hlin, logt, x1, mod[0], ln_g[0, 1], ln_b[0, 1], moe_w_in, moe_b_in,
              moe_w_out, moe_b_out, layer=0, seq=seq, tt=tt, rb=rb, alpha=alpha)

    scale = head_dim ** -0.5
    qwt = (q_w[0] * (scale * LOG2E)).T.astype(BF16)
    kw = kv_w[:, :d].astype(BF16)
    vwt = kv_w[:, d:].T.astype(BF16)
    qt, k2d, vt = _qkv(x2, mod[1], qwt, kw, vwt, bsz=bsz, seq=seq, tl=tl, tk=tq)
    o2d = _attention(qt, k2d, vt, bsz=bsz, seq=seq)
    x3, hlin, logt = _oproj(o2d, x2, mod[1], o_w[0].astype(BF16), ln_g[1, 0], ln_b[1, 0],
                            rwt[1], router_b[1], seq=seq, tl=tl, alpha=alpha)
    x4 = _moe(hlin, logt, x3, mod[1], ln_g[1, 1], ln_b[1, 1], moe_w_in, moe_b_in,
              moe_w_out, moe_b_out, layer=1, seq=seq, tt=tt, rb=rb, alpha=alpha)
    return x4.reshape(bsz, seq, d)
```

```python
import functools

import jax
import jax.numpy as jnp
from jax import lax
from jax.experimental import pallas as pl
from jax.experimental.pallas import tpu as pltpu

F32 = jnp.float32
BF16 = jnp.bfloat16
HIGHEST = lax.Precision.HIGHEST

LANES = 128
N_HEADS = 16
TOP_K = 4
CONV_HALO = 32
SWIGLU_LIMIT = 7.0
SWIGLU_ALPHA = 1.702
LN_EPS = 1e-5
LOG2E = 1.4426950408889634
CHUNK = 8
VMEM_LIMIT = 56 * 1024 * 1024


def _cparams(n_axes):
    return pltpu.CompilerParams(
        dimension_semantics=("arbitrary",) * n_axes, vmem_limit_bytes=VMEM_LIMIT)


def _layer_norm(x, g, b):
    mu = jnp.mean(x, axis=-1, keepdims=True)
    xc = x - mu
    var = jnp.mean(xc * xc, axis=-1, keepdims=True)
    return xc * lax.rsqrt(var + LN_EPS) * g + b


def _silu(x):
    return x * jax.nn.sigmoid(x)


def _split_bf16(a):
    hi = a.astype(BF16)
    return hi, (a - hi.astype(F32)).astype(BF16)


def _bf16_bits(a):
    return lax.bitcast_convert_type(a.astype(BF16).astype(F32), jnp.uint32)


def _pack_rows(x):
    pieces = []
    for s in range(x.shape[1] // (2 * LANES)):
        lo = _bf16_bits(x[:, 2 * s * LANES:(2 * s + 1) * LANES])
        hi = _bf16_bits(x[:, (2 * s + 1) * LANES:(2 * s + 2) * LANES])
        pieces.append((hi & jnp.uint32(0xFFFF0000)) | (lo >> 16))
    return pieces


def _unpack_words(w):
    return (lax.bitcast_convert_type(w << 16, F32),
            lax.bitcast_convert_type(w & jnp.uint32(0xFFFF0000), F32))


def _ada_kernel(c_ref, w_ref, b_ref, o_ref):
    o_ref[0] = jnp.dot(_silu(c_ref[...]), w_ref[0], precision=HIGHEST,
                       preferred_element_type=F32) + b_ref[0]


def _ada(c, ada_w, ada_b):
    depth, d, n = ada_w.shape
    bsz = c.shape[0]
    tn = d
    return pl.pallas_call(
        _ada_kernel,
        grid=(depth, n // tn),
        in_specs=[pl.BlockSpec((bsz, d), lambda l, j: (0, 0)),
                  pl.BlockSpec((1, d, tn), lambda l, j: (l, 0, j)),
                  pl.BlockSpec((1, 1, tn), lambda l, j: (l, 0, j))],
        out_specs=pl.BlockSpec((1, bsz, tn), lambda l, j: (l, 0, j)),
        out_shape=jax.ShapeDtypeStruct((depth, bsz, n), F32),
        compiler_params=_cparams(2),
        name="ada",
    )(c, ada_w, ada_b.reshape(depth, 1, n))


def _post_norm_route(xres, y, mod_ref, lng_ref, lnb_ref, rwt_ref, rb_ref,
                     x_out_ref, hlin_ref, logt_ref, alpha):
    tl, d = xres.shape
    g1 = mod_ref[2:3, :]
    sh2 = mod_ref[3:4, :]
    sc2 = mod_ref[4:5, :]
    x1 = _layer_norm(alpha * xres + (1.0 + g1) * y, lng_ref[...], lnb_ref[...])
    x_out_ref[...] = x1
    h2 = x1 * (1.0 + sc2) + sh2
    pieces = _pack_rows(h2)
    for s, w in enumerate(pieces):
        hlin_ref[pl.ds(s, tl, stride=len(pieces)), :] = w
    w_hi, w_lo = _split_bf16(rwt_ref[...])
    h_hi, h_lo = _split_bf16(h2)
    nt = lambda a, b: lax.dot_general(a, b, (((1,), (1,)), ((), ())),
                                      preferred_element_type=F32)
    logt_ref[...] = nt(w_hi, h_hi) + (nt(w_hi, h_lo) + nt(w_lo, h_hi)) + rb_ref[...]


def _conv_kernel(x_ref, mod_ref, w1_ref, b1_ref, dw_ref, db_ref, clg_ref, clb_ref,
                 w2_ref, b2_ref, lng_ref, lnb_ref, rwt_ref, rb_ref,
                 x_out_ref, hlin_ref, logt_ref,
                 ubuf, sbuf, dwb, cbuf, *, alpha, width):
    tl, d = x_ref.shape
    first_tile = pl.program_id(1) == 0

    @pl.when(jnp.logical_and(pl.program_id(0) == 0, first_tile))
    def _():
        for j in range(width):
            dwb[j] = jnp.broadcast_to(dw_ref[j:j + 1, :], (8, d))

    x = x_ref[...]
    sh1 = mod_ref[0:1, :]
    sc1 = mod_ref[1:2, :]
    h = (x * (1.0 + sc1) + sh1).astype(BF16)
    u = jnp.dot(h, w1_ref[...], preferred_element_type=F32) + b1_ref[...]
    glu = u[:, :d] * jax.nn.sigmoid(u[:, d:])

    @pl.when(first_tile)
    def _():
        ubuf[0:CONV_HALO, :] = jnp.zeros((CONV_HALO, d), F32)

    ubuf[CONV_HALO:CONV_HALO + tl, :] = glu
    for r in range(1, 8):
        sbuf[r - 1] = ubuf[pl.ds(r, tl + CONV_HALO - 8), :]

    off0 = CONV_HALO - (width - 1)
    cw = 512
    rows = 32

    def body(i, carry):
        base = pl.multiple_of(i * rows, rows)
        for c in range(d // cw):
            cs = slice(c * cw, (c + 1) * cw)
            accs = [jnp.broadcast_to(db_ref[:, cs], (8, cw)) for _ in range(rows // 8)]
            for j in range(width):
                q, r = divmod(off0 + j, 8)
                w = dwb[j, :, cs]
                for a in range(rows // 8):
                    start = base + 8 * (q + a)
                    if r == 0:
                        v = ubuf[pl.ds(start, 8), cs]
                    else:
                        v = sbuf[r - 1, pl.ds(start, 8), cs]
                    accs[a] = accs[a] + w * v
            for a in range(rows // 8):
                cbuf[pl.ds(base + 8 * a, 8), cs] = accs[a]
        return carry

    lax.fori_loop(0, tl // rows, body, 0)
    ubuf[0:CONV_HALO, :] = ubuf[tl:tl + CONV_HALO, :]

    v = _silu(_layer_norm(cbuf[...], clg_ref[...], clb_ref[...]))
    y = jnp.dot(v.astype(BF16), w2_ref[...], preferred_element_type=F32) + b2_ref[...]
    _post_norm_route(x, y, mod_ref, lng_ref, lnb_ref, rwt_ref, rb_ref,
                     x_out_ref, hlin_ref, logt_ref, alpha)


def _conv_mixer(x2d, mod, w1, b1, dw, db, clg, clb, w2, b2, lng, lnb, rwt, rb,
                *, bsz, seq, tl, alpha):
    t, d = x2d.shape
    e = rwt.shape[0]
    width = dw.shape[0]
    p = d // (2 * LANES)
    nlt = seq // tl
    row = lambda a: a.reshape(1, -1)
    full = lambda shape: pl.BlockSpec(shape, lambda b, l: (0,) * len(shape))
    return pl.pallas_call(
        functools.partial(_conv_kernel, alpha=alpha, width=width),
        grid=(bsz, nlt),
        in_specs=[pl.BlockSpec((tl, d), lambda b, l: (b * nlt + l, 0)),
                  pl.BlockSpec((None, 6, d), lambda b, l: (b, 0, 0)),
                  full((d, 2 * d)), full((1, 2 * d)), full((width, d)), full((1, d)),
                  full((1, d)), full((1, d)), full((d, d)), full((1, d)),
                  full((1, d)), full((1, d)), full((e, d)), full((e, 1))],
        out_specs=[pl.BlockSpec((tl, d), lambda b, l: (b * nlt + l, 0)),
                   pl.BlockSpec((tl * p, LANES), lambda b, l: (b * nlt + l, 0)),
                   pl.BlockSpec((e, tl), lambda b, l: (0, b * nlt + l))],
        out_shape=[jax.ShapeDtypeStruct((t, d), F32),
                   jax.ShapeDtypeStruct((t * p, LANES), jnp.uint32),
                   jax.ShapeDtypeStruct((e, t), F32)],
        scratch_shapes=[pltpu.VMEM((tl + CONV_HALO, d), F32),
                        pltpu.VMEM((7, tl + CONV_HALO - 8, d), F32),
                        pltpu.VMEM((width, 8, d), F32),
                        pltpu.VMEM((tl, d), F32)],
        compiler_params=_cparams(2),
        name="conv_mixer",
    )(x2d, mod, w1, row(b1), dw, row(db), row(clg), row(clb), w2, row(b2),
      row(lng), row(lnb), rwt, rb.reshape(e, 1))


def _route_kernel(logt_ref, slot_ref, gate_ref, cnt_ref, *, p, stage_rows):
    e, tt = logt_ref.shape
    l = logt_ref[...]
    iota_e = lax.broadcasted_iota(jnp.int32, (e, tt), 0).astype(F32)
    sels, vals = [], []
    for _ in range(TOP_K):
        m = jnp.max(l, axis=0, keepdims=True)
        idx = jnp.min(jnp.where(l == m, iota_e, float(e)), axis=0, keepdims=True)
        sels.append(idx)
        vals.append(m)
        l = jnp.where(iota_e == idx, -jnp.inf, l)
    exps = [jnp.exp(v - vals[0]) for v in vals]
    denom = exps[0] + exps[1] + exps[2] + exps[3]
    onehots = [iota_e == s for s in sels]
    member = jnp.zeros((e, tt), F32)
    for oh in onehots:
        member = member + jnp.where(oh, 1.0, 0.0)
    tri = jnp.where(lax.broadcasted_iota(jnp.int32, (tt, tt), 0)
                    < lax.broadcasted_iota(jnp.int32, (tt, tt), 1), 1.0, 0.0).astype(BF16)
    rank = jnp.dot(member.astype(BF16), tri, preferred_element_type=F32)
    cnt = jnp.sum(member, axis=1, keepdims=True)
    padded = jnp.floor((cnt + (CHUNK - 1)) * (1.0 / CHUNK)) * CHUNK
    low = jnp.where(lax.broadcasted_iota(jnp.int32, (e, e), 1)
                    < lax.broadcasted_iota(jnp.int32, (e, e), 0), 1.0, 0.0)
    loff = jnp.dot(low, jnp.broadcast_to(padded, (e, tt)), precision=HIGHEST,
                   preferred_element_type=F32)
    pos = loff + rank + (pl.program_id(0) % 2 * stage_rows).astype(F32)
    for k in range(TOP_K):
        slot = jnp.sum(jnp.where(onehots[k], pos, 0.0), axis=0, keepdims=True)
        slot_ref[0, k:k + 1, :] = (slot * p).astype(jnp.int32)
        gate_ref[0, k:k + 1, :] = exps[k] / denom
    cnt_ref[0] = jnp.broadcast_to(cnt, (e, LANES)).astype(jnp.int32)


def _route(logt, tt, p, stage_rows):
    e, t = logt.shape
    nt = t // tt
    return pl.pallas_call(
        functools.partial(_route_kernel, p=p, stage_rows=stage_rows),
        grid=(nt,),
        in_specs=[pl.BlockSpec((e, tt), lambda i: (0, i))],
        out_specs=[pl.BlockSpec((1, TOP_K, tt), lambda i: (i, 0, 0)),
                   pl.BlockSpec((1, TOP_K, tt), lambda i: (i, 0, 0)),
                   pl.BlockSpec((1, e, LANES), lambda i: (i, 0, 0))],
        out_shape=[jax.ShapeDtypeStruct((nt, TOP_K, tt), jnp.int32),
                   jax.ShapeDtypeStruct((nt, TOP_K, tt), F32),
                   jax.ShapeDtypeStruct((nt, e, LANES), jnp.int32)],
        compiler_params=_cparams(1),
        name="route",
    )(logt)


def _plan(cnt, rb, nb, n_classes, p):
    nt, e = cnt.shape
    padded = ((cnt + CHUNK - 1) // CHUNK) * CHUNK
    loff = jnp.cumsum(padded, axis=1) - padded
    rows_e = jnp.sum(padded, axis=0)
    blocks_e = (rows_e + rb - 1) // rb
    bend = jnp.cumsum(blocks_e)
    pstart = (bend - blocks_e) * rb
    goff = pstart[None, :] + jnp.cumsum(padded, axis=0) - padded
    block_ids = jnp.arange(nb, dtype=bend.dtype)
    block_e = jnp.minimum(jnp.sum(bend[None, :] <= block_ids[:, None], axis=1), e - 1)
    chunks = padded // CHUNK
    classcnt = jnp.stack([jnp.sum((chunks >> k) & 1, axis=1) for k in range(n_classes)], axis=1)
    tail_start = pstart + rows_e
    tail_chunks = (blocks_e * rb - rows_e) // CHUNK
    i32 = lambda a: a.astype(jnp.int32)
    return (i32(chunks.reshape(-1)), i32(loff.reshape(-1) * p), i32(goff.reshape(-1) * p),
            i32(classcnt.reshape(-1)), i32(tail_start), i32(tail_chunks), i32(block_e),
            i32(bend[-1:]))


def _size_classes(tt):
    return (tt // CHUNK).bit_length()


def _start_group_dmas(chunks_ref, loff_ref, goff_ref, tile, n_experts, p, n_classes, make):
    unroll = 4
    assert n_experts % unroll == 0
    rows = CHUNK * p

    def per_experts(g, carry):
        for u in range(unroll):
            idx = tile * n_experts + g * unroll + u
            n = chunks_ref[idx]
            lo = loff_ref[idx]
            go = goff_ref[idx]
            for k in range(n_classes):
                @pl.when(n & (1 << k) != 0)
                def _():
                    done = (n & ~((2 << k) - 1)) * rows
                    make(pl.multiple_of(lo + done, rows), pl.multiple_of(go + done, rows),
                         rows << k).start()
        return carry

    lax.fori_loop(0, n_experts // unroll, per_experts, 0)


def _wait_group_dmas(classcnt_ref, tile, p, n_classes, make):
    for k in range(n_classes):
        def wait_one(j, c, k=k):
            make(0, 0, (CHUNK << k) * p).wait()
            return c

        lax.fori_loop(0, classcnt_ref[tile * n_classes + k], wait_one, 0)


def _dispatch_kernel(chunks_ref, loff_ref, goff_ref, classcnt_ref, tstart_ref, tchunks_ref,
                     x_ref, slot_ref, xs_ref, stage, sem, *, n_experts):
    tile = pl.program_id(0)
    n_tiles = pl.num_programs(0)
    tt = slot_ref.shape[0] // TOP_K
    p = x_ref.shape[0] // tt
    buf_rows = stage.shape[0] // 2
    n_classes = _size_classes(tt)
    buf = tile % 2

    def make_for(which_buf):
        base = pl.multiple_of(which_buf * buf_rows, CHUNK * p)
        return lambda lrow, grow, nrows: pltpu.make_async_copy(
            stage.at[pl.ds(base + lrow, nrows), :], xs_ref.at[pl.ds(grow, nrows), :],
            sem.at[which_buf])

    @pl.when(tile < 2)
    def _():
        start = pl.multiple_of(buf * buf_rows, CHUNK * p)
        stage[pl.ds(start, buf_rows), :] = jnp.zeros((buf_rows, LANES), stage.dtype)

    @pl.when(tile >= 2)
    def _():
        _wait_group_dmas(classcnt_ref, tile - 2, p, n_classes, make_for(buf))

    @pl.when(tile == 0)
    def _():
        rows = CHUNK * p
        tail = lambda grow: pltpu.make_async_copy(
            stage.at[pl.ds(0, rows), :], xs_ref.at[pl.ds(grow, rows), :], sem.at[0])

        def per_expert(ex, total):
            n = tchunks_ref[ex]

            def per_chunk(j, c):
                tail(pl.multiple_of((tstart_ref[ex] + j * CHUNK) * p, rows)).start()
                return c

            lax.fori_loop(0, n, per_chunk, 0)
            return total + n

        total = lax.fori_loop(0, n_experts, per_expert, 0)

        def wait_tail(j, c):
            tail(0).wait()
            return c

        lax.fori_loop(0, total, wait_tail, 0)

    unroll = 8

    def per_tokens(g, carry):
        for u in range(unroll):
            t = g * unroll + u
            v = x_ref[pl.ds(pl.multiple_of(t * p, p), p), :]
            for k in range(TOP_K):
                s = slot_ref[k * tt + t]
                stage[pl.ds(pl.multiple_of(s, p), p), :] = v
        return carry

    lax.fori_loop(0, tt // unroll, per_tokens, 0)
    _start_group_dmas(chunks_ref, loff_ref, goff_ref, tile, n_experts, p, n_classes,
                      make_for(buf))

    @pl.when(tile == n_tiles - 1)
    def _():
        @pl.when(tile >= 1)
        def _():
            _wait_group_dmas(classcnt_ref, tile - 1, p, n_classes, make_for(1 - buf))

        _wait_group_dmas(classcnt_ref, tile, p, n_classes, make_for(buf))


def _stage_rows(tt, n_experts):
    rows = TOP_K * tt + n_experts * (CHUNK - 1)
    return -(-rows // CHUNK) * CHUNK


def _dispatch(hlin, slots, tables, tails, *, n_rows, n_experts, d, tt):
    chunks, loff, goff, classcnt = tables
    tstart, tchunks = tails
    nt = slots.shape[0] // (TOP_K * tt)
    p = d // (2 * LANES)
    return pl.pallas_call(
        functools.partial(_dispatch_kernel, n_experts=n_experts),
        grid_spec=pltpu.PrefetchScalarGridSpec(
            num_scalar_prefetch=6,
            grid=(nt,),
            in_specs=[pl.BlockSpec((tt * p, LANES), lambda i, *_: (i, 0)),
                      pl.BlockSpec((TOP_K * tt,), lambda i, *_: (i,),
                                   memory_space=pltpu.SMEM)],
            out_specs=pl.BlockSpec(memory_space=pl.ANY),
            scratch_shapes=[pltpu.VMEM((2 * _stage_rows(tt, n_experts) * p, LANES), jnp.uint32),
                            pltpu.SemaphoreType.DMA((2,))]),
        out_shape=jax.ShapeDtypeStruct((n_rows * p, LANES), jnp.uint32),
        compiler_params=_cparams(1),
        name="dispatch",
    )(chunks, loff, goff, classcnt, tstart, tchunks, hlin, slots)


def _expert_kernel(be_ref, nu_ref, x_ref, win_ref, bin_ref, wout_ref, bout_ref, y_ref,
                   win_bf, wout_bf):
    b = pl.program_id(0)
    d = win_ref.shape[1]
    f = wout_ref.shape[1]
    p = d // (2 * LANES)
    rb = x_ref.shape[0] // p

    @pl.when(b < nu_ref[0])
    def _():
        prev = be_ref[jnp.maximum(b - 1, 0)]

        @pl.when(jnp.logical_or(b == 0, be_ref[b] != prev))
        def _():
            win_bf[...] = win_ref[0].astype(BF16)
            wout_bf[...] = wout_ref[0].astype(BF16)

        halves = [h for s in range(p) for h in _unpack_words(x_ref[pl.ds(s, rb, stride=p), :])]
        x = jnp.concatenate(halves, axis=-1)
        u = jnp.dot(x.astype(BF16), win_bf[...], preferred_element_type=F32) + bin_ref[0]
        x_glu = jnp.minimum(u[:, :f], SWIGLU_LIMIT)
        x_lin = jnp.clip(u[:, f:], -SWIGLU_LIMIT, SWIGLU_LIMIT)
        act = (x_lin + 1.0) * (x_glu * jax.nn.sigmoid(SWIGLU_ALPHA * x_glu))
        y = jnp.dot(act.astype(BF16), wout_bf[...], preferred_element_type=F32) + bout_ref[0]
        for s, w in enumerate(_pack_rows(y)):
            y_ref[pl.ds(s, rb, stride=p), :] = w


def _experts(xs, block_e, n_used, w_in, b_in, w_out, b_out, *, layer, rb, nb):
    _, e, d, f2 = w_in.shape
    f = f2 // 2
    p = d // (2 * LANES)
    last = lambda b, nu: jnp.maximum(jnp.minimum(b, nu[0] - 1), 0)
    blk = lambda b, be, nu: (last(b, nu), 0)
    wsel = lambda b, be, nu: (layer, be[last(b, nu)], 0, 0)
    return pl.pallas_call(
        _expert_kernel,
        grid_spec=pltpu.PrefetchScalarGridSpec(
            num_scalar_prefetch=2,
            grid=(nb,),
            in_specs=[pl.BlockSpec((rb * p, LANES), blk),
                      pl.BlockSpec((None, 1, d, f2), wsel),
                      pl.BlockSpec((None, 1, 1, f2), wsel),
                      pl.BlockSpec((None, 1, f, d), wsel),
                      pl.BlockSpec((None, 1, 1, d), wsel)],
            out_specs=pl.BlockSpec((rb * p, LANES), blk),
            scratch_shapes=[pltpu.VMEM((d, f2), BF16), pltpu.VMEM((f, d), BF16)]),
        out_shape=jax.ShapeDtypeStruct(xs.shape, xs.dtype),
        compiler_params=_cparams(1),
        name="experts",
    )(block_e, n_used, xs, w_in, b_in.reshape(-1, e, 1, f2), w_out, b_out.reshape(-1, e, 1, d))


def _combine_kernel(chunks_ref, loff_ref, goff_ref, classcnt_ref, ys_ref, slot_ref, gate_ref,
                    xres_ref, mod_ref, lng_ref, lnb_ref, x_out_ref, stage, comb_even, comb_odd, sem,
                    *, n_experts, alpha):
    tile = pl.program_id(0)
    n_tiles = pl.num_programs(0)
    tt, d = xres_ref.shape
    p = d // (2 * LANES)
    buf_rows = stage.shape[0] // 2
    n_classes = _size_classes(tt)
    buf = tile % 2

    def make_for(which_buf):
        base = pl.multiple_of(which_buf * buf_rows, CHUNK * p)
        return lambda lrow, grow, nrows: pltpu.make_async_copy(
            ys_ref.at[pl.ds(grow, nrows), :], stage.at[pl.ds(base + lrow, nrows), :],
            sem.at[which_buf])

    def start_tile(which_tile, which_buf):
        _start_group_dmas(chunks_ref, loff_ref, goff_ref, which_tile, n_experts, p, n_classes,
                          make_for(which_buf))

    @pl.when(tile == 0)
    def _():
        start_tile(tile, buf)

    @pl.when(tile + 1 < n_tiles)
    def _():
        start_tile(tile + 1, 1 - buf)

    _wait_group_dmas(classcnt_ref, tile, p, n_classes, make_for(buf))
    unroll = 8

    def per_tokens(g, carry):
        for u in range(unroll):
            t = g * unroll + u
            acc = None
            for k in range(TOP_K):
                s = slot_ref[k * tt + t]
                gate = gate_ref[k * tt + t]
                lo, hi = _unpack_words(stage[pl.ds(pl.multiple_of(s, p), p), :])
                lo, hi = gate * lo, gate * hi
                acc = (lo, hi) if acc is None else (acc[0] + lo, acc[1] + hi)
            row = pl.ds(pl.multiple_of(t * p, p), p)
            comb_even[row, :] = acc[0]
            comb_odd[row, :] = acc[1]
        return carry

    lax.fori_loop(0, tt // unroll, per_tokens, 0)
    moe = jnp.concatenate([(comb_odd if c % 2 else comb_even)[pl.ds(c // 2, tt, stride=p), :]
                           for c in range(2 * p)], axis=-1)
    g2 = mod_ref[5:6, :]
    x_out_ref[...] = _layer_norm(alpha * xres_ref[...] + (1.0 + g2) * moe,
                                 lng_ref[...], lnb_ref[...])


def _combine(ys, slots, gates, tables, xres, mod, lng, lnb, *, n_experts, seq, tt, alpha):
    chunks, loff, goff, classcnt = tables
    t, d = xres.shape
    nt = t // tt
    p = d // (2 * LANES)
    tiles_per_seq = seq // tt
    smem = lambda: pl.BlockSpec((TOP_K * tt,), lambda i, *_: (i,), memory_space=pltpu.SMEM)
    return pl.pallas_call(
        functools.partial(_combine_kernel, n_experts=n_experts, alpha=alpha),
        grid_spec=pltpu.PrefetchScalarGridSpec(
            num_scalar_prefetch=4,
            grid=(nt,),
            in_specs=[pl.BlockSpec(memory_space=pl.ANY), smem(), smem(),
                      pl.BlockSpec((tt, d), lambda i, *_: (i, 0)),
                      pl.BlockSpec((None, 6, d), lambda i, *_: (i // tiles_per_seq, 0, 0)),
                      pl.BlockSpec((1, d), lambda i, *_: (0, 0)),
                      pl.BlockSpec((1, d), lambda i, *_: (0, 0))],
            out_specs=pl.BlockSpec((tt, d), lambda i, *_: (i, 0)),
            scratch_shapes=[pltpu.VMEM((2 * _stage_rows(tt, n_experts) * p, LANES), jnp.uint32),
                            pltpu.VMEM((tt * p, LANES), F32), pltpu.VMEM((tt * p, LANES), F32),
                            pltpu.SemaphoreType.DMA((2,))]),
        out_shape=jax.ShapeDtypeStruct((t, d), F32),
        compiler_params=_cparams(1),
        name="combine",
    )(chunks, loff, goff, classcnt, ys, slots, gates, xres, mod, lng.reshape(1, d),
      lnb.reshape(1, d))


def _moe(hlin, logt, xres, mod, lng, lnb, w_in, b_in, w_out, b_out, *, layer, seq, tt, rb,
         alpha):
    e, t = logt.shape
    d = xres.shape[1]
    nt = t // tt
    p = d // (2 * LANES)
    slots, gates, cnt = _route(logt, tt, p, _stage_rows(tt, e))
    slots = slots.reshape(-1)
    gates = gates.reshape(-1)
    max_rows = TOP_K * t + nt * e * (CHUNK - 1)
    nb = -(-max_rows // rb) + e
    chunks, loff, goff, classcnt, tstart, tchunks, block_e, n_used = _plan(
        cnt[:, :, 0], rb, nb, _size_classes(tt), p)
    tables = (chunks, loff, goff, classcnt)
    xs = _dispatch(hlin, slots, tables, (tstart, tchunks), n_rows=nb * rb, n_experts=e, d=d,
                   tt=tt)
    ys = _experts(xs, block_e, n_used, w_in, b_in, w_out, b_out, layer=layer, rb=rb, nb=nb)
    return _combine(ys, slots, gates, tables, xres, mod, lng, lnb,
                    n_experts=e, seq=seq, tt=tt, alpha=alpha)


def _qkv_kernel(x_ref, mod_ref, qwt_ref, kw_ref, vwt_ref, qt_ref, k_ref, vt_ref):
    x = x_ref[...]
    sh1 = mod_ref[0:1, :]
    sc1 = mod_ref[1:2, :]
    h = (x * (1.0 + sc1) + sh1).astype(BF16)
    nt_dims = (((1,), (1,)), ((), ()))
    qt = lax.dot_general(qwt_ref[...], h, nt_dims, preferred_element_type=F32).astype(BF16)
    tq = qt_ref.shape[3]
    for j in range(qt_ref.shape[1]):
        qt_ref[0, j] = qt[:, j * tq:(j + 1) * tq]
    tk = vt_ref.shape[3]
    ngrp = tk // 8
    r = lax.broadcasted_iota(jnp.int32, (tk, tk), 0)
    s = lax.broadcasted_iota(jnp.int32, (tk, tk), 1)
    perm = jnp.where(s == (r & 7) * ngrp + (r >> 3), 1.0, 0.0).astype(BF16)
    xb = x.astype(BF16)
    xp = jnp.concatenate(
        [jnp.dot(perm, xb[j * tk:(j + 1) * tk, :], preferred_element_type=F32)
         for j in range(vt_ref.shape[1])], axis=0).astype(BF16)
    k_ref[...] = jnp.dot(xp, kw_ref[...], preferred_element_type=F32).astype(BF16)
    vt = lax.dot_general(vwt_ref[...], xp, nt_dims, preferred_element_type=F32).astype(BF16)
    for j in range(vt_ref.shape[1]):
        vt_ref[0, j] = vt[:, j * tk:(j + 1) * tk]


def _qkv(x2d, mod, qwt, kw, vwt, *, bsz, seq, tl, tk):
    t, d = x2d.shape
    nlt = seq // tl
    full = pl.BlockSpec((d, d), lambda b, l: (0, 0))
    return pl.pallas_call(
        _qkv_kernel,
        grid=(bsz, nlt),
        in_specs=[pl.BlockSpec((tl, d), lambda b, l: (b * nlt + l, 0)),
                  pl.BlockSpec((None, 6, d), lambda b, l: (b, 0, 0)),
                  full, full, full],
        out_specs=[pl.BlockSpec((1, tl // tk, d, tk), lambda b, l: (b, l, 0, 0)),
                   pl.BlockSpec((tl, d), lambda b, l: (b * nlt + l, 0)),
                   pl.BlockSpec((1, tl // tk, d, tk), lambda b, l: (b, l, 0, 0))],
        out_shape=[jax.ShapeDtypeStruct((bsz, seq // tk, d, tk), BF16),
                   jax.ShapeDtypeStruct((t, d), BF16),
                   jax.ShapeDtypeStruct((bsz, seq // tk, d, tk), BF16)],
        compiler_params=_cparams(2),
        name="qkv",
    )(x2d, mod, qwt, kw, vwt)


def _stick_block(z_ref, a_ref, carry, masked):
    tk, tq = z_ref.shape
    ngrp = tk // 8
    sub = lax.broadcasted_iota(jnp.int32, (8, tq), 0)
    lane = lax.broadcasted_iota(jnp.int32, (8, tq), 1)
    causal = lambda g: sub * ngrp + g < lane
    run = None
    for g in reversed(range(ngrp)):
        z = z_ref[g * 8:(g + 1) * 8, :]
        rest = 1.0 / (1.0 + jnp.exp2(z))
        beta = 1.0 - rest
        if masked:
            rest = jnp.where(causal(g), rest, 1.0)
            beta = jnp.where(causal(g), beta, 0.0)
        z_ref[g * 8:(g + 1) * 8, :] = beta if run is None else beta * run
        run = rest if run is None else run * rest
    s = run
    for k in (1, 2, 4):
        s = s * jnp.where(sub < 8 - k, pltpu.roll(s, 8 - k, axis=0), 1.0)
    later = jnp.where(sub < 7, pltpu.roll(s, 7, axis=0), 1.0) * carry
    for g in range(0, ngrp, 2):
        a_ref[g * 8:(g + 2) * 8, :] = jnp.concatenate(
            [z_ref[gg * 8:(gg + 1) * 8, :] * later for gg in (g, g + 1)], axis=0).astype(BF16)
    return carry * jnp.broadcast_to(s[0:1, :], (8, tq))


def _attn_kernel(qt_ref, k_ref, vt_ref, o_ref, z0, z1, z2, a0, a1, acc_buf, qm_buf, carry_buf,
                 *, head_dim):
    nq, _, tq = qt_ref.shape[1:]
    tk = vt_ref.shape[3]
    hp = qt_ref.shape[2] // head_dim
    row = lax.broadcasted_iota(jnp.int32, (hp * head_dim, tq), 0)

    def set_queries(qi):
        qf = qt_ref[0, qi].astype(F32)
        for h in range(hp):
            in_head = jnp.logical_and(row >= h * head_dim, row < (h + 1) * head_dim)
            qm_buf[h] = jnp.where(in_head, qf, 0.0).astype(BF16)

    def scores(kb, z_out):
        kblk = k_ref[pl.ds(pl.multiple_of(kb * tk, tk), tk), :]
        for h in range(hp):
            z_out[h] = jnp.dot(kblk, qm_buf[h], preferred_element_type=F32)

    def weighted_values(kb, a_in):
        return [jnp.dot(vt_ref[0, kb, h * head_dim:(h + 1) * head_dim, :], a_in[h],
                        preferred_element_type=F32) for h in range(hp)]

    def diagonal(z_in, a_out):
        one = jnp.ones((8, tq), F32)
        for h in range(hp):
            carry_buf[h] = _stick_block(z_in.at[h], a_out.at[h], one, True)

    def open_next(qi):
        nxt = jnp.minimum(qi + 1, nq - 1)
        set_queries(nxt)
        scores(nxt, z2)

    def step(kb, z_in, z_out, a_in, a_out, carries, last_of=None):
        pv = weighted_values(kb + 1, a_in)
        if last_of is None:
            scores(jnp.maximum(kb - 1, 0), z_out)
        else:
            open_next(last_of)
        carries = tuple(_stick_block(z_in.at[h], a_out.at[h], carries[h], False)
                        for h in range(hp))
        for h in range(hp):
            acc_buf[h] += pv[h]
        return carries

    def finish(qi, a_last):
        nxt = jnp.minimum(qi + 1, nq - 1)
        pv = weighted_values(0, a_last)
        scores(jnp.maximum(nxt - 1, 0), z1)
        out = jnp.concatenate([(acc_buf[h] + pv[h]).T for h in range(hp)], axis=-1)
        o_ref[pl.ds(pl.multiple_of(qi * tq, tq), tq), :] = out.astype(o_ref.dtype)
        diagonal(z2, a0)
        acc_buf[...] = jnp.zeros(acc_buf.shape, F32)

    set_queries(0)
    scores(0, z2)
    diagonal(z2, a0)
    acc_buf[...] = jnp.zeros(acc_buf.shape, F32)
    open_next(0)
    finish(0, a0)

    def query_tile(qi, c):
        def pair(i, carries):
            kb = qi - 1 - 2 * i
            carries = step(kb, z1, z0, a0, a1, carries)
            return step(kb - 1, z0, z1, a1, a0, carries)

        carries = lax.fori_loop(0, (qi - 1) // 2, pair,
                                tuple(carry_buf[h] for h in range(hp)))

        @pl.when(qi % 2 == 1)
        def _():
            step(0, z1, z0, a0, a1, carries, last_of=qi)
            finish(qi, a1)

        @pl.when(qi % 2 == 0)
        def _():
            mid = step(1, z1, z0, a0, a1, carries)
            step(0, z0, z1, a1, a0, mid, last_of=qi)
            finish(qi, a0)

        return c

    lax.fori_loop(1, nq, query_tile, 0)


def _attention(qt, k2d, vt, *, bsz, seq):
    _, nq, d, tq = qt.shape
    head_dim = d // N_HEADS
    tk = vt.shape[3]
    assert tk == tq, "the diagonal mask assumes square blocks"
    ncol = d // LANES
    hp = LANES // head_dim
    return pl.pallas_call(
        functools.partial(_attn_kernel, head_dim=head_dim),
        grid=(bsz, ncol),
        in_specs=[pl.BlockSpec((1, nq, LANES, tq), lambda b, c: (b, 0, c, 0)),
                  pl.BlockSpec((seq, LANES), lambda b, c: (b, c)),
                  pl.BlockSpec((1, seq // tk, LANES, tk), lambda b, c: (b, 0, c, 0))],
        out_specs=pl.BlockSpec((seq, LANES), lambda b, c: (b, c)),
        out_shape=jax.ShapeDtypeStruct((bsz * seq, d), BF16),
        scratch_shapes=[pltpu.VMEM((hp, tk, tq), F32), pltpu.VMEM((hp, tk, tq), F32),
                        pltpu.VMEM((hp, tk, tq), F32),
                        pltpu.VMEM((hp, tk, tq), BF16), pltpu.VMEM((hp, tk, tq), BF16),
                        pltpu.VMEM((hp, head_dim, tq), F32),
                        pltpu.VMEM((hp, hp * head_dim, tq), BF16),
                        pltpu.VMEM((hp, 8, tq), F32)],
        compiler_params=_cparams(2),
        name="attention",
    )(qt, k2d, vt)


def _oproj_kernel(o_ref, xres_ref, mod_ref, ow_ref, lng_ref, lnb_ref, rwt_ref, rb_ref,
                  x_out_ref, hlin_ref, logt_ref, *, alpha):
    y = jnp.dot(o_ref[...], ow_ref[...], preferred_element_type=F32)
    _post_norm_route(xres_ref[...], y, mod_ref, lng_ref, lnb_ref, rwt_ref, rb_ref,
                     x_out_ref, hlin_ref, logt_ref, alpha)


def _oproj(o2d, xres, mod, ow, lng, lnb, rwt, rb, *, seq, tl, alpha):
    t, d = xres.shape
    e = rwt.shape[0]
    p = d // (2 * LANES)
    nlt = seq // tl
    full = lambda shape: pl.BlockSpec(shape, lambda i: (0,) * len(shape))
    return pl.pallas_call(
        functools.partial(_oproj_kernel, alpha=alpha),
        grid=(t // tl,),
        in_specs=[pl.BlockSpec((tl, d), lambda i: (i, 0)),
                  pl.BlockSpec((tl, d), lambda i: (i, 0)),
                  pl.BlockSpec((None, 6, d), lambda i: (i // nlt, 0, 0)),
                  full((d, d)), full((1, d)), full((1, d)), full((e, d)), full((e, 1))],
        out_specs=[pl.BlockSpec((tl, d), lambda i: (i, 0)),
                   pl.BlockSpec((tl * p, LANES), lambda i: (i, 0)),
                   pl.BlockSpec((e, tl), lambda i: (0, i))],
        out_shape=[jax.ShapeDtypeStruct((t, d), F32),
                   jax.ShapeDtypeStruct((t * p, LANES), jnp.uint32),
                   jax.ShapeDtypeStruct((e, t), F32)],
        compiler_params=_cparams(1),
        name="oproj",
    )(o2d, xres, mod, ow, lng.reshape(1, d), lnb.reshape(1, d), rwt, rb.reshape(e, 1))


def kernel(x, c, ada_w, ada_b, ln_g, ln_b, cv_w1, cv_b1, cv_dw, cv_db, cv_ln_g, cv_ln_b,
           cv_w2, cv_b2, kv_w, q_w, o_w, router_w, router_b, moe_w_in, moe_b_in,
           moe_w_out, moe_b_out):
    bsz, seq, d = x.shape
    depth = ada_w.shape[0]
    n_a = cv_w1.shape[0]
    assert depth == 2 and n_a == 1 and q_w.shape[0] == 1, "one conv layer then one attention layer"
    alpha = (2.0 * depth) ** 0.25
    head_dim = d // N_HEADS
    tl = min(512, seq)
    tt = min(1024, seq)
    rb = 512
    tq = min(256, seq)

    mod = _ada(c, ada_w, ada_b).reshape(depth, bsz, 6, d)
    x0 = x.reshape(bsz * seq, d)
    rwt = jnp.swapaxes(router_w, 1, 2)

    x1, hlin, logt = _conv_mixer(
        x0, mod[0], cv_w1[0].astype(BF16), cv_b1[0], cv_dw[0], cv_db[0], cv_ln_g[0], cv_ln_b[0],
        cv_w2[0].astype(BF16), cv_b2[0], ln_g[0, 0], ln_b[0, 0], rwt[0], router_b[0],
        bsz=bsz, seq=seq, tl=tl, alpha=alpha)
    x2 = _moe(hlin, logt, x1, mod[0], ln_g[0, 1], ln_b[0, 1], moe_w_in, moe_b_in,
              moe_w_out, moe_b_out, layer=0, seq=seq, tt=tt, rb=rb, alpha=alpha)

    scale = head_dim ** -0.5
    qwt = (q_w[0] * (scale * LOG2E)).T.astype(BF16)
    kw = kv_w[:, :d].astype(BF16)
    vwt = kv_w[:, d:].T.astype(BF16)
    qt, k2d, vt = _qkv(x2, mod[1], qwt, kw, vwt, bsz=bsz, seq=seq, tl=tl, tk=tq)
    o2d = _attention(qt, k2d, vt, bsz=bsz, seq=seq)
    x3, hlin, logt = _oproj(o2d, x2, mod[1], o_w[0].astype(BF16), ln_g[1, 0], ln_b[1, 0],
                            rwt[1], router_b[1], seq=seq, tl=tl, alpha=alpha)
    x4 = _moe(hlin, logt, x3, mod[1], ln_g[1, 1], ln_b[1, 1], moe_w_in, moe_b_in,
              moe_w_out, moe_b_out, layer=1, seq=seq, tt=tt, rb=rb, alpha=alpha)
    return x4.reshape(bsz, seq, d)
```
